```python
import jax, jax.numpy as jnp
from jax import lax
import numpy as np

D_MODEL = 1024
BATCH = 1
SEQ = 16384
DEPTH = 2
DEC_BATCH = 32
DEC_SEQ = 64
PAST_LEN = 1024

CHUNK = 64
D_CONV = D_MODEL // 2
CONV_W = 3
D_RWKV = D_MODEL // 2
HEAD_SIZE = 64
N_HEADS = D_RWKV // HEAD_SIZE
D_DECAY_LORA = 64
D_AAA_LORA = 64
D_GATE_LORA = 128
LN_X_EPS = 64e-5
D_FF = ((8 * D_MODEL // 3 + 255) // 256) * 256
RMS_EPS = 1e-6

OFF_CONV_X = 0
OFF_CONV_B = D_CONV
OFF_CONV_C = 2 * D_CONV
OFF_RWKV = 3 * D_CONV
RWKV_COLS = 3 * D_RWKV + D_DECAY_LORA + D_AAA_LORA + D_GATE_LORA
OFF_GATE = OFF_RWKV + RWKV_COLS
IN_COLS = OFF_GATE + 2 * D_MODEL
R_R = 0
R_K = D_RWKV
R_V = 2 * D_RWKV
R_W = 3 * D_RWKV
R_A = R_W + D_DECAY_LORA
R_G = R_A + D_AAA_LORA

kernel_name = "hybrid_conv_rwkv7_streaming_step"


def _rms_norm(x, g):
    xf = x.astype(jnp.float32)
    y = xf * lax.rsqrt(jnp.mean(xf * xf, axis=-1, keepdims=True) + RMS_EPS)
    return (y * g.astype(jnp.float32)).astype(x.dtype)


def _wkv_scan(s0, r, w, k, v, a, b):
    def step(s, inp):
        r_t, w_t, k_t, v_t, a_t, b_t = inp
        sa = jnp.einsum('bhij,bhj->bhi', s, a_t)
        s = s * w_t[:, :, None, :] + sa[..., None] * b_t[:, :, None, :] + v_t[..., None] * k_t[:, :, None, :]
        y = jnp.einsum('bhij,bhj->bhi', s, r_t)
        return s, y
    xs = tuple(jnp.moveaxis(t, 1, 0) for t in (r, w, k, v, a, b))
    s_fin, ys = lax.scan(step, s0, xs)
    return jnp.moveaxis(ys, 0, 1), s_fin


def _mixer(h, conv_state, shift_state, wkv_state, p):
    bsz, t_len, _ = h.shape
    proj = jnp.einsum('btd,dc->btc', h, p["w_in"])

    xin = proj[..., OFF_CONV_X:OFF_CONV_X + D_CONV]
    bg = proj[..., OFF_CONV_B:OFF_CONV_B + D_CONV]
    cg = proj[..., OFF_CONV_C:OFF_CONV_C + D_CONV]
    u = cg * xin
    u_pad = jnp.concatenate([conv_state.astype(u.dtype), u], axis=1)
    cw = p["conv_w"]
    y_conv = cw[0] * u_pad[:, 0:t_len] + cw[1] * u_pad[:, 1:t_len + 1] + cw[2] * u_pad[:, 2:t_len + 2]
    new_conv = u_pad[:, -(CONV_W - 1):]
    branch_a = jnp.einsum('btc,cd->btd', bg * y_conv, p["w_conv_out"])

    pr = proj[..., OFF_RWKV:OFF_RWKV + RWKV_COLS]
    prev = jnp.concatenate([shift_state.astype(pr.dtype)[:, None], pr[:, :-1]], axis=1)
    ps = pr + (prev - pr) * p["mu_shift"]
    new_shift = pr[:, -1]
    r = ps[..., R_R:R_R + D_RWKV]
    k = ps[..., R_K:R_K + D_RWKV]
    v = ps[..., R_V:R_V + D_RWKV]
    wl = ps[..., R_W:R_W + D_DECAY_LORA]
    al = ps[..., R_A:R_A + D_AAA_LORA]
    gl = ps[..., R_G:R_G + D_GATE_LORA]

    w_raw = (p["w_decay0"] + jnp.tanh(wl) @ p["w_decay2"]).astype(jnp.float32)
    w_raw = -jax.nn.softplus(-w_raw) - 0.5
    decay = jnp.exp(-jnp.exp(w_raw))
    a = jax.nn.sigmoid((p["a0"] + al @ p["a2"]).astype(jnp.float32))
    g = jax.nn.sigmoid(gl) @ p["g2"]

    hs = (bsz, t_len, N_HEADS, HEAD_SIZE)
    rf = r.astype(jnp.float32).reshape(hs)
    vf = v.astype(jnp.float32).reshape(hs)
    kf = k.astype(jnp.float32)
    kk = (kf * p["k_k"].astype(jnp.float32)).reshape(hs)
    kk = kk / jnp.maximum(jnp.sqrt(jnp.sum(kk * kk, axis=-1, keepdims=True)), 1e-12)
    kf = (kf * (1.0 + (a - 1.0) * p["k_a"].astype(jnp.float32))).reshape(hs)
    a_h = a.reshape(hs)
    y, s_fin = _wkv_scan(wkv_state.astype(jnp.float32), rf, decay.reshape(hs), kf, vf, -kk, kk * a_h)

    mu = jnp.mean(y, axis=-1, keepdims=True)
    var = jnp.mean(jnp.square(y - mu), axis=-1, keepdims=True)
    yn = ((y - mu) * lax.rsqrt(var + LN_X_EPS)).reshape(bsz, t_len, D_RWKV)
    yn = yn * p["ln_x_w"].astype(jnp.float32) + p["ln_x_b"].astype(jnp.float32)
    bonus = jnp.sum(rf * kf * p["r_k"].astype(jnp.float32), axis=-1, keepdims=True) * vf
    y_r = (yn + bonus.reshape(bsz, t_len, D_RWKV)).astype(h.dtype)
    branch_b = jnp.einsum('btc,cd->btd', y_r * g, p["w_rwkv_out"])

    gates = jax.nn.sigmoid(proj[..., OFF_GATE:OFF_GATE + 2 * D_MODEL])
    merged = gates[..., :D_MODEL] * branch_a + gates[..., D_MODEL:] * branch_b
    out = jnp.einsum('btd,de->bte', merged, p["w_o"])
    return out, new_conv, new_shift, s_fin


def _layer(x, conv_state, shift_state, wkv_state, p):
    m, new_conv, new_shift, new_wkv = _mixer(_rms_norm(x, p["norm_mix_pre"]), conv_state, shift_state, wkv_state, p)
    x = x + _rms_norm(m, p["norm_mix_post"])
    h = _rms_norm(x, p["norm_ffn_pre"])
    up = jnp.einsum('btd,df->btf', h, p["w_ffn_up"])
    f = jnp.einsum('btf,fd->btd', jax.nn.silu(up[..., :D_FF]) * up[..., D_FF:], p["w_ffn_down"])
    x = x + _rms_norm(f, p["norm_ffn_post"])
    return x, new_conv.astype(x.dtype), new_shift.astype(x.dtype), new_wkv.astype(x.dtype)


def setup_inputs(seed: int = 0) -> dict:
    key = jax.random.key(seed)
    ks = jax.random.split(key, 32)
    nrm = lambda i, shape, s: jax.random.normal(ks[i], shape, jnp.float32) * s
    L = DEPTH
    return {
        "x_prompt": nrm(0, (BATCH, SEQ, D_MODEL), 1.0),
        "x_sample": nrm(1, (DEC_BATCH, DEC_SEQ, D_MODEL), 1.0),
        "state_conv": nrm(2, (L, DEC_BATCH, CONV_W - 1, D_CONV), 1.0),
        "state_shift": nrm(3, (L, DEC_BATCH, RWKV_COLS), 1.0),
        "state_wkv": nrm(4, (L, DEC_BATCH, N_HEADS, HEAD_SIZE, HEAD_SIZE), 0.3),
        "norm_mix_pre": 1.0 + nrm(5, (L, D_MODEL), 0.05),
        "norm_mix_post": 1.0 + nrm(6, (L, D_MODEL), 0.05),
        "w_in": nrm(7, (L, D_MODEL, IN_COLS), D_MODEL ** -0.5),
        "mu_shift": jax.random.uniform(ks[8], (L, RWKV_COLS), jnp.float32, 0.1, 0.9),
        "conv_w": nrm(9, (L, CONV_W, D_CONV), CONV_W ** -0.5),
        "w_decay0": nrm(10, (L, D_RWKV), 0.5) - 0.5,
        "w_decay2": nrm(11, (L, D_DECAY_LORA, D_RWKV), 0.1),
        "a0": nrm(12, (L, D_RWKV), 0.1),
        "a2": nrm(13, (L, D_AAA_LORA, D_RWKV), 0.5 * D_AAA_LORA ** -0.5),
        "g2": nrm(14, (L, D_GATE_LORA, D_RWKV), D_GATE_LORA ** -0.5),
        "k_k": 0.85 + nrm(15, (L, D_RWKV), 0.05),
        "k_a": 1.0 + nrm(16, (L, D_RWKV), 0.05),
        "r_k": nrm(17, (L, N_HEADS, HEAD_SIZE), 0.1),
        "ln_x_w": 1.0 + nrm(18, (L, D_RWKV), 0.05),
        "ln_x_b": nrm(19, (L, D_RWKV), 0.01),
        "w_conv_out": nrm(20, (L, D_CONV, D_MODEL), D_CONV ** -0.5),
        "w_rwkv_out": nrm(21, (L, D_RWKV, D_MODEL), D_RWKV ** -0.5),
        "w_o": nrm(22, (L, D_MODEL, D_MODEL), D_MODEL ** -0.5),
        "norm_ffn_pre": 1.0 + nrm(23, (L, D_MODEL), 0.05),
        "norm_ffn_post": 1.0 + nrm(24, (L, D_MODEL), 0.05),
        "w_ffn_up": nrm(25, (L, D_MODEL, 2 * D_FF), D_MODEL ** -0.5),
        "w_ffn_down": nrm(26, (L, D_FF, D_MODEL), D_FF ** -0.5),
    }


def reference(x_prompt, x_sample, state_conv, state_shift, state_wkv,
              norm_mix_pre, norm_mix_post, w_in, mu_shift, conv_w,
              w_decay0, w_decay2, a0, a2, g2, k_k, k_a, r_k, ln_x_w, ln_x_b,
              w_conv_out, w_rwkv_out, w_o, norm_ffn_pre, norm_ffn_post,
              w_ffn_up, w_ffn_down):
    dt = x_prompt.dtype
    xp, xs = x_prompt, x_sample
    conv_p, shift_p, wkv_p, conv_s, shift_s, wkv_s = [], [], [], [], [], []
    for l in range(DEPTH):
        p = dict(norm_mix_pre=norm_mix_pre[l], norm_mix_post=norm_mix_post[l], w_in=w_in[l],
                 mu_shift=mu_shift[l], conv_w=conv_w[l], w_decay0=w_decay0[l], w_decay2=w_decay2[l],
                 a0=a0[l], a2=a2[l], g2=g2[l], k_k=k_k[l], k_a=k_a[l], r_k=r_k[l],
                 ln_x_w=ln_x_w[l], ln_x_b=ln_x_b[l], w_conv_out=w_conv_out[l],
                 w_rwkv_out=w_rwkv_out[l], w_o=w_o[l], norm_ffn_pre=norm_ffn_pre[l],
                 norm_ffn_post=norm_ffn_post[l], w_ffn_up=w_ffn_up[l], w_ffn_down=w_ffn_down[l])
        xp, c1, s1, k1 = _layer(xp,
                                jnp.zeros((BATCH, CONV_W - 1, D_CONV), dt),
                                jnp.zeros((BATCH, RWKV_COLS), dt),
                                jnp.zeros((BATCH, N_HEADS, HEAD_SIZE, HEAD_SIZE), dt), p)
        xs, c2, s2, k2 = _layer(xs, state_conv[l], state_shift[l], state_wkv[l], p)
        conv_p.append(c1); shift_p.append(s1); wkv_p.append(k1)
        conv_s.append(c2); shift_s.append(s2); wkv_s.append(k2)
    return (xp, xs,
            jnp.stack(conv_p, 0), jnp.stack(shift_p, 0), jnp.stack(wkv_p, 0),
            jnp.stack(conv_s, 0), jnp.stack(shift_s, 0), jnp.stack(wkv_s, 0))
```

```python
import functools

import jax
import jax.numpy as jnp
from jax import lax
from jax.experimental import pallas as pl
from jax.experimental.pallas import tpu as pltpu

D_MODEL = 1024
SEQ = 16384
DEPTH = 2
DEC_BATCH = 32
DEC_SEQ = 64
D_CONV = 512
D_RWKV = 512
HEAD_SIZE = 64
N_HEADS = 8
D_DECAY_LORA = 64
D_AAA_LORA = 64
D_GATE_LORA = 128
D_LORA = D_DECAY_LORA + D_AAA_LORA + D_GATE_LORA
LN_X_EPS = 64e-5
D_FF = 2816
RMS_EPS = 1e-6
RWKV_COLS = 3 * D_RWKV + D_LORA
MIX_COLS = 3 * D_CONV + RWKV_COLS
GATE_COLS = 2 * D_MODEL

CHUNK = 64
N_TOK = SEQ + DEC_BATCH * DEC_SEQ
N_CHUNKS = N_TOK // CHUNK
N_PROMPT_CHUNKS = SEQ // CHUNK
N_SEQ = 1 + DEC_BATCH
PAIR = 2 * HEAD_SIZE
N_PAIRS = N_HEADS // 2
TOKEN_TILE = 256
VMEM_LIMIT_BYTES = 56 * 1024 * 1024

_BF16 = jnp.bfloat16
_F32 = jnp.float32
_NN = (((1,), (0,)), ((), ()))
_NT = (((1,), (1,)), ((), ()))
_TN = (((0,), (0,)), ((), ()))


def _dot(a, b, dims=_NN):
    return lax.dot_general(a, b, dims, preferred_element_type=_F32)


def _split2(x):
    hi = x.astype(_BF16)
    lo = (x - hi.astype(_F32)).astype(_BF16)
    return hi, lo


def _split3(x):
    hi = x.astype(_BF16)
    r1 = x - hi.astype(_F32)
    mid = r1.astype(_BF16)
    lo = (r1 - mid.astype(_F32)).astype(_BF16)
    return hi, mid, lo


def _dot_f32(a, b, dims=_NN):
    ah, al = _split2(a)
    bh, bl = _split2(b)
    return _dot(ah, bh, dims) + (_dot(ah, bl, dims) + _dot(al, bh, dims))


def _dot_exact_rhs(a, b_bf16):
    hi, mid, lo = _split3(a)
    return _dot(hi, b_bf16) + (_dot(mid, b_bf16) + _dot(lo, b_bf16))


def _dot_split_rhs(a, b_hi, b_lo):
    ah, al = _split2(a)
    return _dot(ah, b_hi) + (_dot(ah, b_lo) + _dot(al, b_hi))


def _rms_scale(x):
    return x * lax.rsqrt(jnp.mean(x * x, axis=-1, keepdims=True) + RMS_EPS)


def _iota(shape, dim):
    return lax.broadcasted_iota(jnp.int32, shape, dim)


def _head_sum_matrix():
    r = _iota((D_RWKV, D_RWKV), 0) // HEAD_SIZE
    c = _iota((D_RWKV, D_RWKV), 1) // HEAD_SIZE
    return jnp.where(r == c, 1.0, 0.0).astype(_BF16)


def _seq_of_chunk(c):
    return jnp.maximum(c - (N_PROMPT_CHUNKS - 1), 0)


def _proj_kernel(x_ref, g_ref, w_ref, mix_ref, gate_ref):
    h = (_rms_scale(x_ref[...]) * g_ref[...]).astype(_BF16)
    p = _dot(h, w_ref[...])
    mix_ref[...] = p[:, :MIX_COLS]
    gate_ref[...] = p[:, MIX_COLS:]


def _proj(x, g, w_in_bf16):
    n_in = MIX_COLS + GATE_COLS
    return pl.pallas_call(
        _proj_kernel,
        grid=(N_TOK // TOKEN_TILE,),
        in_specs=[
            pl.BlockSpec((TOKEN_TILE, D_MODEL), lambda i: (i, 0)),
            pl.BlockSpec((1, D_MODEL), lambda i: (0, 0)),
            pl.BlockSpec((D_MODEL, n_in), lambda i: (0, 0)),
        ],
        out_specs=[
            pl.BlockSpec((TOKEN_TILE, MIX_COLS), lambda i: (i, 0)),
            pl.BlockSpec((TOKEN_TILE, GATE_COLS), lambda i: (i, 0)),
        ],
        out_shape=[
            jax.ShapeDtypeStruct((N_TOK, MIX_COLS), _F32),
            jax.ShapeDtypeStruct((N_TOK, GATE_COLS), _F32),
        ],
        compiler_params=pltpu.CompilerParams(
            dimension_semantics=("arbitrary",), vmem_limit_bytes=VMEM_LIMIT_BYTES),
        name="proj",
    )(x, g, w_in_bf16)


def _prep_kernel(mix_ref, conv0_ref, shift0_ref, mu_ref, cw_ref, lora_hi_ref, lora_lo_ref,
                 wd0_ref, a0_ref, kk_w_ref, ka_w_ref,
                 bgy_ref, r_ref, k_ref, v_ref, lw_ref, kk_ref, asig_ref, g_ref,
                 conv_out_ref, shift_out_ref,
                 conv_carry, shift_carry):
    c = pl.program_id(0)
    is_start = jnp.logical_or(c == 0, c >= N_PROMPT_CHUNKS)

    @pl.when(is_start)
    def _():
        conv_carry[...] = conv0_ref[0]
        shift_carry[...] = shift0_ref[0]

    row = _iota((CHUNK, 1), 0)

    xin = mix_ref[:, 0:D_CONV]
    bg = mix_ref[:, D_CONV:2 * D_CONV]
    cg = mix_ref[:, 2 * D_CONV:3 * D_CONV]
    u = cg * xin
    hist = conv_carry[...]
    u1 = jnp.where(row == 0, hist[1:2], pltpu.roll(u, 1, axis=0))
    u2 = pltpu.roll(u, 2, axis=0)
    u2 = jnp.where(row == 0, hist[0:1], jnp.where(row == 1, hist[1:2], u2))
    cw = cw_ref[...]
    y_conv = cw[0:1] * u2 + cw[1:2] * u1 + cw[2:3] * u
    bgy_ref[...] = bg * y_conv
    new_conv = u[CHUNK - 2:CHUNK]
    conv_carry[...] = new_conv
    conv_out_ref[0] = new_conv

    pr = mix_ref[:, 3 * D_CONV:MIX_COLS]
    prev = jnp.where(row == 0, shift_carry[...], pltpu.roll(pr, 1, axis=0))
    ps = pr + (prev - pr) * mu_ref[...]
    new_shift = pr[CHUNK - 1:CHUNK]
    shift_carry[...] = new_shift
    shift_out_ref[0] = new_shift

    r = ps[:, 0:D_RWKV]
    k = ps[:, D_RWKV:2 * D_RWKV]
    v = ps[:, 2 * D_RWKV:3 * D_RWKV]
    lora_in = ps[:, 3 * D_RWKV:RWKV_COLS]
    lane = _iota((CHUNK, D_LORA), 1)
    act = jnp.where(lane < D_DECAY_LORA, jnp.tanh(lora_in),
                    jnp.where(lane < D_DECAY_LORA + D_AAA_LORA, lora_in, jax.nn.sigmoid(lora_in)))
    lora = _dot_split_rhs(act, lora_hi_ref[...], lora_lo_ref[...])
    w_raw = wd0_ref[...] + lora[:, 0:D_RWKV]
    w_raw = -jax.nn.softplus(-w_raw) - 0.5
    lw_ref[...] = -jnp.exp(w_raw)
    a_sig = jax.nn.sigmoid(a0_ref[...] + lora[:, D_RWKV:2 * D_RWKV])
    g_ref[...] = lora[:, 2 * D_RWKV:3 * D_RWKV]

    kk = k * kk_w_ref[...]
    ss = _dot_exact_rhs(kk * kk, _head_sum_matrix())
    kk_ref[...] = kk / jnp.maximum(jnp.sqrt(ss), 1e-12)
    k_ref[...] = k * (1.0 + (a_sig - 1.0) * ka_w_ref[...])
    asig_ref[...] = a_sig
    r_ref[...] = r
    v_ref[...] = v


def _prep(mix, conv0, shift0, mu, cw, lora_hi, lora_lo, wd0, a0, kk_w, ka_w):
    tok = lambda width: pl.BlockSpec((CHUNK, width), lambda c: (c, 0))
    full = lambda a: pl.BlockSpec(a.shape, lambda c: (0,) * a.ndim)
    seq3 = lambda a: pl.BlockSpec((1,) + a.shape[1:], lambda c: (_seq_of_chunk(c), 0, 0))
    tok_out = jax.ShapeDtypeStruct((N_TOK, D_RWKV), _F32)
    return pl.pallas_call(
        _prep_kernel,
        grid=(N_CHUNKS,),
        in_specs=[tok(MIX_COLS), seq3(conv0), seq3(shift0), full(mu), full(cw), full(lora_hi),
                  full(lora_lo), full(wd0), full(a0), full(kk_w), full(ka_w)],
        out_specs=[tok(D_CONV)] + [tok(D_RWKV)] * 7 + [
            pl.BlockSpec((1, 2, D_CONV), lambda c: (c, 0, 0)),
            pl.BlockSpec((1, 1, RWKV_COLS), lambda c: (c, 0, 0)),
        ],
        out_shape=[jax.ShapeDtypeStruct((N_TOK, D_CONV), _F32)] + [tok_out] * 7 + [
            jax.ShapeDtypeStruct((N_CHUNKS, 2, D_CONV), _F32),
            jax.ShapeDtypeStruct((N_CHUNKS, 1, RWKV_COLS), _F32),
        ],
        scratch_shapes=[pltpu.VMEM((2, D_CONV), _F32), pltpu.VMEM((1, RWKV_COLS), _F32)],
        compiler_params=pltpu.CompilerParams(
            dimension_semantics=("arbitrary",), vmem_limit_bytes=VMEM_LIMIT_BYTES),
        name="prep",
    )(mix, conv0, shift0, mu, cw, lora_hi, lora_lo, wd0, a0, kk_w, ka_w)


def _wkv_kernel(r_ref, k_ref, v_ref, lw_ref, kk_ref, asig_ref, g_ref, s0_ref,
                rk_w_ref, lnw_ref, lnb_ref,
                z_ref, s_out_ref, state):
    c = pl.program_id(0)
    is_start = jnp.logical_or(c == 0, c >= N_PROMPT_CHUNKS)

    @pl.when(is_start)
    def _():
        state[...] = s0_ref[0]

    lw = lw_ref[...]
    tri = jnp.where(_iota((CHUNK, CHUNK), 1) <= _iota((CHUNK, CHUNK), 0), 1.0, 0.0).astype(_BF16)
    cum = _dot_exact_rhs_left(tri, lw)
    last = cum[CHUNK - 1:CHUNK]
    e_inv = jnp.exp(-cum)
    e_tail = jnp.exp(last - cum)
    kk = kk_ref[...]
    kf = k_ref[...]
    v = v_ref[...]
    r = r_ref[...]
    b = kk * asig_ref[...]
    a_t = -kk * jnp.exp(cum - lw)
    r_t = r * jnp.exp(cum)
    b_t = b * e_inv
    k_t = kf * e_inv
    b_h = b * e_tail
    k_h = kf * e_tail
    dec = jnp.exp(last)

    lane = _iota((1, PAIR), 1)
    first = lane < HEAD_SIZE
    g_row = _iota((4 * CHUNK, PAIR), 0)
    t_idx = g_row % CHUNK
    incl = (g_row // CHUNK) % 2
    s_idx = _iota((4 * CHUNK, PAIR), 1) % CHUNK
    keep = s_idx < t_idx + incl
    eye = _iota((PAIR, PAIR), 0) == _iota((PAIR, PAIR), 1)
    same_head = (_iota((PAIR, PAIR), 0) // HEAD_SIZE) == (_iota((PAIR, PAIR), 1) // HEAD_SIZE)

    ys = []
    for p in range(N_PAIRS):
        sl = slice(p * PAIR, (p + 1) * PAIR)
        w_p = state[p]
        ar = jnp.concatenate([a_t[:, sl], r_t[:, sl]], axis=0)
        bk = jnp.concatenate([b_t[:, sl], k_t[:, sl]], axis=0)
        ar_split = jnp.concatenate([jnp.where(first, ar, 0.0), jnp.where(first, 0.0, ar)], axis=0)
        gm = jnp.where(keep, _dot_f32(ar_split, bk, _NT), 0.0)
        l0, rr0 = gm[0:CHUNK], gm[CHUNK:2 * CHUNK]
        l1, rr1 = gm[2 * CHUNK:3 * CHUNK], gm[3 * CHUNK:4 * CHUNK]
        ss = _dot_f32(ar, w_p)

        l_bd = jnp.concatenate([jnp.where(first, l0, 0.0),
                                jnp.where(first, 0.0, pltpu.roll(l1, HEAD_SIZE, axis=1))], axis=0)
        inv = jnp.where(eye, 1.0, 0.0) + l_bd
        pw = l_bd
        for _ in range(5):
            pw = _dot_f32(pw, pw)
            inv = inv + _dot_f32(inv, pw)

        v_p = v[:, sl]
        vv = jnp.concatenate([v_p, v_p], axis=0)
        rhs = ss[0:CHUNK] + jnp.where(first, _dot_f32(jnp.where(first, 0.0, l0), vv),
                                      _dot_f32(jnp.where(first, 0.0, l1), vv))
        rhs_split = jnp.concatenate([jnp.where(first, rhs, 0.0), jnp.where(first, 0.0, rhs)], axis=0)
        u2 = _dot_f32(inv, rhs_split)
        u_p = u2[0:CHUNK] + u2[CHUNK:2 * CHUNK]
        uv = jnp.concatenate([u_p, v_p], axis=0)
        ys.append(ss[CHUNK:2 * CHUNK] + jnp.where(first, _dot_f32(rr0, uv), _dot_f32(rr1, uv)))

        bk_h = jnp.concatenate([b_h[:, sl], k_h[:, sl]], axis=0)
        dec_col = jnp.sum(jnp.where(eye, dec[:, sl], 0.0), axis=1, keepdims=True)
        w_new = w_p * dec_col + jnp.where(same_head, _dot_f32(bk_h, uv, _TN), 0.0)
        state[p] = w_new
        s_out_ref[0, p] = w_new

    y = jnp.concatenate(ys, axis=1)
    hs = _head_sum_matrix()
    inv_n = 1.0 / HEAD_SIZE
    mu = _dot_exact_rhs(y, hs) * inv_n
    d = y - mu
    var = _dot_exact_rhs(d * d, hs) * inv_n
    yn = d * lax.rsqrt(var + LN_X_EPS) * lnw_ref[...] + lnb_ref[...]
    bonus = _dot_exact_rhs(r * kf * rk_w_ref[...], hs) * v
    z_ref[...] = (yn + bonus) * g_ref[...]


def _dot_exact_rhs_left(m_bf16, x):
    hi, mid, lo = _split3(x)
    return _dot(m_bf16, hi) + (_dot(m_bf16, mid) + _dot(m_bf16, lo))


def _wkv(r, k, v, lw, kk, asig, g, s0, rk_w, lnw, lnb):
    tok = pl.BlockSpec((CHUNK, D_RWKV), lambda c: (c, 0))
    row = pl.BlockSpec((1, D_RWKV), lambda c: (0, 0))
    st = pl.BlockSpec((1, N_PAIRS, PAIR, PAIR), lambda c: (_seq_of_chunk(c), 0, 0, 0))
    return pl.pallas_call(
        _wkv_kernel,
        grid=(N_CHUNKS,),
        in_specs=[tok] * 7 + [st, row, row, row],
        out_specs=[tok, st],
        out_shape=[jax.ShapeDtypeStruct((N_TOK, D_RWKV), _F32),
                   jax.ShapeDtypeStruct((N_SEQ, N_PAIRS, PAIR, PAIR), _F32)],
        scratch_shapes=[pltpu.VMEM((N_PAIRS, PAIR, PAIR), _F32)],
        compiler_params=pltpu.CompilerParams(
            dimension_semantics=("arbitrary",), vmem_limit_bytes=VMEM_LIMIT_BYTES),
        name="wkv",
    )(r, k, v, lw, kk, asig, g, s0, rk_w, lnw, lnb)


def _merge_kernel(x_ref, bgy_ref, z_ref, gate_ref, wc_ref, wr_ref, wo_ref, g_ref, o_ref):
    branch_a = _dot(bgy_ref[...].astype(_BF16), wc_ref[...])
    branch_b = _dot(z_ref[...].astype(_BF16), wr_ref[...])
    merged = (jax.nn.sigmoid(gate_ref[:, 0:D_MODEL]) * branch_a
              + jax.nn.sigmoid(gate_ref[:, D_MODEL:GATE_COLS]) * branch_b)
    m = _dot(merged.astype(_BF16), wo_ref[...])
    o_ref[...] = x_ref[...] + _rms_scale(m) * g_ref[...]


def _merge(x, bgy, z, gates, wc, wr, wo, g):
    tok = lambda width: pl.BlockSpec((TOKEN_TILE, width), lambda i: (i, 0))
    full = lambda a: pl.BlockSpec(a.shape, lambda i: (0,) * a.ndim)
    return pl.pallas_call(
        _merge_kernel,
        grid=(N_TOK // TOKEN_TILE,),
        in_specs=[tok(D_MODEL), tok(D_CONV), tok(D_RWKV), tok(GATE_COLS),
                  full(wc), full(wr), full(wo), full(g)],
        out_specs=tok(D_MODEL),
        out_shape=jax.ShapeDtypeStruct((N_TOK, D_MODEL), _F32),
        compiler_params=pltpu.CompilerParams(
            dimension_semantics=("arbitrary",), vmem_limit_bytes=VMEM_LIMIT_BYTES),
        name="merge",
    )(x, bgy, z, gates, wc, wr, wo, g)


def _ffn_kernel(x_ref, gpre_ref, wup_ref, wdown_ref, gpost_ref, o_ref):
    x = x_ref[...]
    h = (_rms_scale(x) * gpre_ref[...]).astype(_BF16)
    up = _dot(h, wup_ref[...])
    act = jax.nn.silu(up[:, :D_FF]) * up[:, D_FF:]
    f = _dot(act.astype(_BF16), wdown_ref[...])
    o_ref[...] = x + _rms_scale(f) * gpost_ref[...]


def _ffn(x, gpre, wup, wdown, gpost):
    tok = pl.BlockSpec((TOKEN_TILE, D_MODEL), lambda i: (i, 0))
    full = lambda a: pl.BlockSpec(a.shape, lambda i: (0,) * a.ndim)
    return pl.pallas_call(
        _ffn_kernel,
        grid=(N_TOK // TOKEN_TILE,),
        in_specs=[tok, full(gpre), full(wup), full(wdown), full(gpost)],
        out_specs=tok,
        out_shape=jax.ShapeDtypeStruct((N_TOK, D_MODEL), _F32),
        compiler_params=pltpu.CompilerParams(
            dimension_semantics=("arbitrary",), vmem_limit_bytes=VMEM_LIMIT_BYTES),
        name="ffn",
    )(x, gpre, wup, wdown, gpost)


def _state_to_pairs(s):
    st = jnp.swapaxes(s, -1, -2).reshape(s.shape[0], N_PAIRS, 2, HEAD_SIZE, HEAD_SIZE)
    eye2 = jnp.eye(2, dtype=s.dtype)
    out = st[:, :, :, :, None, :] * eye2[None, None, :, None, :, None]
    return out.reshape(s.shape[0], N_PAIRS, PAIR, PAIR)


def _pairs_to_state(w):
    w6 = w.reshape(w.shape[0], N_PAIRS, 2, HEAD_SIZE, 2, HEAD_SIZE)
    diag = jnp.stack([w6[:, :, 0, :, 0, :], w6[:, :, 1, :, 1, :]], axis=2)
    return jnp.swapaxes(diag, -1, -2).reshape(w.shape[0], N_HEADS, HEAD_SIZE, HEAD_SIZE)


def _lora_weight(w_decay2, a2, g2):
    w = jnp.zeros((D_LORA, 3 * D_RWKV), _F32)
    w = w.at[0:D_DECAY_LORA, 0:D_RWKV].set(w_decay2)
    w = w.at[D_DECAY_LORA:D_DECAY_LORA + D_AAA_LORA, D_RWKV:2 * D_RWKV].set(a2)
    w = w.at[D_DECAY_LORA + D_AAA_LORA:, 2 * D_RWKV:].set(g2)
    hi = w.astype(_BF16)
    lo = (w - hi.astype(_F32)).astype(_BF16)
    return hi, lo


def kernel(x_prompt, x_sample, state_conv, state_shift, state_wkv, norm_mix_pre, norm_mix_post, w_in, mu_shift, conv_w, w_decay0, w_decay2, a0, a2, g2, k_k, k_a, r_k, ln_x_w, ln_x_b, w_conv_out, w_rwkv_out, w_o, norm_ffn_pre, norm_ffn_post, w_ffn_up, w_ffn_down):
    x = jnp.concatenate([x_prompt.reshape(SEQ, D_MODEL),
                         x_sample.reshape(DEC_BATCH * DEC_SEQ, D_MODEL)], axis=0)
    row = lambda a: a.reshape(1, -1)
    conv_p, shift_p, wkv_p, conv_s, shift_s, wkv_s = [], [], [], [], [], []
    for l in range(DEPTH):
        conv0 = jnp.concatenate([jnp.zeros((1, 2, D_CONV), _F32), state_conv[l]], axis=0)
        shift0 = jnp.concatenate([jnp.zeros((1, 1, RWKV_COLS), _F32), state_shift[l][:, None]], axis=0)
        s0 = jnp.concatenate([jnp.zeros((1, N_PAIRS, PAIR, PAIR), _F32),
                              _state_to_pairs(state_wkv[l])], axis=0)
        lora_hi, lora_lo = _lora_weight(w_decay2[l], a2[l], g2[l])

        mix, gates = _proj(x, row(norm_mix_pre[l]), w_in[l].astype(_BF16))
        (bgy, r, k, v, lw, kk, asig, g, conv_out, shift_out) = _prep(
            mix, conv0, shift0, row(mu_shift[l]), conv_w[l], lora_hi, lora_lo,
            row(w_decay0[l]), row(a0[l]), row(k_k[l]), row(k_a[l]))
        z, s_out = _wkv(r, k, v, lw, kk, asig, g, s0,
                        row(r_k[l]), row(ln_x_w[l]), row(ln_x_b[l]))
        x = _merge(x, bgy, z, gates, w_conv_out[l].astype(_BF16), w_rwkv_out[l].astype(_BF16),
                   w_o[l].astype(_BF16), row(norm_mix_post[l]))
        x = _ffn(x, row(norm_ffn_pre[l]), w_ffn_up[l].astype(_BF16), w_ffn_down[l].astype(_BF16),
                 row(norm_ffn_post[l]))

        last = N_PROMPT_CHUNKS - 1
        conv_p.append(conv_out[last:last + 1])
        conv_s.append(conv_out[N_PROMPT_CHUNKS:])
        shift_p.append(shift_out[last:last + 1, 0])
        shift_s.append(shift_out[N_PROMPT_CHUNKS:, 0])
        states = _pairs_to_state(s_out)
        wkv_p.append(states[0:1])
        wkv_s.append(states[1:])
    return (x[:SEQ].reshape(1, SEQ, D_MODEL),
            x[SEQ:].reshape(DEC_BATCH, DEC_SEQ, D_MODEL),
            jnp.stack(conv_p, 0), jnp.stack(shift_p, 0), jnp.stack(wkv_p, 0),
            jnp.stack(conv_s, 0), jnp.stack(shift_s, 0), jnp.stack(wkv_s, 0))
```

```python
import jax
import jax.numpy as jnp
from jax import lax
from jax.experimental import pallas as pl
from jax.experimental.pallas import tpu as pltpu

D_MODEL = 1024
SEQ = 16384
DEPTH = 2
DEC_BATCH = 32
DEC_SEQ = 64
D_CONV = 512
D_RWKV = 512
HEAD_SIZE = 64
N_HEADS = 8
D_DECAY_LORA = 64
D_AAA_LORA = 64
D_GATE_LORA = 128
D_LORA = D_DECAY_LORA + D_AAA_LORA + D_GATE_LORA
LN_X_EPS = 64e-5
D_FF = 2816
RMS_EPS = 1e-6
RWKV_COLS = 3 * D_RWKV + D_LORA
MIX_COLS = 3 * D_CONV + RWKV_COLS
GATE_COLS = 2 * D_MODEL

CHUNK = 64
N_TOK = SEQ + DEC_BATCH * DEC_SEQ
N_CHUNKS = N_TOK // CHUNK
N_PROMPT_CHUNKS = SEQ // CHUNK
PAIR = 2 * HEAD_SIZE
N_PAIRS = N_HEADS // 2
TOKEN_TILE = 256
INTRA_CHUNKS = 2
SCAN_CHUNKS = 8
SUBLANES = 8
VMEM_LIMIT_BYTES = 56 * 1024 * 1024

_BF16 = jnp.bfloat16
_F32 = jnp.float32
_NN = (((1,), (0,)), ((), ()))
_NT = (((1,), (1,)), ((), ()))
_TN = (((0,), (0,)), ((), ()))


def _dot(a, b, dims=_NN):
    return lax.dot_general(a, b, dims, preferred_element_type=_F32)


def _split2(x):
    hi = x.astype(_BF16)
    lo = (x - hi.astype(_F32)).astype(_BF16)
    return hi, lo


def _split3(x):
    hi = x.astype(_BF16)
    r1 = x - hi.astype(_F32)
    mid = r1.astype(_BF16)
    lo = (r1 - mid.astype(_F32)).astype(_BF16)
    return hi, mid, lo


def _dot_pieces(a, b, dims=_NN):
    return _dot(a[0], b[0], dims) + (_dot(a[0], b[1], dims) + _dot(a[1], b[0], dims))


def _dot_f32(a, b, dims=_NN):
    return _dot_pieces(_split2(a), _split2(b), dims)


def _rms_scale(x):
    return x * lax.rsqrt(jnp.mean(x * x, axis=-1, keepdims=True) + RMS_EPS)


def _iota(shape, dim):
    return lax.broadcasted_iota(jnp.int32, shape, dim)


def _first_head(width=PAIR):
    return _iota((1, width), 1) % PAIR < HEAD_SIZE


def _head_sums(x):
    first = _first_head()
    outs = []
    for p in range(x.shape[1] // PAIR):
        xp = x[:, p * PAIR:(p + 1) * PAIR]
        lo = jnp.sum(jnp.where(first, xp, 0.0), axis=1, keepdims=True)
        hi = jnp.sum(jnp.where(first, 0.0, xp), axis=1, keepdims=True)
        outs.append(jnp.where(first, lo, hi))
    return jnp.concatenate(outs, axis=1)


def _by_head_rows(x):
    first = _first_head()
    return jnp.concatenate([jnp.where(first, x, 0.0), jnp.where(first, 0.0, x)], axis=0)


def _diag_blocks(x):
    return jnp.where(_first_head(), x[0:HEAD_SIZE], x[HEAD_SIZE:PAIR])


def _proj_kernel(x_ref, g_ref, w_ref, mix_ref, gate_ref):
    h = (_rms_scale(x_ref[...]) * g_ref[...]).astype(_BF16)
    p = _dot(h, w_ref[...])
    mix_ref[...] = p[:, :MIX_COLS]
    gate_ref[...] = p[:, MIX_COLS:]


def _proj(x, g, w_in_bf16):
    n_in = MIX_COLS + GATE_COLS
    return pl.pallas_call(
        _proj_kernel,
        grid=(N_TOK // TOKEN_TILE,),
        in_specs=[
            pl.BlockSpec((TOKEN_TILE, D_MODEL), lambda i: (i, 0)),
            pl.BlockSpec((1, D_MODEL), lambda i: (0, 0)),
            pl.BlockSpec((D_MODEL, n_in), lambda i: (0, 0)),
        ],
        out_specs=[
            pl.BlockSpec((TOKEN_TILE, MIX_COLS), lambda i: (i, 0)),
            pl.BlockSpec((TOKEN_TILE, GATE_COLS), lambda i: (i, 0)),
        ],
        out_shape=[
            jax.ShapeDtypeStruct((N_TOK, MIX_COLS), _F32),
            jax.ShapeDtypeStruct((N_TOK, GATE_COLS), _F32),
        ],
        compiler_params=pltpu.CompilerParams(
            dimension_semantics=("arbitrary",), vmem_limit_bytes=VMEM_LIMIT_BYTES),
        name="proj",
    )(x, g, w_in_bf16)


def _decayed_operands(r, kf, v, lw, kk, a_sig):
    tri = jnp.where(_iota((CHUNK, CHUNK), 1) <= _iota((CHUNK, CHUNK), 0), 1.0, 0.0).astype(_BF16)
    l_hi, l_mid, l_lo = _split3(lw)
    cum = _dot(tri, l_hi) + (_dot(tri, l_mid) + _dot(tri, l_lo))
    last = cum[CHUNK - 1:CHUNK]
    e_inv = jnp.exp(-cum)
    e_tail = jnp.exp(last - cum)
    b = kk * a_sig
    return dict(a_t=-kk * jnp.exp(cum - lw), r_t=r * jnp.exp(cum), b_t=b * e_inv, k_t=kf * e_inv,
                b_h=b * e_tail, k_h=kf * e_tail, v=v, dec=jnp.exp(last))


def _intra_pairs(chunks):
    first = _first_head()
    first2 = _first_head(2 * PAIR)
    g_row = _iota((4 * CHUNK, PAIR), 0)
    keep = (_iota((4 * CHUNK, PAIR), 1) % CHUNK) < (g_row % CHUNK) + (g_row // CHUNK) % 2
    rc_xor = _iota((PAIR, PAIR), 0) ^ _iota((PAIR, PAIR), 1)
    eye = rc_xor == 0
    zeros = jnp.zeros((CHUNK, PAIR), _F32)
    items = [(c, p) for c in range(len(chunks)) for p in range(N_PAIRS)]
    part = lambda it, name: chunks[it[0]][name][:, it[1] * PAIR:(it[1] + 1) * PAIR]

    gm = {}
    for it in items:
        ar = jnp.concatenate([part(it, "a_t"), part(it, "r_t")], axis=0)
        bk = jnp.concatenate([part(it, "b_t"), part(it, "k_t")], axis=0)
        ar_split = jnp.concatenate([jnp.where(first, ar, 0.0), jnp.where(first, 0.0, ar)], axis=0)
        gm[it] = jnp.where(keep, _dot_f32(ar_split, bk, _NT), 0.0)

    l_bd, inv = {}, {}
    for it in items:
        l0, l1 = gm[it][0:CHUNK], gm[it][2 * CHUNK:3 * CHUNK]
        l_bd[it] = jnp.concatenate([jnp.where(first, l0, 0.0),
                                    jnp.where(first, 0.0, pltpu.roll(l1, HEAD_SIZE, axis=1))], axis=0)
        inv[it] = jnp.where(eye, 1.0, jnp.where(rc_xor == 1, l_bd[it], 0.0))
    m = 2
    while m < CHUNK:
        level = jnp.logical_and(rc_xor >= m, rc_xor < 2 * m)
        inv_pieces = {it: _split2(inv[it]) for it in items}
        x = {it: _dot_pieces(_split2(jnp.where(level, l_bd[it], 0.0)), inv_pieces[it]) for it in items}
        for it in items:
            inv[it] = inv[it] + _dot_pieces(inv_pieces[it], _split2(x[it]))
        m *= 2

    akv = {}
    for it in items:
        v_p = part(it, "v")
        vv_pieces = _split2(jnp.concatenate([v_p, v_p], axis=0))
        l0, l1 = gm[it][0:CHUNK], gm[it][2 * CHUNK:3 * CHUNK]
        akv[it] = jnp.where(first, _dot_pieces(_split2(jnp.where(first, 0.0, l0)), vv_pieces),
                            _dot_pieces(_split2(jnp.where(first, 0.0, l1)), vv_pieces))
    uva_pieces = {}
    for it in items:
        tx = _dot_f32(inv[it], jnp.concatenate([_by_head_rows(akv[it]),
                                                _by_head_rows(part(it, "a_t"))], axis=1))
        u0a = tx[0:CHUNK] + tx[CHUNK:2 * CHUNK]
        uva = jnp.concatenate([u0a, jnp.concatenate([part(it, "v"), zeros], axis=1)], axis=0)
        uva_pieces[it] = _split2(uva)
    rps, y0s, mts, n0s = {}, {}, {}, {}
    for it in items:
        rr0, rr1 = gm[it][CHUNK:2 * CHUNK], gm[it][3 * CHUNK:4 * CHUNK]
        yr = jnp.where(first2, _dot_pieces(_split2(rr0), uva_pieces[it]),
                       _dot_pieces(_split2(rr1), uva_pieces[it]))
        y0s[it] = yr[:, 0:PAIR]
        rps[it] = part(it, "r_t") + yr[:, PAIR:2 * PAIR]
    for it in items:
        bk_h = jnp.concatenate([part(it, "b_h"), part(it, "k_h")], axis=0)
        nm = _dot_pieces(_split2(bk_h), uva_pieces[it], _TN)
        n0s[it] = _diag_blocks(nm[:, 0:PAIR])
        mts[it] = _diag_blocks(nm[:, PAIR:2 * PAIR] + jnp.where(eye, part(it, "dec"), 0.0))
    cat = lambda d, c: jnp.concatenate([d[(c, p)] for p in range(N_PAIRS)], axis=1)
    return ([cat(rps, c) for c in range(len(chunks))], [cat(y0s, c) for c in range(len(chunks))], mts, n0s)


def _intra_kernel(mix_ref, prev_ref, conv0_ref, shift0_ref, mu_ref, cw_ref, lora_hi_ref, lora_lo_ref,
                  wd0_ref, a0_ref, kk_w_ref, ka_w_ref, rk_w_ref,
                  bgy_ref, rp_ref, y0_ref, g_ref, bg_ref, mt_ref, n0_ref, conv_out_ref, shift_out_ref):
    i = pl.program_id(0)
    is_sample = i >= N_PROMPT_CHUNKS // INTRA_CHUNKS
    row = _iota((CHUNK, 1), 0)
    cw = cw_ref[...]
    lane = _iota((CHUNK, D_LORA), 1)

    prev = jnp.where(i == 0, 0.0, prev_ref[SUBLANES - 2:SUBLANES, :])
    u_hist = prev[:, 2 * D_CONV:3 * D_CONV] * prev[:, 0:D_CONV]
    pr_hist = prev[1:2, 3 * D_CONV:MIX_COLS]

    chunks = []
    for c in range(INTRA_CHUNKS):
        rows = slice(c * CHUNK, (c + 1) * CHUNK)
        u_hist = jnp.where(is_sample, conv0_ref[c], u_hist)
        pr_hist = jnp.where(is_sample, shift0_ref[c], pr_hist)

        xin = mix_ref[rows, 0:D_CONV]
        bg = mix_ref[rows, D_CONV:2 * D_CONV]
        cg = mix_ref[rows, 2 * D_CONV:3 * D_CONV]
        u = cg * xin
        u1 = jnp.where(row == 0, u_hist[1:2], pltpu.roll(u, 1, axis=0))
        u2 = jnp.where(row == 0, u_hist[0:1], jnp.where(row == 1, u_hist[1:2], pltpu.roll(u, 2, axis=0)))
        bgy_ref[rows, :] = bg * (cw[0:1] * u2 + cw[1:2] * u1 + cw[2:3] * u)
        u_hist = u[CHUNK - 2:CHUNK]
        conv_out_ref[c] = u_hist

        pr = mix_ref[rows, 3 * D_CONV:MIX_COLS]
        shifted = jnp.where(row == 0, pr_hist, pltpu.roll(pr, 1, axis=0))
        ps = pr + (shifted - pr) * mu_ref[...]
        pr_hist = pr[CHUNK - 1:CHUNK]
        shift_out_ref[c] = pr_hist

        r = ps[:, 0:D_RWKV]
        k = ps[:, D_RWKV:2 * D_RWKV]
        v = ps[:, 2 * D_RWKV:3 * D_RWKV]
        lora_in = ps[:, 3 * D_RWKV:RWKV_COLS]
        act = jnp.where(lane < D_DECAY_LORA, jnp.tanh(lora_in),
                        jnp.where(lane < D_DECAY_LORA + D_AAA_LORA, lora_in, jax.nn.sigmoid(lora_in)))
        lora = _dot_pieces(_split2(act), (lora_hi_ref[...], lora_lo_ref[...]))
        w_raw = wd0_ref[...] + lora[:, 0:D_RWKV]
        w_raw = -jax.nn.softplus(-w_raw) - 0.5
        lw = -jnp.exp(w_raw)
        a_sig = jax.nn.sigmoid(a0_ref[...] + lora[:, D_RWKV:2 * D_RWKV])
        g = lora[:, 2 * D_RWKV:3 * D_RWKV]

        kk = k * kk_w_ref[...]
        kk = kk / jnp.maximum(jnp.sqrt(_head_sums(kk * kk)), 1e-12)
        kf = k * (1.0 + (a_sig - 1.0) * ka_w_ref[...])
        g_ref[rows, :] = g
        bg_ref[rows, :] = _head_sums(r * kf * rk_w_ref[...]) * v * g

        chunks.append(_decayed_operands(r, kf, v, lw, kk, a_sig))

    rps, y0s, mts, n0s = _intra_pairs(chunks)
    for c in range(INTRA_CHUNKS):
        rows = slice(c * CHUNK, (c + 1) * CHUNK)
        rp_ref[rows, :] = rps[c]
        y0_ref[rows, :] = y0s[c]
        for p in range(N_PAIRS):
            mt_ref[c, p] = mts[(c, p)]
            n0_ref[c, p] = n0s[(c, p)]


def _intra(mix, conv0, shift0, mu, cw, lora_hi, lora_lo, wd0, a0, kk_w, ka_w, rk_w):
    rows = INTRA_CHUNKS * CHUNK
    n_prompt_steps = N_PROMPT_CHUNKS // INTRA_CHUNKS
    tok = lambda width: pl.BlockSpec((rows, width), lambda i: (i, 0))
    full = lambda a: pl.BlockSpec(a.shape, lambda i: (0,) * a.ndim)
    seq = lambda a: pl.BlockSpec((INTRA_CHUNKS,) + a.shape[1:],
                                 lambda i: (jnp.maximum(i - n_prompt_steps, 0), 0, 0))
    per_chunk = lambda *dims: pl.BlockSpec((INTRA_CHUNKS,) + dims, lambda i: (i,) + (0,) * len(dims))
    prev = pl.BlockSpec((SUBLANES, MIX_COLS),
                        lambda i: (jnp.maximum(i * (rows // SUBLANES) - 1, 0), 0))
    tok_out = jax.ShapeDtypeStruct((N_TOK, D_RWKV), _F32)
    blocks = jax.ShapeDtypeStruct((N_CHUNKS, N_PAIRS, HEAD_SIZE, PAIR), _F32)
    return pl.pallas_call(
        _intra_kernel,
        grid=(N_CHUNKS // INTRA_CHUNKS,),
        in_specs=[tok(MIX_COLS), prev, seq(conv0), seq(shift0), full(mu), full(cw), full(lora_hi),
                  full(lora_lo), full(wd0), full(a0), full(kk_w), full(ka_w), full(rk_w)],
        out_specs=[tok(D_CONV)] + [tok(D_RWKV)] * 4 + [
            per_chunk(N_PAIRS, HEAD_SIZE, PAIR), per_chunk(N_PAIRS, HEAD_SIZE, PAIR),
            per_chunk(2, D_CONV), per_chunk(1, RWKV_COLS)],
        out_shape=[jax.ShapeDtypeStruct((N_TOK, D_CONV), _F32)] + [tok_out] * 4 + [
            blocks, blocks,
            jax.ShapeDtypeStruct((N_CHUNKS, 2, D_CONV), _F32),
            jax.ShapeDtypeStruct((N_CHUNKS, 1, RWKV_COLS), _F32)],
        compiler_params=pltpu.CompilerParams(
            dimension_semantics=("arbitrary",), vmem_limit_bytes=VMEM_LIMIT_BYTES),
        name="intra",
    )(mix, mix, conv0, shift0, mu, cw, lora_hi, lora_lo, wd0, a0, kk_w, ka_w, rk_w)


def _scan_kernel(rp_ref, y0_ref, mt_ref, n0_ref, s0_ref, lnw_ref, lnb_ref,
                 yn_ref, s_prompt_ref, s_sample_ref, state):
    j = pl.program_id(0)
    is_sample = j >= N_PROMPT_CHUNKS // SCAN_CHUNKS

    @pl.when(j == 0)
    def _():
        state[...] = jnp.zeros_like(state)

    first = _first_head()
    ws = [state[p] for p in range(N_PAIRS)]
    for c in range(SCAN_CHUNKS):
        rows = slice(c * CHUNK, (c + 1) * CHUNK)
        ys = []
        for p in range(N_PAIRS):
            sl = slice(p * PAIR, (p + 1) * PAIR)
            w_pieces = _split2(jnp.where(is_sample, s0_ref[c, p], ws[p]))
            ys.append(_dot_pieces(_split2(rp_ref[rows, sl]), w_pieces) + y0_ref[rows, sl])
            ws[p] = (_dot_pieces(_split2(_by_head_rows(mt_ref[c, p])), w_pieces)
                     + _by_head_rows(n0_ref[c, p]))
            s_sample_ref[c, p] = ws[p]
        y = jnp.concatenate(ys, axis=1)
        inv_n = 1.0 / HEAD_SIZE
        d = y - _head_sums(y) * inv_n
        var = _head_sums(d * d) * inv_n
        yn_ref[rows, :] = d * lax.rsqrt(var + LN_X_EPS) * lnw_ref[...] + lnb_ref[...]

    for p in range(N_PAIRS):
        state[p] = jnp.where(is_sample, state[p], ws[p])
        s_prompt_ref[p] = state[p]


def _scan(rp, y0, mt, n0, s0, lnw, lnb):
    rows = SCAN_CHUNKS * CHUNK
    n_prompt_steps = N_PROMPT_CHUNKS // SCAN_CHUNKS
    tok = pl.BlockSpec((rows, D_RWKV), lambda j: (j, 0))
    row = pl.BlockSpec((1, D_RWKV), lambda j: (0, 0))
    blk = pl.BlockSpec((SCAN_CHUNKS, N_PAIRS, HEAD_SIZE, PAIR), lambda j: (j, 0, 0, 0))
    st = pl.BlockSpec((SCAN_CHUNKS, N_PAIRS, PAIR, PAIR),
                      lambda j: (jnp.maximum(j - n_prompt_steps, 0), 0, 0, 0))
    return pl.pallas_call(
        _scan_kernel,
        grid=(N_CHUNKS // SCAN_CHUNKS,),
        in_specs=[tok, tok, blk, blk, st, row, row],
        out_specs=[tok, pl.BlockSpec((N_PAIRS, PAIR, PAIR), lambda j: (0, 0, 0)), st],
        out_shape=[jax.ShapeDtypeStruct((N_TOK, D_RWKV), _F32),
                   jax.ShapeDtypeStruct((N_PAIRS, PAIR, PAIR), _F32),
                   jax.ShapeDtypeStruct((DEC_BATCH, N_PAIRS, PAIR, PAIR), _F32)],
        scratch_shapes=[pltpu.VMEM((N_PAIRS, PAIR, PAIR), _F32)],
        compiler_params=pltpu.CompilerParams(
            dimension_semantics=("arbitrary",), vmem_limit_bytes=VMEM_LIMIT_BYTES),
        name="scan",
    )(rp, y0, mt, n0, s0, lnw, lnb)


def _merge_kernel(x_ref, bgy_ref, yn_ref, g_ref, bg_ref, gate_ref, wc_ref, wr_ref, wo_ref, gpost_ref, o_ref):
    branch_a = _dot(bgy_ref[...].astype(_BF16), wc_ref[...])
    z = yn_ref[...] * g_ref[...] + bg_ref[...]
    branch_b = _dot(z.astype(_BF16), wr_ref[...])
    merged = (jax.nn.sigmoid(gate_ref[:, 0:D_MODEL]) * branch_a
              + jax.nn.sigmoid(gate_ref[:, D_MODEL:GATE_COLS]) * branch_b)
    m = _dot(merged.astype(_BF16), wo_ref[...])
    o_ref[...] = x_ref[...] + _rms_scale(m) * gpost_ref[...]


def _merge(x, bgy, yn, g, bg, gates, wc, wr, wo, gpost):
    tok = lambda width: pl.BlockSpec((TOKEN_TILE, width), lambda i: (i, 0))
    full = lambda a: pl.BlockSpec(a.shape, lambda i: (0,) * a.ndim)
    return pl.pallas_call(
        _merge_kernel,
        grid=(N_TOK // TOKEN_TILE,),
        in_specs=[tok(D_MODEL), tok(D_CONV), tok(D_RWKV), tok(D_RWKV), tok(D_RWKV), tok(GATE_COLS),
                  full(wc), full(wr), full(wo), full(gpost)],
        out_specs=tok(D_MODEL),
        out_shape=jax.ShapeDtypeStruct((N_TOK, D_MODEL), _F32),
        compiler_params=pltpu.CompilerParams(
            dimension_semantics=("arbitrary",), vmem_limit_bytes=VMEM_LIMIT_BYTES),
        name="merge",
    )(x, bgy, yn, g, bg, gates, wc, wr, wo, gpost)


def _ffn_kernel(x_ref, gpre_ref, wup_ref, wdown_ref, gpost_ref, o_ref):
    x = x_ref[...]
    h = (_rms_scale(x) * gpre_ref[...]).astype(_BF16)
    up = _dot(h, wup_ref[...])
    act = jax.nn.silu(up[:, :D_FF]) * up[:, D_FF:]
    f = _dot(act.astype(_BF16), wdown_ref[...])
    o_ref[...] = x + _rms_scale(f) * gpost_ref[...]


def _ffn(x, gpre, wup, wdown, gpost):
    tok = pl.BlockSpec((TOKEN_TILE, D_MODEL), lambda i: (i, 0))
    full = lambda a: pl.BlockSpec(a.shape, lambda i: (0,) * a.ndim)
    return pl.pallas_call(
        _ffn_kernel,
        grid=(N_TOK // TOKEN_TILE,),
        in_specs=[tok, full(gpre), full(wup), full(wdown), full(gpost)],
        out_specs=tok,
        out_shape=jax.ShapeDtypeStruct((N_TOK, D_MODEL), _F32),
        compiler_params=pltpu.CompilerParams(
            dimension_semantics=("arbitrary",), vmem_limit_bytes=VMEM_LIMIT_BYTES),
        name="ffn",
    )(x, gpre, wup, wdown, gpost)


def _state_to_pairs(s):
    st = jnp.swapaxes(s, -1, -2).reshape(s.shape[0], N_PAIRS, 2, HEAD_SIZE, HEAD_SIZE)
    eye2 = jnp.eye(2, dtype=s.dtype)
    out = st[:, :, :, :, None, :] * eye2[None, None, :, None, :, None]
    return out.reshape(s.shape[0], N_PAIRS, PAIR, PAIR)


def _pairs_to_state(w):
    w6 = w.reshape(w.shape[0], N_PAIRS, 2, HEAD_SIZE, 2, HEAD_SIZE)
    diag = jnp.stack([w6[:, :, 0, :, 0, :], w6[:, :, 1, :, 1, :]], axis=2)
    return jnp.swapaxes(diag, -1, -2).reshape(w.shape[0], N_HEADS, HEAD_SIZE, HEAD_SIZE)


def _lora_weight(w_decay2, a2, g2):
    w = jnp.zeros((D_LORA, 3 * D_RWKV), _F32)
    w = w.at[0:D_DECAY_LORA, 0:D_RWKV].set(w_decay2)
    w = w.at[D_DECAY_LORA:D_DECAY_LORA + D_AAA_LORA, D_RWKV:2 * D_RWKV].set(a2)
    w = w.at[D_DECAY_LORA + D_AAA_LORA:, 2 * D_RWKV:].set(g2)
    hi = w.astype(_BF16)
    lo = (w - hi.astype(_F32)).astype(_BF16)
    return hi, lo


def kernel(x_prompt, x_sample, state_conv, state_shift, state_wkv, norm_mix_pre, norm_mix_post, w_in, mu_shift, conv_w, w_decay0, w_decay2, a0, a2, g2, k_k, k_a, r_k, ln_x_w, ln_x_b, w_conv_out, w_rwkv_out, w_o, norm_ffn_pre, norm_ffn_post, w_ffn_up, w_ffn_down):
    x = jnp.concatenate([x_prompt.reshape(SEQ, D_MODEL),
                         x_sample.reshape(DEC_BATCH * DEC_SEQ, D_MODEL)], axis=0)
    row = lambda a: a.reshape(1, -1)
    conv_p, shift_p, wkv_p, conv_s, shift_s, wkv_s = [], [], [], [], [], []
    for l in range(DEPTH):
        lora_hi, lora_lo = _lora_weight(w_decay2[l], a2[l], g2[l])
        mix, gates = _proj(x, row(norm_mix_pre[l]), w_in[l].astype(_BF16))
        (bgy, rp, y0, g, bg, mt, n0, conv_out, shift_out) = _intra(
            mix, state_conv[l], state_shift[l][:, None], row(mu_shift[l]), conv_w[l], lora_hi, lora_lo,
            row(w_decay0[l]), row(a0[l]), row(k_k[l]), row(k_a[l]), row(r_k[l]))
        yn, s_prompt, s_sample = _scan(rp, y0, mt, n0, _state_to_pairs(state_wkv[l]),
                                       row(ln_x_w[l]), row(ln_x_b[l]))
        x = _merge(x, bgy, yn, g, bg, gates, w_conv_out[l].astype(_BF16), w_rwkv_out[l].astype(_BF16),
                   w_o[l].astype(_BF16), row(norm_mix_post[l]))
        x = _ffn(x, row(norm_ffn_pre[l]), w_ffn_up[l].astype(_BF16), w_ffn_down[l].astype(_BF16),
                 row(norm_ffn_post[l]))

        last = N_PROMPT_CHUNKS - 1
        conv_p.append(conv_out[last:last + 1])
        conv_s.append(conv_out[N_PROMPT_CHUNKS:])
        shift_p.append(shift_out[last:last + 1, 0])
        shift_s.append(shift_out[N_PROMPT_CHUNKS:, 0])
        wkv_p.append(_pairs_to_state(s_prompt[None]))
        wkv_s.append(_pairs_to_state(s_sample))
    return (x[:SEQ].reshape(1, SEQ, D_MODEL),
            x[SEQ:].reshape(DEC_BATCH, DEC_SEQ, D_MODEL),
            jnp.stack(conv_p, 0), jnp.stack(shift_p, 0), jnp.stack(wkv_p, 0),
            jnp.stack(conv_s, 0), jnp.stack(shift_s, 0), jnp.stack(wkv_s, 0))
```

```python
import jax
import jax.numpy as jnp
from jax import lax
from jax.experimental import pallas as pl
from jax.experimental.pallas import tpu as pltpu

D_MODEL = 1024
SEQ = 16384
DEPTH = 2
DEC_BATCH = 32
DEC_SEQ = 64
D_CONV = 512
D_RWKV = 512
HEAD_SIZE = 64
N_HEADS = 8
D_DECAY_LORA = 64
D_AAA_LORA = 64
D_GATE_LORA = 128
D_LORA = D_DECAY_LORA + D_AAA_LORA + D_GATE_LORA
LN_X_EPS = 64e-5
D_FF = 2816
RMS_EPS = 1e-6
RWKV_COLS = 3 * D_RWKV + D_LORA
MIX_COLS = 3 * D_CONV + RWKV_COLS
GATE_COLS = 2 * D_MODEL

CHUNK = 64
N_TOK = SEQ + DEC_BATCH * DEC_SEQ
N_CHUNKS = N_TOK // CHUNK
N_PROMPT_CHUNKS = SEQ // CHUNK
PAIR = 2 * HEAD_SIZE
N_PAIRS = N_HEADS // 2
TOKEN_TILE = 256
N_PROMPT_TILES = SEQ // TOKEN_TILE
STREAMS_PER_TILE = TOKEN_TILE // DEC_SEQ
INTRA_CHUNKS = 2
SCAN_CHUNKS = 8
SUBLANES = 8
PIECES_A = 1
PIECES_INV = 1
PIECES_OUT = 1
PIECES_LORA = 1
VMEM_LIMIT_BYTES = 56 * 1024 * 1024

_BF16 = jnp.bfloat16
_F32 = jnp.float32
_NN = (((1,), (0,)), ((), ()))
_NT = (((1,), (1,)), ((), ()))
_TN = (((0,), (0,)), ((), ()))


def _dot(a, b, dims=_NN):
    return lax.dot_general(a, b, dims, preferred_element_type=_F32)


def _pieces(x, n):
    out = []
    for _ in range(n - 1):
        hi = x.astype(_BF16)
        out.append(hi)
        x = x - hi.astype(_F32)
    out.append(x.astype(_BF16))
    return tuple(out)


def _split2(x):
    return _pieces(x, 2)


def _split3(x):
    return _pieces(x, 3)


def _dot_pieces(a, b, dims=_NN):
    n = max(len(a), len(b))
    terms = [_dot(a[i], b[j], dims) for i in range(len(a)) for j in range(len(b)) if i + j < n]
    total = terms[0]
    if len(terms) > 1:
        rest = terms[1]
        for t in terms[2:]:
            rest = rest + t
        total = total + rest
    return total


def _dot_f32(a, b, dims=_NN):
    return _dot_pieces(_split2(a), _split2(b), dims)


def _rms_scale(x):
    return x * lax.rsqrt(jnp.mean(x * x, axis=-1, keepdims=True) + RMS_EPS)


def _iota(shape, dim):
    return lax.broadcasted_iota(jnp.int32, shape, dim)


def _first_head(width=PAIR):
    return _iota((1, width), 1) % PAIR < HEAD_SIZE


def _head_sums(x):
    first = _first_head()
    outs = []
    for p in range(x.shape[1] // PAIR):
        xp = x[:, p * PAIR:(p + 1) * PAIR]
        lo = jnp.sum(jnp.where(first, xp, 0.0), axis=1, keepdims=True)
        hi = jnp.sum(jnp.where(first, 0.0, xp), axis=1, keepdims=True)
        outs.append(jnp.where(first, lo, hi))
    return jnp.concatenate(outs, axis=1)


def _by_head_rows(x):
    first = _first_head()
    return jnp.concatenate([jnp.where(first, x, 0.0), jnp.where(first, 0.0, x)], axis=0)


def _diag_blocks(x):
    return jnp.where(_first_head(), x[0:HEAD_SIZE], x[HEAD_SIZE:PAIR])


def _token_specs(n_parts):
    if n_parts == 1:
        return [pl.BlockSpec((TOKEN_TILE, D_MODEL), lambda i: (i, 0))]
    return [pl.BlockSpec((1, TOKEN_TILE, D_MODEL), lambda i: (0, jnp.minimum(i, N_PROMPT_TILES - 1), 0)),
            pl.BlockSpec((STREAMS_PER_TILE, DEC_SEQ, D_MODEL),
                         lambda i: (jnp.maximum(i - N_PROMPT_TILES, 0), 0, 0))]


def _load_tokens(x_refs):
    if len(x_refs) == 1:
        return x_refs[0][...]
    prompt_ref, stream_ref = x_refs
    return jnp.where(pl.program_id(0) >= N_PROMPT_TILES,
                     stream_ref[...].reshape(TOKEN_TILE, D_MODEL), prompt_ref[0])


def _proj_kernel(*refs):
    *x_refs, g_ref, w_ref, mix_ref, gate_ref = refs
    h = (_rms_scale(_load_tokens(x_refs)) * g_ref[...]).astype(_BF16)
    p = _dot(h, w_ref[...])
    mix_ref[...] = p[:, :MIX_COLS]
    gate_ref[...] = p[:, MIX_COLS:]


def _proj(x_parts, g, w_in_bf16):
    n_in = MIX_COLS + GATE_COLS
    return pl.pallas_call(
        _proj_kernel,
        grid=(N_TOK // TOKEN_TILE,),
        in_specs=_token_specs(len(x_parts)) + [
            pl.BlockSpec((1, D_MODEL), lambda i: (0, 0)),
            pl.BlockSpec((D_MODEL, n_in), lambda i: (0, 0)),
        ],
        out_specs=[
            pl.BlockSpec((TOKEN_TILE, MIX_COLS), lambda i: (i, 0)),
            pl.BlockSpec((TOKEN_TILE, GATE_COLS), lambda i: (i, 0)),
        ],
        out_shape=[
            jax.ShapeDtypeStruct((N_TOK, MIX_COLS), _F32),
            jax.ShapeDtypeStruct((N_TOK, GATE_COLS), _F32),
        ],
        compiler_params=pltpu.CompilerParams(
            dimension_semantics=("arbitrary",), vmem_limit_bytes=VMEM_LIMIT_BYTES),
        name="proj",
    )(*x_parts, g, w_in_bf16)


def _decayed_operands(r, kf, v, lw, kk, a_sig):
    tri = jnp.where(_iota((CHUNK, CHUNK), 1) <= _iota((CHUNK, CHUNK), 0), 1.0, 0.0).astype(_BF16)
    l_hi, l_mid, l_lo = _split3(lw)
    cum = _dot(tri, l_hi) + (_dot(tri, l_mid) + _dot(tri, l_lo))
    last = cum[CHUNK - 1:CHUNK]
    e_inv = jnp.exp(-cum)
    e_tail = jnp.exp(last - cum)
    b = kk * a_sig
    return dict(a_t=-kk * jnp.exp(cum - lw), r_t=r * jnp.exp(cum), b_t=b * e_inv, k_t=kf * e_inv,
                b_h=b * e_tail, k_h=kf * e_tail, v=v, dec=jnp.exp(last))


def _intra_pairs(chunks):
    first = _first_head()
    first2 = _first_head(2 * PAIR)
    g_row = _iota((4 * CHUNK, PAIR), 0)
    keep = (_iota((4 * CHUNK, PAIR), 1) % CHUNK) < (g_row % CHUNK) + (g_row // CHUNK) % 2
    rc_xor = _iota((PAIR, PAIR), 0) ^ _iota((PAIR, PAIR), 1)
    eye = rc_xor == 0
    zeros = jnp.zeros((CHUNK, PAIR), _F32)
    items = [(c, p) for c in range(len(chunks)) for p in range(N_PAIRS)]
    part = lambda it, name: chunks[it[0]][name][:, it[1] * PAIR:(it[1] + 1) * PAIR]
    out_pieces = lambda x: _pieces(x, PIECES_OUT)

    gm = {}
    for it in items:
        ar = jnp.concatenate([part(it, "a_t"), part(it, "r_t")], axis=0)
        bk = jnp.concatenate([part(it, "b_t"), part(it, "k_t")], axis=0)
        ar_split = jnp.concatenate([jnp.where(first, ar, 0.0), jnp.where(first, 0.0, ar)], axis=0)
        gm[it] = jnp.where(keep, _dot_pieces(_pieces(ar_split, PIECES_A), _pieces(bk, PIECES_A), _NT), 0.0)

    l_bd, inv = {}, {}
    for it in items:
        l0, l1 = gm[it][0:CHUNK], gm[it][2 * CHUNK:3 * CHUNK]
        l_bd[it] = jnp.concatenate([jnp.where(first, l0, 0.0),
                                    jnp.where(first, 0.0, pltpu.roll(l1, HEAD_SIZE, axis=1))], axis=0)
        inv[it] = jnp.where(eye, 1.0, jnp.where(rc_xor == 1, l_bd[it], 0.0))
    m = 2
    while m < CHUNK:
        level = jnp.logical_and(rc_xor >= m, rc_xor < 2 * m)
        inv_pieces = {it: _pieces(inv[it], PIECES_INV) for it in items}
        x = {it: _dot_pieces(_pieces(jnp.where(level, l_bd[it], 0.0), PIECES_INV), inv_pieces[it])
             for it in items}
        for it in items:
            inv[it] = inv[it] + _dot_pieces(inv_pieces[it], _pieces(x[it], PIECES_INV))
        m *= 2

    akv = {}
    for it in items:
        v_p = part(it, "v")
        vv_pieces = out_pieces(jnp.concatenate([v_p, v_p], axis=0))
        l0, l1 = gm[it][0:CHUNK], gm[it][2 * CHUNK:3 * CHUNK]
        akv[it] = jnp.where(first, _dot_pieces(out_pieces(jnp.where(first, 0.0, l0)), vv_pieces),
                            _dot_pieces(out_pieces(jnp.where(first, 0.0, l1)), vv_pieces))
    uva_pieces = {}
    for it in items:
        tx = _dot_pieces(out_pieces(inv[it]), out_pieces(jnp.concatenate(
            [_by_head_rows(akv[it]), _by_head_rows(part(it, "a_t"))], axis=1)))
        u0a = tx[0:CHUNK] + tx[CHUNK:2 * CHUNK]
        uva = jnp.concatenate([u0a, jnp.concatenate([part(it, "v"), zeros], axis=1)], axis=0)
        uva_pieces[it] = out_pieces(uva)
    rps, y0s, mts, n0s = {}, {}, {}, {}
    for it in items:
        rr0, rr1 = gm[it][CHUNK:2 * CHUNK], gm[it][3 * CHUNK:4 * CHUNK]
        yr = jnp.where(first2, _dot_pieces(out_pieces(rr0), uva_pieces[it]),
                       _dot_pieces(out_pieces(rr1), uva_pieces[it]))
        y0s[it] = yr[:, 0:PAIR]
        rps[it] = part(it, "r_t") + yr[:, PAIR:2 * PAIR]
    for it in items:
        bk_h = jnp.concatenate([part(it, "b_h"), part(it, "k_h")], axis=0)
        nm = _dot_pieces(out_pieces(bk_h), uva_pieces[it], _TN)
        n0s[it] = _diag_blocks(nm[:, 0:PAIR])
        mts[it] = _diag_blocks(nm[:, PAIR:2 * PAIR] + jnp.where(eye, part(it, "dec"), 0.0))
    cat = lambda d, c: jnp.concatenate([d[(c, p)] for p in range(N_PAIRS)], axis=1)
    return ([cat(rps, c) for c in range(len(chunks))], [cat(y0s, c) for c in range(len(chunks))], mts, n0s)


def _intra_kernel(mix_ref, prev_ref, conv0_ref, shift0_ref, mu_ref, cw_ref, lora_hi_ref, lora_lo_ref,
                  wd0_ref, a0_ref, kk_w_ref, ka_w_ref, rk_w_ref,
                  bgy_ref, rp_ref, y0_ref, g_ref, bg_ref, mt_ref, n0_ref, conv_out_ref, shift_out_ref):
    i = pl.program_id(0)
    is_sample = i >= N_PROMPT_CHUNKS // INTRA_CHUNKS
    row = _iota((CHUNK, 1), 0)
    cw = cw_ref[...]
    lane = _iota((CHUNK, D_LORA), 1)

    prev = jnp.where(i == 0, 0.0, prev_ref[SUBLANES - 2:SUBLANES, :])
    u_hist = prev[:, 2 * D_CONV:3 * D_CONV] * prev[:, 0:D_CONV]
    pr_hist = prev[1:2, 3 * D_CONV:MIX_COLS]

    chunks = []
    for c in range(INTRA_CHUNKS):
        rows = slice(c * CHUNK, (c + 1) * CHUNK)
        u_hist = jnp.where(is_sample, conv0_ref[c], u_hist)
        pr_hist = jnp.where(is_sample, shift0_ref[c], pr_hist)

        xin = mix_ref[rows, 0:D_CONV]
        bg = mix_ref[rows, D_CONV:2 * D_CONV]
        cg = mix_ref[rows, 2 * D_CONV:3 * D_CONV]
        u = cg * xin
        u1 = jnp.where(row == 0, u_hist[1:2], pltpu.roll(u, 1, axis=0))
        u2 = jnp.where(row == 0, u_hist[0:1], jnp.where(row == 1, u_hist[1:2], pltpu.roll(u, 2, axis=0)))
        bgy_ref[rows, :] = bg * (cw[0:1] * u2 + cw[1:2] * u1 + cw[2:3] * u)
        u_hist = u[CHUNK - 2:CHUNK]
        conv_out_ref[c] = u_hist

        pr = mix_ref[rows, 3 * D_CONV:MIX_COLS]
        shifted = jnp.where(row == 0, pr_hist, pltpu.roll(pr, 1, axis=0))
        ps = pr + (shifted - pr) * mu_ref[...]
        pr_hist = pr[CHUNK - 1:CHUNK]
        shift_out_ref[c] = pr_hist

        r = ps[:, 0:D_RWKV]
        k = ps[:, D_RWKV:2 * D_RWKV]
        v = ps[:, 2 * D_RWKV:3 * D_RWKV]
        lora_in = ps[:, 3 * D_RWKV:RWKV_COLS]
        act = jnp.where(lane < D_DECAY_LORA, jnp.tanh(lora_in),
                        jnp.where(lane < D_DECAY_LORA + D_AAA_LORA, lora_in, jax.nn.sigmoid(lora_in)))
        lora = _dot_pieces(_pieces(act, PIECES_LORA), (lora_hi_ref[...], lora_lo_ref[...])[:PIECES_LORA])
        w_raw = wd0_ref[...] + lora[:, 0:D_RWKV]
        w_raw = -jax.nn.softplus(-w_raw) - 0.5
        lw = -jnp.exp(w_raw)
        a_sig = jax.nn.sigmoid(a0_ref[...] + lora[:, D_RWKV:2 * D_RWKV])
        g = lora[:, 2 * D_RWKV:3 * D_RWKV]

        kk = k * kk_w_ref[...]
        kk = kk / jnp.maximum(jnp.sqrt(_head_sums(kk * kk)), 1e-12)
        kf = k * (1.0 + (a_sig - 1.0) * ka_w_ref[...])
        g_ref[rows, :] = g
        bg_ref[rows, :] = _head_sums(r * kf * rk_w_ref[...]) * v * g

        chunks.append(_decayed_operands(r, kf, v, lw, kk, a_sig))

    rps, y0s, mts, n0s = _intra_pairs(chunks)
    for c in range(INTRA_CHUNKS):
        rows = slice(c * CHUNK, (c + 1) * CHUNK)
        rp_ref[rows, :] = rps[c]
        y0_ref[rows, :] = y0s[c]
        for p in range(N_PAIRS):
            mt_ref[c, p] = mts[(c, p)]
            n0_ref[c, p] = n0s[(c, p)]


def _intra(mix, conv0, shift0, mu, cw, lora_hi, lora_lo, wd0, a0, kk_w, ka_w, rk_w):
    rows = INTRA_CHUNKS * CHUNK
    n_prompt_steps = N_PROMPT_CHUNKS // INTRA_CHUNKS
    tok = lambda width: pl.BlockSpec((rows, width), lambda i: (i, 0))
    full = lambda a: pl.BlockSpec(a.shape, lambda i: (0,) * a.ndim)
    seq = lambda a: pl.BlockSpec((INTRA_CHUNKS,) + a.shape[1:],
                                 lambda i: (jnp.maximum(i - n_prompt_steps, 0), 0, 0))
    per_chunk = lambda *dims: pl.BlockSpec((INTRA_CHUNKS,) + dims, lambda i: (i,) + (0,) * len(dims))
    prev = pl.BlockSpec((SUBLANES, MIX_COLS),
                        lambda i: (jnp.maximum(i * (rows // SUBLANES) - 1, 0), 0))
    tok_out = jax.ShapeDtypeStruct((N_TOK, D_RWKV), _F32)
    blocks = jax.ShapeDtypeStruct((N_CHUNKS, N_PAIRS, HEAD_SIZE, PAIR), _F32)
    return pl.pallas_call(
        _intra_kernel,
        grid=(N_CHUNKS // INTRA_CHUNKS,),
        in_specs=[tok(MIX_COLS), prev, seq(conv0), seq(shift0), full(mu), full(cw), full(lora_hi),
                  full(lora_lo), full(wd0), full(a0), full(kk_w), full(ka_w), full(rk_w)],
        out_specs=[tok(D_CONV)] + [tok(D_RWKV)] * 4 + [
            per_chunk(N_PAIRS, HEAD_SIZE, PAIR), per_chunk(N_PAIRS, HEAD_SIZE, PAIR),
            per_chunk(2, D_CONV), per_chunk(1, RWKV_COLS)],
        out_shape=[jax.ShapeDtypeStruct((N_TOK, D_CONV), _F32)] + [tok_out] * 4 + [
            blocks, blocks,
            jax.ShapeDtypeStruct((N_CHUNKS, 2, D_CONV), _F32),
            jax.ShapeDtypeStruct((N_CHUNKS, 1, RWKV_COLS), _F32)],
        compiler_params=pltpu.CompilerParams(
            dimension_semantics=("arbitrary",), vmem_limit_bytes=VMEM_LIMIT_BYTES),
        name="intra",
    )(mix, mix, conv0, shift0, mu, cw, lora_hi, lora_lo, wd0, a0, kk_w, ka_w, rk_w)


def _scan_kernel(rp_ref, y0_ref, mt_ref, n0_ref, s0_ref, lnw_ref, lnb_ref,
                 yn_ref, s_prompt_ref, s_sample_ref, state):
    j = pl.program_id(0)
    is_sample = j >= N_PROMPT_CHUNKS // SCAN_CHUNKS

    @pl.when(j == 0)
    def _():
        state[...] = jnp.zeros_like(state)

    ws = [state[p] for p in range(N_PAIRS)]
    for c in range(SCAN_CHUNKS):
        rows = slice(c * CHUNK, (c + 1) * CHUNK)
        ys = []
        for p in range(N_PAIRS):
            sl = slice(p * PAIR, (p + 1) * PAIR)
            w_pieces = _split2(jnp.where(is_sample, _by_head_rows(s0_ref[c, p]), ws[p]))
            ys.append(_dot_pieces(_split2(rp_ref[rows, sl]), w_pieces) + y0_ref[rows, sl])
            ws[p] = (_dot_pieces(_split2(_by_head_rows(mt_ref[c, p])), w_pieces)
                     + _by_head_rows(n0_ref[c, p]))
            s_sample_ref[c, p] = _diag_blocks(ws[p])
        y = jnp.concatenate(ys, axis=1)
        inv_n = 1.0 / HEAD_SIZE
        d = y - _head_sums(y) * inv_n
        var = _head_sums(d * d) * inv_n
        yn_ref[rows, :] = d * lax.rsqrt(var + LN_X_EPS) * lnw_ref[...] + lnb_ref[...]

    for p in range(N_PAIRS):
        state[p] = jnp.where(is_sample, state[p], ws[p])
        s_prompt_ref[p] = _diag_blocks(state[p])


def _scan(rp, y0, mt, n0, s0, lnw, lnb):
    rows = SCAN_CHUNKS * CHUNK
    n_prompt_steps = N_PROMPT_CHUNKS // SCAN_CHUNKS
    tok = pl.BlockSpec((rows, D_RWKV), lambda j: (j, 0))
    row = pl.BlockSpec((1, D_RWKV), lambda j: (0, 0))
    blk = pl.BlockSpec((SCAN_CHUNKS, N_PAIRS, HEAD_SIZE, PAIR), lambda j: (j, 0, 0, 0))
    st = pl.BlockSpec((SCAN_CHUNKS, N_PAIRS, HEAD_SIZE, PAIR),
                      lambda j: (jnp.maximum(j - n_prompt_steps, 0), 0, 0, 0))
    return pl.pallas_call(
        _scan_kernel,
        grid=(N_CHUNKS // SCAN_CHUNKS,),
        in_specs=[tok, tok, blk, blk, st, row, row],
        out_specs=[tok, pl.BlockSpec((N_PAIRS, HEAD_SIZE, PAIR), lambda j: (0, 0, 0)), st],
        out_shape=[jax.ShapeDtypeStruct((N_TOK, D_RWKV), _F32),
                   jax.ShapeDtypeStruct((N_PAIRS, HEAD_SIZE, PAIR), _F32),
                   jax.ShapeDtypeStruct((DEC_BATCH, N_PAIRS, HEAD_SIZE, PAIR), _F32)],
        scratch_shapes=[pltpu.VMEM((N_PAIRS, PAIR, PAIR), _F32)],
        compiler_params=pltpu.CompilerParams(
            dimension_semantics=("arbitrary",), vmem_limit_bytes=VMEM_LIMIT_BYTES),
        name="scan",
    )(rp, y0, mt, n0, s0, lnw, lnb)


def _merge_kernel(*refs):
    *x_refs, bgy_ref, yn_ref, g_ref, bg_ref, gate_ref, wc_ref, wr_ref, wo_ref, gpost_ref, o_ref = refs
    branch_a = _dot(bgy_ref[...].astype(_BF16), wc_ref[...])
    z = yn_ref[...] * g_ref[...] + bg_ref[...]
    branch_b = _dot(z.astype(_BF16), wr_ref[...])
    merged = (jax.nn.sigmoid(gate_ref[:, 0:D_MODEL]) * branch_a
              + jax.nn.sigmoid(gate_ref[:, D_MODEL:GATE_COLS]) * branch_b)
    m = _dot(merged.astype(_BF16), wo_ref[...])
    o_ref[...] = _load_tokens(x_refs) + _rms_scale(m) * gpost_ref[...]


def _merge(x_parts, bgy, yn, g, bg, gates, wc, wr, wo, gpost):
    tok = lambda width: pl.BlockSpec((TOKEN_TILE, width), lambda i: (i, 0))
    full = lambda a: pl.BlockSpec(a.shape, lambda i: (0,) * a.ndim)
    return pl.pallas_call(
        _merge_kernel,
        grid=(N_TOK // TOKEN_TILE,),
        in_specs=_token_specs(len(x_parts)) + [
            tok(D_CONV), tok(D_RWKV), tok(D_RWKV), tok(D_RWKV), tok(GATE_COLS),
            full(wc), full(wr), full(wo), full(gpost)],
        out_specs=tok(D_MODEL),
        out_shape=jax.ShapeDtypeStruct((N_TOK, D_MODEL), _F32),
        compiler_params=pltpu.CompilerParams(
            dimension_semantics=("arbitrary",), vmem_limit_bytes=VMEM_LIMIT_BYTES),
        name="merge",
    )(*x_parts, bgy, yn, g, bg, gates, wc, wr, wo, gpost)


def _ffn_kernel(x_ref, gpre_ref, wup_ref, wdown_ref, gpost_ref, *o_refs):
    x = x_ref[...]
    h = (_rms_scale(x) * gpre_ref[...]).astype(_BF16)
    up = _dot(h, wup_ref[...])
    act = jax.nn.silu(up[:, :D_FF]) * up[:, D_FF:]
    f = _dot(act.astype(_BF16), wdown_ref[...])
    out = x + _rms_scale(f) * gpost_ref[...]
    if len(o_refs) == 1:
        o_refs[0][...] = out
        return
    prompt_ref, stream_ref = o_refs
    is_stream = pl.program_id(0) >= N_PROMPT_TILES

    @pl.when(jnp.logical_not(is_stream))
    def _():
        prompt_ref[0] = out

    @pl.when(is_stream)
    def _():
        stream_ref[...] = out.reshape(STREAMS_PER_TILE, DEC_SEQ, D_MODEL)


def _ffn(x, gpre, wup, wdown, gpost, split_out):
    tok = pl.BlockSpec((TOKEN_TILE, D_MODEL), lambda i: (i, 0))
    full = lambda a: pl.BlockSpec(a.shape, lambda i: (0,) * a.ndim)
    if split_out:
        out_specs = _token_specs(2)
        out_shape = [jax.ShapeDtypeStruct((1, SEQ, D_MODEL), _F32),
                     jax.ShapeDtypeStruct((DEC_BATCH, DEC_SEQ, D_MODEL), _F32)]
    else:
        out_specs = tok
        out_shape = jax.ShapeDtypeStruct((N_TOK, D_MODEL), _F32)
    return pl.pallas_call(
        _ffn_kernel,
        grid=(N_TOK // TOKEN_TILE,),
        in_specs=[tok, full(gpre), full(wup), full(wdown), full(gpost)],
        out_specs=out_specs,
        out_shape=out_shape,
        compiler_params=pltpu.CompilerParams(
            dimension_semantics=("arbitrary",), vmem_limit_bytes=VMEM_LIMIT_BYTES),
        name="ffn",
    )(x, gpre, wup, wdown, gpost)


def _state_to_pairs(s):
    s5 = s.reshape(s.shape[0], N_PAIRS, 2, HEAD_SIZE, HEAD_SIZE)
    return jnp.transpose(s5, (0, 1, 4, 2, 3)).reshape(s.shape[0], N_PAIRS, HEAD_SIZE, PAIR)


def _pairs_to_state(w):
    w5 = w.reshape(w.shape[0], N_PAIRS, HEAD_SIZE, 2, HEAD_SIZE)
    return jnp.transpose(w5, (0, 1, 3, 4, 2)).reshape(w.shape[0], N_HEADS, HEAD_SIZE, HEAD_SIZE)


def _lora_weight(w_decay2, a2, g2):
    w = jnp.zeros((D_LORA, 3 * D_RWKV), _F32)
    w = w.at[0:D_DECAY_LORA, 0:D_RWKV].set(w_decay2)
    w = w.at[D_DECAY_LORA:D_DECAY_LORA + D_AAA_LORA, D_RWKV:2 * D_RWKV].set(a2)
    w = w.at[D_DECAY_LORA + D_AAA_LORA:, 2 * D_RWKV:].set(g2)
    hi = w.astype(_BF16)
    lo = (w - hi.astype(_F32)).astype(_BF16)
    return hi, lo


def kernel(x_prompt, x_sample, state_conv, state_shift, state_wkv, norm_mix_pre, norm_mix_post, w_in, mu_shift, conv_w, w_decay0, w_decay2, a0, a2, g2, k_k, k_a, r_k, ln_x_w, ln_x_b, w_conv_out, w_rwkv_out, w_o, norm_ffn_pre, norm_ffn_post, w_ffn_up, w_ffn_down):
    x_parts = (x_prompt, x_sample)
    row = lambda a: a.reshape(1, -1)
    conv_p, shift_p, wkv_p, conv_s, shift_s, wkv_s = [], [], [], [], [], []
    for l in range(DEPTH):
        lora_hi, lora_lo = _lora_weight(w_decay2[l], a2[l], g2[l])
        mix, gates = _proj(x_parts, row(norm_mix_pre[l]), w_in[l].astype(_BF16))
        (bgy, rp, y0, g, bg, mt, n0, conv_out, shift_out) = _intra(
            mix, state_conv[l], state_shift[l][:, None], row(mu_shift[l]), conv_w[l], lora_hi, lora_lo,
            row(w_decay0[l]), row(a0[l]), row(k_k[l]), row(k_a[l]), row(r_k[l]))
        yn, s_prompt, s_sample = _scan(rp, y0, mt, n0, _state_to_pairs(state_wkv[l]),
                                       row(ln_x_w[l]), row(ln_x_b[l]))
        x = _merge(x_parts, bgy, yn, g, bg, gates, w_conv_out[l].astype(_BF16), w_rwkv_out[l].astype(_BF16),
                   w_o[l].astype(_BF16), row(norm_mix_post[l]))
        x = _ffn(x, row(norm_ffn_pre[l]), w_ffn_up[l].astype(_BF16), w_ffn_down[l].astype(_BF16),
                 row(norm_ffn_post[l]), split_out=(l == DEPTH - 1))
        x_parts = (x,)

        last = N_PROMPT_CHUNKS - 1
        conv_p.append(conv_out[last:last + 1])
        conv_s.append(conv_out[N_PROMPT_CHUNKS:])
        shift_p.append(shift_out[last:last + 1, 0])
        shift_s.append(shift_out[N_PROMPT_CHUNKS:, 0])
        wkv_p.append(_pairs_to_state(s_prompt[None]))
        wkv_s.append(_pairs_to_state(s_sample))
    y_prompt, y_sample = x
    return (y_prompt, y_sample,
            jnp.stack(conv_p, 0), jnp.stack(shift_p, 0), jnp.stack(wkv_p, 0),
            jnp.stack(conv_s, 0), jnp.stack(shift_s, 0), jnp.stack(wkv_s, 0))
```

```python
import jax
import jax.numpy as jnp
from jax import lax
from jax.experimental import pallas as pl
from jax.experimental.pallas import tpu as pltpu

D_MODEL = 1024
SEQ = 16384
DEPTH = 2
DEC_BATCH = 32
DEC_SEQ = 64
D_CONV = 512
D_RWKV = 512
HEAD_SIZE = 64
N_HEADS = 8
D_DECAY_LORA = 64
D_AAA_LORA = 64
D_GATE_LORA = 128
D_LORA = D_DECAY_LORA + D_AAA_LORA + D_GATE_LORA
LN_X_EPS = 64e-5
D_FF = 2816
RMS_EPS = 1e-6
RWKV_COLS = 3 * D_RWKV + D_LORA
MIX_COLS = 3 * D_CONV + RWKV_COLS
GATE_COLS = 2 * D_MODEL

CHUNK = 64
N_TOK = SEQ + DEC_BATCH * DEC_SEQ
N_CHUNKS = N_TOK // CHUNK
N_PROMPT_CHUNKS = SEQ // CHUNK
PAIR = 2 * HEAD_SIZE
N_PAIRS = N_HEADS // 2
TOKEN_TILE = 256
N_PROMPT_TILES = SEQ // TOKEN_TILE
STREAMS_PER_TILE = TOKEN_TILE // DEC_SEQ
INTRA_CHUNKS = 8
INTRA_GROUP = 2
SCAN_CHUNKS = 8
SUBLANES = 8
PIECES_A = 1
PIECES_INV = 1
PIECES_OUT = 1
PIECES_LORA = 1
VMEM_LIMIT_BYTES = 56 * 1024 * 1024

_BF16 = jnp.bfloat16
_F32 = jnp.float32
_NN = (((1,), (0,)), ((), ()))
_NT = (((1,), (1,)), ((), ()))
_TN = (((0,), (0,)), ((), ()))


def _dot(a, b, dims=_NN):
    return lax.dot_general(a, b, dims, preferred_element_type=_F32)


def _pieces(x, n):
    out = []
    for _ in range(n - 1):
        hi = x.astype(_BF16)
        out.append(hi)
        x = x - hi.astype(_F32)
    out.append(x.astype(_BF16))
    return tuple(out)


def _split2(x):
    return _pieces(x, 2)


def _split3(x):
    return _pieces(x, 3)


def _dot_pieces(a, b, dims=_NN):
    n = max(len(a), len(b))
    terms = [_dot(a[i], b[j], dims) for i in range(len(a)) for j in range(len(b)) if i + j < n]
    total = terms[0]
    if len(terms) > 1:
        rest = terms[1]
        for t in terms[2:]:
            rest = rest + t
        total = total + rest
    return total


def _dot_f32(a, b, dims=_NN):
    return _dot_pieces(_split2(a), _split2(b), dims)


def _run_steps(steps):
    while True:
        try:
            next(steps)
        except StopIteration as stop:
            return stop.value


def _zip_steps(a, b):
    values, live = {}, {"a": a, "b": b}
    while live:
        for name in ("a", "b"):
            if name in live:
                try:
                    next(live[name])
                except StopIteration as stop:
                    values[name] = stop.value
                    del live[name]
    return values["a"], values["b"]


def _rms_scale(x):
    return x * lax.rsqrt(jnp.mean(x * x, axis=-1, keepdims=True) + RMS_EPS)


def _iota(shape, dim):
    return lax.broadcasted_iota(jnp.int32, shape, dim)


def _first_head(width=PAIR):
    return _iota((1, width), 1) % PAIR < HEAD_SIZE


def _head_sums(x):
    first = _first_head()
    outs = []
    for p in range(x.shape[1] // PAIR):
        xp = x[:, p * PAIR:(p + 1) * PAIR]
        lo = jnp.sum(jnp.where(first, xp, 0.0), axis=1, keepdims=True)
        hi = jnp.sum(jnp.where(first, 0.0, xp), axis=1, keepdims=True)
        outs.append(jnp.where(first, lo, hi))
    return jnp.concatenate(outs, axis=1)


def _by_head_rows(x):
    first = _first_head()
    return jnp.concatenate([jnp.where(first, x, 0.0), jnp.where(first, 0.0, x)], axis=0)


def _diag_blocks(x):
    return jnp.where(_first_head(), x[0:HEAD_SIZE], x[HEAD_SIZE:PAIR])


def _token_specs(n_parts):
    if n_parts == 1:
        return [pl.BlockSpec((TOKEN_TILE, D_MODEL), lambda i: (i, 0))]
    return [pl.BlockSpec((1, TOKEN_TILE, D_MODEL), lambda i: (0, jnp.minimum(i, N_PROMPT_TILES - 1), 0)),
            pl.BlockSpec((STREAMS_PER_TILE, DEC_SEQ, D_MODEL),
                         lambda i: (jnp.maximum(i - N_PROMPT_TILES, 0), 0, 0))]


def _load_tokens(x_refs):
    if len(x_refs) == 1:
        return x_refs[0][...]
    prompt_ref, stream_ref = x_refs
    return jnp.where(pl.program_id(0) >= N_PROMPT_TILES,
                     stream_ref[...].reshape(TOKEN_TILE, D_MODEL), prompt_ref[0])


def _proj_kernel(*refs):
    *x_refs, g_ref, w_ref, mix_ref, gate_ref = refs
    h = (_rms_scale(_load_tokens(x_refs)) * g_ref[...]).astype(_BF16)
    p = _dot(h, w_ref[...])
    mix_ref[...] = p[:, :MIX_COLS]
    gate_ref[...] = p[:, MIX_COLS:]


def _proj(x_parts, g, w_in_bf16):
    n_in = MIX_COLS + GATE_COLS
    return pl.pallas_call(
        _proj_kernel,
        grid=(N_TOK // TOKEN_TILE,),
        in_specs=_token_specs(len(x_parts)) + [
            pl.BlockSpec((1, D_MODEL), lambda i: (0, 0)),
            pl.BlockSpec((D_MODEL, n_in), lambda i: (0, 0)),
        ],
        out_specs=[
            pl.BlockSpec((TOKEN_TILE, MIX_COLS), lambda i: (i, 0)),
            pl.BlockSpec((TOKEN_TILE, GATE_COLS), lambda i: (i, 0)),
        ],
        out_shape=[
            jax.ShapeDtypeStruct((N_TOK, MIX_COLS), _F32),
            jax.ShapeDtypeStruct((N_TOK, GATE_COLS), _F32),
        ],
        compiler_params=pltpu.CompilerParams(
            dimension_semantics=("arbitrary",), vmem_limit_bytes=VMEM_LIMIT_BYTES),
        name="proj",
    )(*x_parts, g, w_in_bf16)


def _running_log_decay(lw):
    tri = jnp.where(_iota((CHUNK, CHUNK), 1) <= _iota((CHUNK, CHUNK), 0), 1.0, 0.0).astype(_BF16)
    return _dot_pieces((tri,), _split3(lw))


def _decayed_operands(r, kf, v, lw, cum, kk, a_sig):
    last = cum[CHUNK - 1:CHUNK]
    e_inv = jnp.exp(-cum)
    e_tail = jnp.exp(last - cum)
    b = kk * a_sig
    return dict(a_t=-kk * jnp.exp(cum - lw), r_t=r * jnp.exp(cum), b_t=b * e_inv, k_t=kf * e_inv,
                b_h=b * e_tail, k_h=kf * e_tail, v=v, dec=jnp.exp(last))


def _intra_pairs(chunks):
    first = _first_head()
    first2 = _first_head(2 * PAIR)
    g_row = _iota((4 * CHUNK, PAIR), 0)
    keep = (_iota((4 * CHUNK, PAIR), 1) % CHUNK) < (g_row % CHUNK) + (g_row // CHUNK) % 2
    rc_xor = _iota((PAIR, PAIR), 0) ^ _iota((PAIR, PAIR), 1)
    eye = rc_xor == 0
    zeros = jnp.zeros((CHUNK, PAIR), _F32)
    items = [(c, p) for c in range(len(chunks)) for p in range(N_PAIRS)]
    part = lambda it, name: chunks[it[0]][name][:, it[1] * PAIR:(it[1] + 1) * PAIR]
    out_pieces = lambda x: _pieces(x, PIECES_OUT)

    gm = {}
    for it in items:
        ar = jnp.concatenate([part(it, "a_t"), part(it, "r_t")], axis=0)
        bk = jnp.concatenate([part(it, "b_t"), part(it, "k_t")], axis=0)
        ar_split = jnp.concatenate([jnp.where(first, ar, 0.0), jnp.where(first, 0.0, ar)], axis=0)
        gm[it] = jnp.where(keep, _dot_pieces(_pieces(ar_split, PIECES_A), _pieces(bk, PIECES_A), _NT), 0.0)
    yield

    l_bd, inv = {}, {}
    for it in items:
        l0, l1 = gm[it][0:CHUNK], gm[it][2 * CHUNK:3 * CHUNK]
        l_bd[it] = jnp.concatenate([jnp.where(first, l0, 0.0),
                                    jnp.where(first, 0.0, pltpu.roll(l1, HEAD_SIZE, axis=1))], axis=0)
        inv[it] = jnp.where(eye, 1.0, jnp.where(rc_xor == 1, l_bd[it], 0.0))
    yield
    m = 2
    while m < CHUNK:
        level = jnp.logical_and(rc_xor >= m, rc_xor < 2 * m)
        inv_pieces = {it: _pieces(inv[it], PIECES_INV) for it in items}
        x = {it: _dot_pieces(_pieces(jnp.where(level, l_bd[it], 0.0), PIECES_INV), inv_pieces[it])
             for it in items}
        yield
        for it in items:
            inv[it] = inv[it] + _dot_pieces(inv_pieces[it], _pieces(x[it], PIECES_INV))
        yield
        m *= 2

    akv = {}
    for it in items:
        v_p = part(it, "v")
        vv_pieces = out_pieces(jnp.concatenate([v_p, v_p], axis=0))
        l0, l1 = gm[it][0:CHUNK], gm[it][2 * CHUNK:3 * CHUNK]
        akv[it] = jnp.where(first, _dot_pieces(out_pieces(jnp.where(first, 0.0, l0)), vv_pieces),
                            _dot_pieces(out_pieces(jnp.where(first, 0.0, l1)), vv_pieces))
    yield
    uva_pieces = {}
    for it in items:
        tx = _dot_pieces(out_pieces(inv[it]), out_pieces(jnp.concatenate(
            [_by_head_rows(akv[it]), _by_head_rows(part(it, "a_t"))], axis=1)))
        u0a = tx[0:CHUNK] + tx[CHUNK:2 * CHUNK]
        uva = jnp.concatenate([u0a, jnp.concatenate([part(it, "v"), zeros], axis=1)], axis=0)
        uva_pieces[it] = out_pieces(uva)
    yield
    rps, y0s, mts, n0s = {}, {}, {}, {}
    for it in items:
        rr0, rr1 = gm[it][CHUNK:2 * CHUNK], gm[it][3 * CHUNK:4 * CHUNK]
        yr = jnp.where(first2, _dot_pieces(out_pieces(rr0), uva_pieces[it]),
                       _dot_pieces(out_pieces(rr1), uva_pieces[it]))
        y0s[it] = yr[:, 0:PAIR]
        rps[it] = part(it, "r_t") + yr[:, PAIR:2 * PAIR]
    yield
    for it in items:
        bk_h = jnp.concatenate([part(it, "b_h"), part(it, "k_h")], axis=0)
        nm = _dot_pieces(out_pieces(bk_h), uva_pieces[it], _TN)
        n0s[it] = _diag_blocks(nm[:, 0:PAIR])
        mts[it] = _diag_blocks(nm[:, PAIR:2 * PAIR] + jnp.where(eye, part(it, "dec"), 0.0))
    cat = lambda d, c: jnp.concatenate([d[(c, p)] for p in range(N_PAIRS)], axis=1)
    return ([cat(rps, c) for c in range(len(chunks))], [cat(y0s, c) for c in range(len(chunks))], mts, n0s)


def _intra_kernel(mix_ref, prev_ref, conv0_ref, shift0_ref, mu_ref, cw_ref, lora_hi_ref, lora_lo_ref,
                  wd0_ref, a0_ref, kk_w_ref, ka_w_ref, rk_w_ref,
                  bgy_ref, rp_ref, y0_ref, g_ref, bg_ref, mt_ref, n0_ref, conv_out_ref, shift_out_ref):
    i = pl.program_id(0)
    is_sample = i >= N_PROMPT_CHUNKS // INTRA_CHUNKS
    row = _iota((CHUNK, 1), 0)
    cw = cw_ref[...]
    lane = _iota((CHUNK, D_LORA), 1)

    prev = jnp.where(i == 0, 0.0, prev_ref[SUBLANES - 2:SUBLANES, :])
    hist = dict(u=prev[:, 2 * D_CONV:3 * D_CONV] * prev[:, 0:D_CONV],
                pr=prev[1:2, 3 * D_CONV:MIX_COLS])

    def prepare(c):
        rows = slice(c * CHUNK, (c + 1) * CHUNK)
        u_hist = jnp.where(is_sample, conv0_ref[c], hist["u"])
        pr_hist = jnp.where(is_sample, shift0_ref[c], hist["pr"])

        xin = mix_ref[rows, 0:D_CONV]
        bg = mix_ref[rows, D_CONV:2 * D_CONV]
        cg = mix_ref[rows, 2 * D_CONV:3 * D_CONV]
        u = cg * xin
        u1 = jnp.where(row == 0, u_hist[1:2], pltpu.roll(u, 1, axis=0))
        u2 = jnp.where(row == 0, u_hist[0:1], jnp.where(row == 1, u_hist[1:2], pltpu.roll(u, 2, axis=0)))
        bgy_ref[rows, :] = bg * (cw[0:1] * u2 + cw[1:2] * u1 + cw[2:3] * u)
        hist["u"] = u[CHUNK - 2:CHUNK]
        conv_out_ref[c] = hist["u"]
        yield

        pr = mix_ref[rows, 3 * D_CONV:MIX_COLS]
        shifted = jnp.where(row == 0, pr_hist, pltpu.roll(pr, 1, axis=0))
        ps = pr + (shifted - pr) * mu_ref[...]
        hist["pr"] = pr[CHUNK - 1:CHUNK]
        shift_out_ref[c] = hist["pr"]
        r = ps[:, 0:D_RWKV]
        k = ps[:, D_RWKV:2 * D_RWKV]
        v = ps[:, 2 * D_RWKV:3 * D_RWKV]
        yield

        lora_in = ps[:, 3 * D_RWKV:RWKV_COLS]
        act = jnp.where(lane < D_DECAY_LORA, jnp.tanh(lora_in),
                        jnp.where(lane < D_DECAY_LORA + D_AAA_LORA, lora_in, jax.nn.sigmoid(lora_in)))
        lora = _dot_pieces(_pieces(act, PIECES_LORA), (lora_hi_ref[...], lora_lo_ref[...])[:PIECES_LORA])
        yield

        w_raw = wd0_ref[...] + lora[:, 0:D_RWKV]
        w_raw = -jax.nn.softplus(-w_raw) - 0.5
        lw = -jnp.exp(w_raw)
        cum = _running_log_decay(lw)
        yield

        a_sig = jax.nn.sigmoid(a0_ref[...] + lora[:, D_RWKV:2 * D_RWKV])
        g = lora[:, 2 * D_RWKV:3 * D_RWKV]
        g_ref[rows, :] = g
        kk = k * kk_w_ref[...]
        yield
        kk = kk / jnp.maximum(jnp.sqrt(_head_sums(kk * kk)), 1e-12)
        yield
        kf = k * (1.0 + (a_sig - 1.0) * ka_w_ref[...])
        bg_ref[rows, :] = _head_sums(r * kf * rk_w_ref[...]) * v * g
        yield
        return _decayed_operands(r, kf, v, lw, cum, kk, a_sig)

    def prepare_group(first_chunk):
        ops = []
        for c in range(first_chunk, first_chunk + INTRA_GROUP):
            ops.append((yield from prepare(c)))
        return ops

    def store(first_chunk, results):
        rps, y0s, mts, n0s = results
        for g in range(INTRA_GROUP):
            c = first_chunk + g
            rows = slice(c * CHUNK, (c + 1) * CHUNK)
            rp_ref[rows, :] = rps[g]
            y0_ref[rows, :] = y0s[g]
            for p in range(N_PAIRS):
                mt_ref[c, p] = mts[(g, p)]
                n0_ref[c, p] = n0s[(g, p)]

    operands = _run_steps(prepare_group(0))
    for first_chunk in range(0, INTRA_CHUNKS, INTRA_GROUP):
        pair_steps = _intra_pairs(operands)
        if first_chunk + INTRA_GROUP < INTRA_CHUNKS:
            results, operands = _zip_steps(pair_steps, prepare_group(first_chunk + INTRA_GROUP))
        else:
            results = _run_steps(pair_steps)
        store(first_chunk, results)


def _intra(mix, conv0, shift0, mu, cw, lora_hi, lora_lo, wd0, a0, kk_w, ka_w, rk_w):
    rows = INTRA_CHUNKS * CHUNK
    n_prompt_steps = N_PROMPT_CHUNKS // INTRA_CHUNKS
    tok = lambda width: pl.BlockSpec((rows, width), lambda i: (i, 0))
    full = lambda a: pl.BlockSpec(a.shape, lambda i: (0,) * a.ndim)
    seq = lambda a: pl.BlockSpec((INTRA_CHUNKS,) + a.shape[1:],
                                 lambda i: (jnp.maximum(i - n_prompt_steps, 0), 0, 0))
    per_chunk = lambda *dims: pl.BlockSpec((INTRA_CHUNKS,) + dims, lambda i: (i,) + (0,) * len(dims))
    prev = pl.BlockSpec((SUBLANES, MIX_COLS),
                        lambda i: (jnp.maximum(i * (rows // SUBLANES) - 1, 0), 0))
    tok_out = jax.ShapeDtypeStruct((N_TOK, D_RWKV), _F32)
    blocks = jax.ShapeDtypeStruct((N_CHUNKS, N_PAIRS, HEAD_SIZE, PAIR), _F32)
    return pl.pallas_call(
        _intra_kernel,
        grid=(N_CHUNKS // INTRA_CHUNKS,),
        in_specs=[tok(MIX_COLS), prev, seq(conv0), seq(shift0), full(mu), full(cw), full(lora_hi),
                  full(lora_lo), full(wd0), full(a0), full(kk_w), full(ka_w), full(rk_w)],
        out_specs=[tok(D_CONV)] + [tok(D_RWKV)] * 4 + [
            per_chunk(N_PAIRS, HEAD_SIZE, PAIR), per_chunk(N_PAIRS, HEAD_SIZE, PAIR),
            per_chunk(2, D_CONV), per_chunk(1, RWKV_COLS)],
        out_shape=[jax.ShapeDtypeStruct((N_TOK, D_CONV), _F32)] + [tok_out] * 4 + [
            blocks, blocks,
            jax.ShapeDtypeStruct((N_CHUNKS, 2, D_CONV), _F32),
            jax.ShapeDtypeStruct((N_CHUNKS, 1, RWKV_COLS), _F32)],
        compiler_params=pltpu.CompilerParams(
            dimension_semantics=("arbitrary",), vmem_limit_bytes=VMEM_LIMIT_BYTES),
        name="intra",
    )(mix, mix, conv0, shift0, mu, cw, lora_hi, lora_lo, wd0, a0, kk_w, ka_w, rk_w)


def _scan_kernel(rp_ref, y0_ref, mt_ref, n0_ref, s0_ref, lnw_ref, lnb_ref,
                 yn_ref, s_prompt_ref, s_sample_ref, state):
    j = pl.program_id(0)
    is_stream = j >= N_PROMPT_CHUNKS // SCAN_CHUNKS

    @pl.when(j == 0)
    def _():
        state[...] = jnp.zeros_like(state)

    def advance(c, p, w):
        rows, sl = slice(c * CHUNK, (c + 1) * CHUNK), slice(p * PAIR, (p + 1) * PAIR)
        w_pieces = _split2(w)
        y = _dot_pieces(_split2(rp_ref[rows, sl]), w_pieces) + y0_ref[rows, sl]
        w_next = _dot_pieces(_split2(_by_head_rows(mt_ref[c, p])), w_pieces) + _by_head_rows(n0_ref[c, p])
        return y, w_next

    def group_norm(c, ys):
        y = jnp.concatenate(ys, axis=1)
        inv_n = 1.0 / HEAD_SIZE
        d = y - _head_sums(y) * inv_n
        var = _head_sums(d * d) * inv_n
        yn_ref[c * CHUNK:(c + 1) * CHUNK, :] = d * lax.rsqrt(var + LN_X_EPS) * lnw_ref[...] + lnb_ref[...]

    eye = (_iota((PAIR, PAIR), 0) == _iota((PAIR, PAIR), 1)).astype(_BF16)

    def heads_of(s):
        return s[0:HEAD_SIZE, 0:HEAD_SIZE], s[HEAD_SIZE:PAIR, HEAD_SIZE:PAIR]

    @pl.when(jnp.logical_not(is_stream))
    def _():
        ws = [state[p] for p in range(N_PAIRS)]
        for c in range(SCAN_CHUNKS):
            ys = []
            for p in range(N_PAIRS):
                y, ws[p] = advance(c, p, ws[p])
                ys.append(y)
            group_norm(c, ys)
        for p in range(N_PAIRS):
            state[p] = ws[p]
            s_prompt_ref[2 * p], s_prompt_ref[2 * p + 1] = heads_of(_dot_pieces((eye,), _split3(ws[p]), _NT))

    @pl.when(is_stream)
    def _():
        zeros = jnp.zeros((HEAD_SIZE, HEAD_SIZE), _F32)
        for c in range(SCAN_CHUNKS):
            rows = slice(c * CHUNK, (c + 1) * CHUNK)
            ys = []
            for p in range(N_PAIRS):
                sl = slice(p * PAIR, (p + 1) * PAIR)
                s0 = jnp.concatenate([jnp.concatenate([s0_ref[c, 2 * p], zeros], axis=1),
                                      jnp.concatenate([zeros, s0_ref[c, 2 * p + 1]], axis=1)], axis=0)
                s0_hi, s0_lo = _split2(s0)
                ys.append(_dot_pieces(_split2(rp_ref[rows, sl]), (s0_hi, s0_lo), _NT) + y0_ref[rows, sl])
                mn = jnp.concatenate([_by_head_rows(mt_ref[c, p]), _by_head_rows(n0_ref[c, p])], axis=1)
                s_next = _dot_pieces((jnp.concatenate([s0_hi, eye], axis=1),
                                      jnp.concatenate([s0_lo, jnp.zeros_like(eye)], axis=1)),
                                     _split2(mn), _NT)
                s_sample_ref[c, 2 * p], s_sample_ref[c, 2 * p + 1] = heads_of(s_next)
            group_norm(c, ys)


def _scan(rp, y0, m, n0, state_wkv, layer, lnw, lnb):
    rows = SCAN_CHUNKS * CHUNK
    n_prompt_steps = N_PROMPT_CHUNKS // SCAN_CHUNKS
    tok = pl.BlockSpec((rows, D_RWKV), lambda j: (j, 0))
    row = pl.BlockSpec((1, D_RWKV), lambda j: (0, 0))
    blk = pl.BlockSpec((SCAN_CHUNKS, N_PAIRS, HEAD_SIZE, PAIR), lambda j: (j, 0, 0, 0))
    heads = (N_HEADS, HEAD_SIZE, HEAD_SIZE)
    st_in = pl.BlockSpec((None, SCAN_CHUNKS) + heads,
                         lambda j: (layer, jnp.maximum(j - n_prompt_steps, 0), 0, 0, 0))
    st_out = pl.BlockSpec((SCAN_CHUNKS,) + heads, lambda j: (jnp.maximum(j - n_prompt_steps, 0), 0, 0, 0))
    return pl.pallas_call(
        _scan_kernel,
        grid=(N_CHUNKS // SCAN_CHUNKS,),
        in_specs=[tok, tok, blk, blk, st_in, row, row],
        out_specs=[tok, pl.BlockSpec(heads, lambda j: (0, 0, 0)), st_out],
        out_shape=[jax.ShapeDtypeStruct((N_TOK, D_RWKV), _F32),
                   jax.ShapeDtypeStruct(heads, _F32),
                   jax.ShapeDtypeStruct((DEC_BATCH,) + heads, _F32)],
        scratch_shapes=[pltpu.VMEM((N_PAIRS, PAIR, PAIR), _F32)],
        compiler_params=pltpu.CompilerParams(
            dimension_semantics=("arbitrary",), vmem_limit_bytes=VMEM_LIMIT_BYTES),
        name="scan",
    )(rp, y0, m, n0, state_wkv, lnw, lnb)


def _merge_kernel(*refs):
    *x_refs, bgy_ref, yn_ref, g_ref, bg_ref, gate_ref, wc_ref, wr_ref, wo_ref, gpost_ref, o_ref = refs
    branch_a = _dot(bgy_ref[...].astype(_BF16), wc_ref[...])
    z = yn_ref[...] * g_ref[...] + bg_ref[...]
    branch_b = _dot(z.astype(_BF16), wr_ref[...])
    merged = (jax.nn.sigmoid(gate_ref[:, 0:D_MODEL]) * branch_a
              + jax.nn.sigmoid(gate_ref[:, D_MODEL:GATE_COLS]) * branch_b)
    m = _dot(merged.astype(_BF16), wo_ref[...])
    o_ref[...] = _load_tokens(x_refs) + _rms_scale(m) * gpost_ref[...]


def _merge(x_parts, bgy, yn, g, bg, gates, wc, wr, wo, gpost):
    tok = lambda width: pl.BlockSpec((TOKEN_TILE, width), lambda i: (i, 0))
    full = lambda a: pl.BlockSpec(a.shape, lambda i: (0,) * a.ndim)
    return pl.pallas_call(
        _merge_kernel,
        grid=(N_TOK // TOKEN_TILE,),
        in_specs=_token_specs(len(x_parts)) + [
            tok(D_CONV), tok(D_RWKV), tok(D_RWKV), tok(D_RWKV), tok(GATE_COLS),
            full(wc), full(wr), full(wo), full(gpost)],
        out_specs=tok(D_MODEL),
        out_shape=jax.ShapeDtypeStruct((N_TOK, D_MODEL), _F32),
        compiler_params=pltpu.CompilerParams(
            dimension_semantics=("arbitrary",), vmem_limit_bytes=VMEM_LIMIT_BYTES),
        name="merge",
    )(*x_parts, bgy, yn, g, bg, gates, wc, wr, wo, gpost)


def _ffn_kernel(x_ref, gpre_ref, wup_ref, wdown_ref, gpost_ref, *o_refs):
    x = x_ref[...]
    h = (_rms_scale(x) * gpre_ref[...]).astype(_BF16)
    up = _dot(h, wup_ref[...])
    act = jax.nn.silu(up[:, :D_FF]) * up[:, D_FF:]
    f = _dot(act.astype(_BF16), wdown_ref[...])
    out = x + _rms_scale(f) * gpost_ref[...]
    if len(o_refs) == 1:
        o_refs[0][...] = out
        return
    prompt_ref, stream_ref = o_refs
    is_stream = pl.program_id(0) >= N_PROMPT_TILES

    @pl.when(jnp.logical_not(is_stream))
    def _():
        prompt_ref[0] = out

    @pl.when(is_stream)
    def _():
        stream_ref[...] = out.reshape(STREAMS_PER_TILE, DEC_SEQ, D_MODEL)


def _ffn(x, gpre, wup, wdown, gpost, split_out):
    tok = pl.BlockSpec((TOKEN_TILE, D_MODEL), lambda i: (i, 0))
    full = lambda a: pl.BlockSpec(a.shape, lambda i: (0,) * a.ndim)
    if split_out:
        out_specs = _token_specs(2)
        out_shape = [jax.ShapeDtypeStruct((1, SEQ, D_MODEL), _F32),
                     jax.ShapeDtypeStruct((DEC_BATCH, DEC_SEQ, D_MODEL), _F32)]
    else:
        out_specs = tok
        out_shape = jax.ShapeDtypeStruct((N_TOK, D_MODEL), _F32)
    return pl.pallas_call(
        _ffn_kernel,
        grid=(N_TOK // TOKEN_TILE,),
        in_specs=[tok, full(gpre), full(wup), full(wdown), full(gpost)],
        out_specs=out_specs,
        out_shape=out_shape,
        compiler_params=pltpu.CompilerParams(
            dimension_semantics=("arbitrary",), vmem_limit_bytes=VMEM_LIMIT_BYTES),
        name="ffn",
    )(x, gpre, wup, wdown, gpost)


def _lora_weight(w_decay2, a2, g2):
    w = jnp.zeros((D_LORA, 3 * D_RWKV), _F32)
    w = w.at[0:D_DECAY_LORA, 0:D_RWKV].set(w_decay2)
    w = w.at[D_DECAY_LORA:D_DECAY_LORA + D_AAA_LORA, D_RWKV:2 * D_RWKV].set(a2)
    w = w.at[D_DECAY_LORA + D_AAA_LORA:, 2 * D_RWKV:].set(g2)
    hi = w.astype(_BF16)
    lo = (w - hi.astype(_F32)).astype(_BF16)
    return hi, lo


def kernel(x_prompt, x_sample, state_conv, state_shift, state_wkv, norm_mix_pre, norm_mix_post, w_in, mu_shift, conv_w, w_decay0, w_decay2, a0, a2, g2, k_k, k_a, r_k, ln_x_w, ln_x_b, w_conv_out, w_rwkv_out, w_o, norm_ffn_pre, norm_ffn_post, w_ffn_up, w_ffn_down):
    x_parts = (x_prompt, x_sample)
    row = lambda a: a.reshape(1, -1)
    conv_p, shift_p, wkv_p, conv_s, shift_s, wkv_s = [], [], [], [], [], []
    for l in range(DEPTH):
        lora_hi, lora_lo = _lora_weight(w_decay2[l], a2[l], g2[l])
        mix, gates = _proj(x_parts, row(norm_mix_pre[l]), w_in[l].astype(_BF16))
        (bgy, rp, y0, g, bg, m, n0, conv_out, shift_out) = _intra(
            mix, state_conv[l], state_shift[l][:, None], row(mu_shift[l]), conv_w[l], lora_hi, lora_lo,
            row(w_decay0[l]), row(a0[l]), row(k_k[l]), row(k_a[l]), row(r_k[l]))
        yn, s_prompt, s_sample = _scan(rp, y0, m, n0, state_wkv, l, row(ln_x_w[l]), row(ln_x_b[l]))
        x = _merge(x_parts, bgy, yn, g, bg, gates, w_conv_out[l].astype(_BF16), w_rwkv_out[l].astype(_BF16),
                   w_o[l].astype(_BF16), row(norm_mix_post[l]))
        x = _ffn(x, row(norm_ffn_pre[l]), w_ffn_up[l].astype(_BF16), w_ffn_down[l].astype(_BF16),
                 row(norm_ffn_post[l]), split_out=(l == DEPTH - 1))
        x_parts = (x,)

        last = N_PROMPT_CHUNKS - 1
        conv_p.append(conv_out[last:last + 1])
        conv_s.append(conv_out[N_PROMPT_CHUNKS:])
        shift_p.append(shift_out[last:last + 1, 0])
        shift_s.append(shift_out[N_PROMPT_CHUNKS:, 0])
        wkv_p.append(s_prompt[None])
        wkv_s.append(s_sample)
    y_prompt, y_sample = x
    return (y_prompt, y_sample,
            jnp.stack(conv_p, 0), jnp.stack(shift_p, 0), jnp.stack(wkv_p, 0),
            jnp.stack(conv_s, 0), jnp.stack(shift_s, 0), jnp.stack(wkv_s, 0))
```

```python
import functools

import jax
import jax.numpy as jnp
from jax import lax
from jax.experimental import pallas as pl
from jax.experimental.pallas import tpu as pltpu

D_MODEL = 1024
SEQ = 16384
DEPTH = 2
DEC_BATCH = 32
DEC_SEQ = 64
D_CONV = 512
D_RWKV = 512
HEAD_SIZE = 64
N_HEADS = 8
D_DECAY_LORA = 64
D_AAA_LORA = 64
D_GATE_LORA = 128
D_LORA = D_DECAY_LORA + D_AAA_LORA + D_GATE_LORA
LN_X_EPS = 64e-5
D_FF = 2816
RMS_EPS = 1e-6
RWKV_COLS = 3 * D_RWKV + D_LORA
MIX_COLS = 3 * D_CONV + RWKV_COLS
GATE_COLS = 2 * D_MODEL

CHUNK = 64
N_TOK = SEQ + DEC_BATCH * DEC_SEQ
N_CHUNKS = N_TOK // CHUNK
N_PROMPT_CHUNKS = SEQ // CHUNK
PAIR = 2 * HEAD_SIZE
N_PAIRS = N_HEADS // 2
TOKEN_TILE = 256
N_PROMPT_TILES = SEQ // TOKEN_TILE
STREAMS_PER_TILE = TOKEN_TILE // DEC_SEQ
INTRA_CHUNKS = 8
INTRA_GROUP = 2
SCAN_CHUNKS = 8
SUBLANES = 8
PIECES_A = 1
PIECES_INV = 1
PIECES_OUT = 1
VMEM_LIMIT_BYTES = 56 * 1024 * 1024

_BF16 = jnp.bfloat16
_F32 = jnp.float32
_NN = (((1,), (0,)), ((), ()))
_NT = (((1,), (1,)), ((), ()))
_TN = (((0,), (0,)), ((), ()))


def _dot(a, b, dims=_NN):
    return lax.dot_general(a, b, dims, preferred_element_type=_F32)


def _pieces(x, n):
    out = []
    for _ in range(n - 1):
        hi = x.astype(_BF16)
        out.append(hi)
        x = x - hi.astype(_F32)
    out.append(x.astype(_BF16))
    return tuple(out)


def _split2(x):
    return _pieces(x, 2)


def _split3(x):
    return _pieces(x, 3)


def _dot_pieces(a, b, dims=_NN):
    n = max(len(a), len(b))
    terms = [_dot(a[i], b[j], dims) for i in range(len(a)) for j in range(len(b)) if i + j < n]
    total = terms[0]
    if len(terms) > 1:
        rest = terms[1]
        for t in terms[2:]:
            rest = rest + t
        total = total + rest
    return total


def _dot_f32(a, b, dims=_NN):
    return _dot_pieces(_split2(a), _split2(b), dims)


def _run_steps(steps):
    while True:
        try:
            next(steps)
        except StopIteration as stop:
            return stop.value


def _zip_steps(a, b):
    values, live = {}, {"a": a, "b": b}
    while live:
        for name in ("a", "b"):
            if name in live:
                try:
                    next(live[name])
                except StopIteration as stop:
                    values[name] = stop.value
                    del live[name]
    return values["a"], values["b"]


def _rms_scale(x):
    return x * lax.rsqrt(jnp.mean(x * x, axis=-1, keepdims=True) + RMS_EPS)


def _iota(shape, dim):
    return lax.broadcasted_iota(jnp.int32, shape, dim)


def _first_head(width=PAIR):
    return _iota((1, width), 1) % PAIR < HEAD_SIZE


def _head_sums(x):
    first = _first_head()
    outs = []
    for p in range(x.shape[1] // PAIR):
        xp = x[:, p * PAIR:(p + 1) * PAIR]
        lo = jnp.sum(jnp.where(first, xp, 0.0), axis=1, keepdims=True)
        hi = jnp.sum(jnp.where(first, 0.0, xp), axis=1, keepdims=True)
        outs.append(jnp.where(first, lo, hi))
    return jnp.concatenate(outs, axis=1)


def _by_head_rows(x):
    first = _first_head()
    return jnp.concatenate([jnp.where(first, x, 0.0), jnp.where(first, 0.0, x)], axis=0)


def _diag_blocks(x):
    return jnp.where(_first_head(), x[0:HEAD_SIZE], x[HEAD_SIZE:PAIR])


def _layer_spec(a, layer):
    return pl.BlockSpec((None,) + a.shape[1:], lambda i: (layer,) + (0,) * (a.ndim - 1),
                        pipeline_mode=pl.Buffered(1))


def _token_specs(n_parts):
    if n_parts == 1:
        return [pl.BlockSpec((TOKEN_TILE, D_MODEL), lambda i: (i, 0))]
    return [pl.BlockSpec((1, TOKEN_TILE, D_MODEL), lambda i: (0, jnp.minimum(i, N_PROMPT_TILES - 1), 0)),
            pl.BlockSpec((STREAMS_PER_TILE, DEC_SEQ, D_MODEL),
                         lambda i: (jnp.maximum(i - N_PROMPT_TILES, 0), 0, 0))]


def _load_tokens(x_refs):
    if len(x_refs) == 1:
        return x_refs[0][...]
    prompt_ref, stream_ref = x_refs
    return jnp.where(pl.program_id(0) >= N_PROMPT_TILES,
                     stream_ref[...].reshape(TOKEN_TILE, D_MODEL), prompt_ref[0])


def _proj_kernel(*refs):
    *x_refs, g_ref, w_ref, mix_ref, gate_ref = refs
    h = (_rms_scale(_load_tokens(x_refs)) * g_ref[...]).astype(_BF16)
    p = _dot(h, w_ref[...])
    mix_ref[...] = p[:, :MIX_COLS]
    gate_ref[...] = p[:, MIX_COLS:]


def _proj(x_parts, layer, g, w_in_bf16):
    return pl.pallas_call(
        _proj_kernel,
        grid=(N_TOK // TOKEN_TILE,),
        in_specs=_token_specs(len(x_parts)) + [_layer_spec(g, layer), _layer_spec(w_in_bf16, layer)],
        out_specs=[
            pl.BlockSpec((TOKEN_TILE, MIX_COLS), lambda i: (i, 0)),
            pl.BlockSpec((TOKEN_TILE, GATE_COLS), lambda i: (i, 0)),
        ],
        out_shape=[
            jax.ShapeDtypeStruct((N_TOK, MIX_COLS), _F32),
            jax.ShapeDtypeStruct((N_TOK, GATE_COLS), _F32),
        ],
        compiler_params=pltpu.CompilerParams(
            dimension_semantics=("arbitrary",), vmem_limit_bytes=VMEM_LIMIT_BYTES),
        name="proj",
    )(*x_parts, g, w_in_bf16)


def _running_log_decay(lw):
    tri = jnp.where(_iota((CHUNK, CHUNK), 1) <= _iota((CHUNK, CHUNK), 0), 1.0, 0.0).astype(_BF16)
    return _dot_pieces((tri,), _split3(lw))


def _decayed_operands(r, kf, v, lw, cum, kk, a_sig):
    last = cum[CHUNK - 1:CHUNK]
    e_inv = jnp.exp(-cum)
    e_tail = jnp.exp(last - cum)
    b = kk * a_sig
    return dict(a_t=-kk * jnp.exp(cum - lw), r_t=r * jnp.exp(cum), b_t=b * e_inv, k_t=kf * e_inv,
                b_h=b * e_tail, k_h=kf * e_tail, v=v, dec=jnp.exp(last))


def _intra_pairs(chunks):
    first = _first_head()
    first2 = _first_head(2 * PAIR)
    g_row = _iota((4 * CHUNK, PAIR), 0)
    keep = (_iota((4 * CHUNK, PAIR), 1) % CHUNK) < (g_row % CHUNK) + (g_row // CHUNK) % 2
    rc_xor = _iota((PAIR, PAIR), 0) ^ _iota((PAIR, PAIR), 1)
    eye = rc_xor == 0
    zeros = jnp.zeros((CHUNK, PAIR), _F32)
    items = [(c, p) for c in range(len(chunks)) for p in range(N_PAIRS)]
    part = lambda it, name: chunks[it[0]][name][:, it[1] * PAIR:(it[1] + 1) * PAIR]
    out_pieces = lambda x: _pieces(x, PIECES_OUT)

    gm = {}
    for it in items:
        ar = jnp.concatenate([part(it, "a_t"), part(it, "r_t")], axis=0)
        bk = jnp.concatenate([part(it, "b_t"), part(it, "k_t")], axis=0)
        ar_split = jnp.concatenate([jnp.where(first, ar, 0.0), jnp.where(first, 0.0, ar)], axis=0)
        gm[it] = jnp.where(keep, _dot_pieces(_pieces(ar_split, PIECES_A), _pieces(bk, PIECES_A), _NT), 0.0)
    yield

    l_bd, inv = {}, {}
    for it in items:
        l0, l1 = gm[it][0:CHUNK], gm[it][2 * CHUNK:3 * CHUNK]
        l_bd[it] = jnp.concatenate([jnp.where(first, l0, 0.0),
                                    jnp.where(first, 0.0, pltpu.roll(l1, HEAD_SIZE, axis=1))], axis=0)
        inv[it] = jnp.where(eye, 1.0, jnp.where(rc_xor == 1, l_bd[it], 0.0))
    yield
    m = 2
    while m < CHUNK:
        level = jnp.logical_and(rc_xor >= m, rc_xor < 2 * m)
        inv_pieces = {it: _pieces(inv[it], PIECES_INV) for it in items}
        x = {it: _dot_pieces(_pieces(jnp.where(level, l_bd[it], 0.0), PIECES_INV), inv_pieces[it])
             for it in items}
        yield
        for it in items:
            inv[it] = inv[it] + _dot_pieces(inv_pieces[it], _pieces(x[it], PIECES_INV))
        yield
        m *= 2

    akv = {}
    for it in items:
        v_p = part(it, "v")
        vv_pieces = out_pieces(jnp.concatenate([v_p, v_p], axis=0))
        l0, l1 = gm[it][0:CHUNK], gm[it][2 * CHUNK:3 * CHUNK]
        akv[it] = jnp.where(first, _dot_pieces(out_pieces(jnp.where(first, 0.0, l0)), vv_pieces),
                            _dot_pieces(out_pieces(jnp.where(first, 0.0, l1)), vv_pieces))
    yield
    uva_pieces = {}
    for it in items:
        tx = _dot_pieces(out_pieces(inv[it]), out_pieces(jnp.concatenate(
            [_by_head_rows(akv[it]), _by_head_rows(part(it, "a_t"))], axis=1)))
        u0a = tx[0:CHUNK] + tx[CHUNK:2 * CHUNK]
        uva = jnp.concatenate([u0a, jnp.concatenate([part(it, "v"), zeros], axis=1)], axis=0)
        uva_pieces[it] = out_pieces(uva)
    yield
    rps, y0s, mts, n0s = {}, {}, {}, {}
    for it in items:
        rr0, rr1 = gm[it][CHUNK:2 * CHUNK], gm[it][3 * CHUNK:4 * CHUNK]
        yr = jnp.where(first2, _dot_pieces(out_pieces(rr0), uva_pieces[it]),
                       _dot_pieces(out_pieces(rr1), uva_pieces[it]))
        y0s[it] = yr[:, 0:PAIR]
        rps[it] = part(it, "r_t") + yr[:, PAIR:2 * PAIR]
    yield
    for it in items:
        bk_h = jnp.concatenate([part(it, "b_h"), part(it, "k_h")], axis=0)
        nm = _dot_pieces(out_pieces(bk_h), uva_pieces[it], _TN)
        n0s[it] = _diag_blocks(nm[:, 0:PAIR])
        mts[it] = _diag_blocks(nm[:, PAIR:2 * PAIR] + jnp.where(eye, part(it, "dec"), 0.0))
    cat = lambda d, c: jnp.concatenate([d[(c, p)] for p in range(N_PAIRS)], axis=1)
    return ([cat(rps, c) for c in range(len(chunks))], [cat(y0s, c) for c in range(len(chunks))], mts, n0s)


def _intra_kernel(mix_ref, prev_ref, conv0_ref, shift0_ref, mu_ref, cw_ref, lora_w_ref,
                  wd0_ref, a0_ref, kk_w_ref, ka_w_ref, rk_w_ref,
                  bgy_ref, rp_ref, y0_ref, g_ref, bg_ref, mt_ref, n0_ref, conv_out_ref, shift_out_ref):
    i = pl.program_id(0)
    is_sample = i >= N_PROMPT_CHUNKS // INTRA_CHUNKS
    row = _iota((CHUNK, 1), 0)
    cw = cw_ref[...]
    lane = _iota((CHUNK, D_LORA), 1)

    prev = jnp.where(i == 0, 0.0, prev_ref[SUBLANES - 2:SUBLANES, :])
    hist = dict(u=prev[:, 2 * D_CONV:3 * D_CONV] * prev[:, 0:D_CONV],
                pr=prev[1:2, 3 * D_CONV:MIX_COLS])

    def prepare(c):
        rows = slice(c * CHUNK, (c + 1) * CHUNK)
        u_hist = jnp.where(is_sample, conv0_ref[c], hist["u"])
        pr_hist = jnp.where(is_sample, shift0_ref[c], hist["pr"])

        xin = mix_ref[rows, 0:D_CONV]
        bg = mix_ref[rows, D_CONV:2 * D_CONV]
        cg = mix_ref[rows, 2 * D_CONV:3 * D_CONV]
        u = cg * xin
        u1 = jnp.where(row == 0, u_hist[1:2], pltpu.roll(u, 1, axis=0))
        u2 = jnp.where(row == 0, u_hist[0:1], jnp.where(row == 1, u_hist[1:2], pltpu.roll(u, 2, axis=0)))
        bgy_ref[rows, :] = bg * (cw[0:1] * u2 + cw[1:2] * u1 + cw[2:3] * u)
        hist["u"] = u[CHUNK - 2:CHUNK]
        conv_out_ref[c] = hist["u"]
        yield

        pr = mix_ref[rows, 3 * D_CONV:MIX_COLS]
        shifted = jnp.where(row == 0, pr_hist, pltpu.roll(pr, 1, axis=0))
        ps = pr + (shifted - pr) * mu_ref[...]
        hist["pr"] = pr[CHUNK - 1:CHUNK]
        shift_out_ref[c] = hist["pr"]
        r = ps[:, 0:D_RWKV]
        k = ps[:, D_RWKV:2 * D_RWKV]
        v = ps[:, 2 * D_RWKV:3 * D_RWKV]
        yield

        lora_in = ps[:, 3 * D_RWKV:RWKV_COLS]
        act = jnp.where(lane < D_DECAY_LORA, jnp.tanh(lora_in),
                        jnp.where(lane < D_DECAY_LORA + D_AAA_LORA, lora_in, jax.nn.sigmoid(lora_in)))
        lora = _dot(act.astype(_BF16), lora_w_ref[...])
        yield

        w_raw = wd0_ref[...] + lora[:, 0:D_RWKV]
        w_raw = -jax.nn.softplus(-w_raw) - 0.5
        lw = -jnp.exp(w_raw)
        cum = _running_log_decay(lw)
        yield

        a_sig = jax.nn.sigmoid(a0_ref[...] + lora[:, D_RWKV:2 * D_RWKV])
        g = lora[:, 2 * D_RWKV:3 * D_RWKV]
        g_ref[rows, :] = g
        kk = k * kk_w_ref[...]
        yield
        kk = kk / jnp.maximum(jnp.sqrt(_head_sums(kk * kk)), 1e-12)
        yield
        kf = k * (1.0 + (a_sig - 1.0) * ka_w_ref[...])
        bg_ref[rows, :] = _head_sums(r * kf * rk_w_ref[...]) * v * g
        yield
        return _decayed_operands(r, kf, v, lw, cum, kk, a_sig)

    def prepare_group(first_chunk):
        ops = []
        for c in range(first_chunk, first_chunk + INTRA_GROUP):
            ops.append((yield from prepare(c)))
        return ops

    def store(first_chunk, results):
        rps, y0s, mts, n0s = results
        for g in range(INTRA_GROUP):
            c = first_chunk + g
            rows = slice(c * CHUNK, (c + 1) * CHUNK)
            rp_ref[rows, :] = rps[g]
            y0_ref[rows, :] = y0s[g]
            for p in range(N_PAIRS):
                mt_ref[c, p] = mts[(g, p)]
                n0_ref[c, p] = n0s[(g, p)]

    operands = _run_steps(prepare_group(0))
    for first_chunk in range(0, INTRA_CHUNKS, INTRA_GROUP):
        pair_steps = _intra_pairs(operands)
        if first_chunk + INTRA_GROUP < INTRA_CHUNKS:
            results, operands = _zip_steps(pair_steps, prepare_group(first_chunk + INTRA_GROUP))
        else:
            results = _run_steps(pair_steps)
        store(first_chunk, results)


def _intra(mix, layer, conv0, shift0, mu, cw, lora_w, wd0, a0, kk_w, ka_w, rk_w):
    rows = INTRA_CHUNKS * CHUNK
    n_prompt_steps = N_PROMPT_CHUNKS // INTRA_CHUNKS
    tok = lambda width: pl.BlockSpec((rows, width), lambda i: (i, 0))
    full = lambda a: _layer_spec(a, layer)
    seq = lambda a: pl.BlockSpec((None, INTRA_CHUNKS) + a.shape[2:],
                                 lambda i: (layer, jnp.maximum(i - n_prompt_steps, 0), 0, 0))
    per_chunk = lambda *dims: pl.BlockSpec((INTRA_CHUNKS,) + dims, lambda i: (i,) + (0,) * len(dims))
    prev = pl.BlockSpec((SUBLANES, MIX_COLS),
                        lambda i: (jnp.maximum(i * (rows // SUBLANES) - 1, 0), 0))
    tok_out = jax.ShapeDtypeStruct((N_TOK, D_RWKV), _F32)
    blocks = jax.ShapeDtypeStruct((N_CHUNKS, N_PAIRS, HEAD_SIZE, PAIR), _F32)
    return pl.pallas_call(
        _intra_kernel,
        grid=(N_CHUNKS // INTRA_CHUNKS,),
        in_specs=[tok(MIX_COLS), prev, seq(conv0), seq(shift0), full(mu), full(cw), full(lora_w),
                  full(wd0), full(a0), full(kk_w), full(ka_w), full(rk_w)],
        out_specs=[tok(D_CONV)] + [tok(D_RWKV)] * 4 + [
            per_chunk(N_PAIRS, HEAD_SIZE, PAIR), per_chunk(N_PAIRS, HEAD_SIZE, PAIR),
            per_chunk(2, D_CONV), per_chunk(1, RWKV_COLS)],
        out_shape=[jax.ShapeDtypeStruct((N_TOK, D_CONV), _F32)] + [tok_out] * 4 + [
            blocks, blocks,
            jax.ShapeDtypeStruct((N_CHUNKS, 2, D_CONV), _F32),
            jax.ShapeDtypeStruct((N_CHUNKS, 1, RWKV_COLS), _F32)],
        compiler_params=pltpu.CompilerParams(
            dimension_semantics=("arbitrary",), vmem_limit_bytes=VMEM_LIMIT_BYTES),
        name="intra",
    )(mix, mix, conv0, shift0, mu, cw, lora_w, wd0, a0, kk_w, ka_w, rk_w)


def _scan_kernel(rp_ref, y0_ref, mt_ref, n0_ref, s0_ref, lnw_ref, lnb_ref,
                 yn_ref, s_prompt_ref, s_sample_ref, state):
    j = pl.program_id(0)
    is_stream = j >= N_PROMPT_CHUNKS // SCAN_CHUNKS

    @pl.when(j == 0)
    def _():
        state[...] = jnp.zeros_like(state)

    def advance(c, p, w):
        rows, sl = slice(c * CHUNK, (c + 1) * CHUNK), slice(p * PAIR, (p + 1) * PAIR)
        w_pieces = _split2(w)
        y = _dot_pieces(_split2(rp_ref[rows, sl]), w_pieces) + y0_ref[rows, sl]
        w_next = _dot_pieces(_split2(_by_head_rows(mt_ref[c, p])), w_pieces) + _by_head_rows(n0_ref[c, p])
        return y, w_next

    def group_norm(c, ys):
        y = jnp.concatenate(ys, axis=1)
        inv_n = 1.0 / HEAD_SIZE
        d = y - _head_sums(y) * inv_n
        var = _head_sums(d * d) * inv_n
        yn_ref[c * CHUNK:(c + 1) * CHUNK, :] = d * lax.rsqrt(var + LN_X_EPS) * lnw_ref[...] + lnb_ref[...]

    eye = (_iota((PAIR, PAIR), 0) == _iota((PAIR, PAIR), 1)).astype(_BF16)

    def heads_of(s):
        return s[0:HEAD_SIZE, 0:HEAD_SIZE], s[HEAD_SIZE:PAIR, HEAD_SIZE:PAIR]

    @pl.when(jnp.logical_not(is_stream))
    def _():
        ws = [state[p] for p in range(N_PAIRS)]
        for c in range(SCAN_CHUNKS):
            ys = []
            for p in range(N_PAIRS):
                y, ws[p] = advance(c, p, ws[p])
                ys.append(y)
            group_norm(c, ys)
        for p in range(N_PAIRS):
            state[p] = ws[p]
            s_prompt_ref[2 * p], s_prompt_ref[2 * p + 1] = heads_of(_dot_pieces((eye,), _split3(ws[p]), _NT))

    @pl.when(is_stream)
    def _():
        zeros = jnp.zeros((HEAD_SIZE, HEAD_SIZE), _F32)
        for c in range(SCAN_CHUNKS):
            rows = slice(c * CHUNK, (c + 1) * CHUNK)
            ys = []
            for p in range(N_PAIRS):
                sl = slice(p * PAIR, (p + 1) * PAIR)
                s0 = jnp.concatenate([jnp.concatenate([s0_ref[c, 2 * p], zeros], axis=1),
                                      jnp.concatenate([zeros, s0_ref[c, 2 * p + 1]], axis=1)], axis=0)
                s0_hi, s0_lo = _split2(s0)
                ys.append(_dot_pieces(_split2(rp_ref[rows, sl]), (s0_hi, s0_lo), _NT) + y0_ref[rows, sl])
                mn = jnp.concatenate([_by_head_rows(mt_ref[c, p]), _by_head_rows(n0_ref[c, p])], axis=1)
                s_next = _dot_pieces((jnp.concatenate([s0_hi, eye], axis=1),
                                      jnp.concatenate([s0_lo, jnp.zeros_like(eye)], axis=1)),
                                     _split2(mn), _NT)
                s_sample_ref[c, 2 * p], s_sample_ref[c, 2 * p + 1] = heads_of(s_next)
            group_norm(c, ys)


def _scan(rp, y0, m, n0, state_wkv, layer, lnw, lnb):
    rows = SCAN_CHUNKS * CHUNK
    n_prompt_steps = N_PROMPT_CHUNKS // SCAN_CHUNKS
    tok = pl.BlockSpec((rows, D_RWKV), lambda j: (j, 0))
    blk = pl.BlockSpec((SCAN_CHUNKS, N_PAIRS, HEAD_SIZE, PAIR), lambda j: (j, 0, 0, 0))
    heads = (N_HEADS, HEAD_SIZE, HEAD_SIZE)
    st_in = pl.BlockSpec((None, SCAN_CHUNKS) + heads,
                         lambda j: (layer, jnp.maximum(j - n_prompt_steps, 0), 0, 0, 0))
    st_out = pl.BlockSpec((SCAN_CHUNKS,) + heads, lambda j: (jnp.maximum(j - n_prompt_steps, 0), 0, 0, 0))
    return pl.pallas_call(
        _scan_kernel,
        grid=(N_CHUNKS // SCAN_CHUNKS,),
        in_specs=[tok, tok, blk, blk, st_in, _layer_spec(lnw, layer), _layer_spec(lnb, layer)],
        out_specs=[tok, pl.BlockSpec(heads, lambda j: (0, 0, 0)), st_out],
        out_shape=[jax.ShapeDtypeStruct((N_TOK, D_RWKV), _F32),
                   jax.ShapeDtypeStruct(heads, _F32),
                   jax.ShapeDtypeStruct((DEC_BATCH,) + heads, _F32)],
        scratch_shapes=[pltpu.VMEM((N_PAIRS, PAIR, PAIR), _F32)],
        compiler_params=pltpu.CompilerParams(
            dimension_semantics=("arbitrary",), vmem_limit_bytes=VMEM_LIMIT_BYTES),
        name="scan",
    )(rp, y0, m, n0, state_wkv, lnw, lnb)


def _post_kernel(n_x, *refs):
    x_refs, refs = refs[:n_x], refs[n_x:]
    (bgy_ref, yn_ref, g_ref, bg_ref, gate_ref,
     wc_ref, wr_ref, wo_ref, gmix_ref, gpre_ref, wup_ref, wdown_ref, gpost_ref, *o_refs) = refs

    branch_a = _dot(bgy_ref[...].astype(_BF16), wc_ref[...])
    z = yn_ref[...] * g_ref[...] + bg_ref[...]
    branch_b = _dot(z.astype(_BF16), wr_ref[...])
    merged = (jax.nn.sigmoid(gate_ref[:, 0:D_MODEL]) * branch_a
              + jax.nn.sigmoid(gate_ref[:, D_MODEL:GATE_COLS]) * branch_b)
    m = _dot(merged.astype(_BF16), wo_ref[...])
    x = _load_tokens(x_refs) + _rms_scale(m) * gmix_ref[...]

    h = (_rms_scale(x) * gpre_ref[...]).astype(_BF16)
    up = _dot(h, wup_ref[...])
    act = jax.nn.silu(up[:, :D_FF]) * up[:, D_FF:]
    f = _dot(act.astype(_BF16), wdown_ref[...])
    out = x + _rms_scale(f) * gpost_ref[...]
    if len(o_refs) == 1:
        o_refs[0][...] = out
        return
    prompt_ref, stream_ref = o_refs
    is_stream = pl.program_id(0) >= N_PROMPT_TILES

    @pl.when(jnp.logical_not(is_stream))
    def _():
        prompt_ref[0] = out

    @pl.when(is_stream)
    def _():
        stream_ref[...] = out.reshape(STREAMS_PER_TILE, DEC_SEQ, D_MODEL)


def _post(x_parts, layer, token_inputs, params, split_out):
    tok = lambda a: pl.BlockSpec((TOKEN_TILE, a.shape[1]), lambda i: (i, 0))
    if split_out:
        out_specs = _token_specs(2)
        out_shape = [jax.ShapeDtypeStruct((1, SEQ, D_MODEL), _F32),
                     jax.ShapeDtypeStruct((DEC_BATCH, DEC_SEQ, D_MODEL), _F32)]
    else:
        out_specs = _token_specs(1)
        out_shape = [jax.ShapeDtypeStruct((N_TOK, D_MODEL), _F32)]
    return pl.pallas_call(
        functools.partial(_post_kernel, len(x_parts)),
        grid=(N_TOK // TOKEN_TILE,),
        in_specs=(_token_specs(len(x_parts)) + [tok(a) for a in token_inputs]
                  + [_layer_spec(a, layer) for a in params]),
        out_specs=out_specs,
        out_shape=out_shape,
        compiler_params=pltpu.CompilerParams(
            dimension_semantics=("arbitrary",), vmem_limit_bytes=VMEM_LIMIT_BYTES),
        name="post",
    )(*x_parts, *token_inputs, *params)


def _lora_weight(w_decay2, a2, g2):
    w = jnp.zeros((DEPTH, D_LORA, 3 * D_RWKV), _BF16)
    w = w.at[:, 0:D_DECAY_LORA, 0:D_RWKV].set(w_decay2.astype(_BF16))
    w = w.at[:, D_DECAY_LORA:D_DECAY_LORA + D_AAA_LORA, D_RWKV:2 * D_RWKV].set(a2.astype(_BF16))
    return w.at[:, D_DECAY_LORA + D_AAA_LORA:, 2 * D_RWKV:].set(g2.astype(_BF16))


def kernel(x_prompt, x_sample, state_conv, state_shift, state_wkv, norm_mix_pre, norm_mix_post, w_in, mu_shift, conv_w, w_decay0, w_decay2, a0, a2, g2, k_k, k_a, r_k, ln_x_w, ln_x_b, w_conv_out, w_rwkv_out, w_o, norm_ffn_pre, norm_ffn_post, w_ffn_up, w_ffn_down):
    x_parts = (x_prompt, x_sample)
    rows = lambda a: a.reshape(DEPTH, 1, -1)
    bf16 = lambda a: a.astype(_BF16)
    lora_w = _lora_weight(w_decay2, a2, g2)
    w_in_b = bf16(w_in)
    intra_params = (rows(mu_shift), conv_w, lora_w, rows(w_decay0), rows(a0), rows(k_k), rows(k_a), rows(r_k))
    post_params = (bf16(w_conv_out), bf16(w_rwkv_out), bf16(w_o), rows(norm_mix_post), rows(norm_ffn_pre),
                   bf16(w_ffn_up), bf16(w_ffn_down), rows(norm_ffn_post))
    norm_pre, lnw, lnb = rows(norm_mix_pre), rows(ln_x_w), rows(ln_x_b)
    shift0 = state_shift[:, :, None]
    conv_p, shift_p, wkv_p, conv_s, shift_s, wkv_s = [], [], [], [], [], []
    for l in range(DEPTH):
        mix, gates = _proj(x_parts, l, norm_pre, w_in_b)
        (bgy, rp, y0, g, bg, m, n0, conv_out, shift_out) = _intra(mix, l, state_conv, shift0, *intra_params)
        yn, s_prompt, s_sample = _scan(rp, y0, m, n0, state_wkv, l, lnw, lnb)
        x_parts = _post(x_parts, l, (bgy, yn, g, bg, gates), post_params, split_out=(l == DEPTH - 1))

        last = N_PROMPT_CHUNKS - 1
        conv_p.append(conv_out[last:last + 1])
        conv_s.append(conv_out[N_PROMPT_CHUNKS:])
        shift_p.append(shift_out[last:last + 1, 0])
        shift_s.append(shift_out[N_PROMPT_CHUNKS:, 0])
        wkv_p.append(s_prompt[None])
        wkv_s.append(s_sample)
    y_prompt, y_sample = x_parts
    return (y_prompt, y_sample,
            jnp.stack(conv_p, 0), jnp.stack(shift_p, 0), jnp.stack(wkv_p, 0),
            jnp.stack(conv_s, 0), jnp.stack(shift_s, 0), jnp.stack(wkv_s, 0))
```

```python
import functools

import jax
import jax.numpy as jnp
from jax import lax
from jax.experimental import pallas as pl
from jax.experimental.pallas import tpu as pltpu

D_MODEL = 1024
SEQ = 16384
DEPTH = 2
DEC_BATCH = 32
DEC_SEQ = 64
D_CONV = 512
D_RWKV = 512
HEAD_SIZE = 64
N_HEADS = 8
D_DECAY_LORA = 64
D_AAA_LORA = 64
D_GATE_LORA = 128
D_LORA = D_DECAY_LORA + D_AAA_LORA + D_GATE_LORA
LN_X_EPS = 64e-5
D_FF = 2816
RMS_EPS = 1e-6
RWKV_COLS = 3 * D_RWKV + D_LORA
MIX_COLS = 3 * D_CONV + RWKV_COLS
GATE_COLS = 2 * D_MODEL

CHUNK = 64
N_TOK = SEQ + DEC_BATCH * DEC_SEQ
N_CHUNKS = N_TOK // CHUNK
N_PROMPT_CHUNKS = SEQ // CHUNK
PAIR = 2 * HEAD_SIZE
N_PAIRS = N_HEADS // 2
TOKEN_TILE = 256
PROJ_TILE = 512
N_PROMPT_TILES = SEQ // TOKEN_TILE
STREAMS_PER_TILE = TOKEN_TILE // DEC_SEQ
TILE_CHUNKS = TOKEN_TILE // CHUNK
INTRA_CHUNKS = 8
INTRA_GROUP = 4
SUBLANES = 8
PIECES_A = 1
PIECES_INV = 1
PIECES_OUT = 1
VMEM_LIMIT_BYTES = 56 * 1024 * 1024

_BF16 = jnp.bfloat16
_F32 = jnp.float32
_NN = (((1,), (0,)), ((), ()))
_NT = (((1,), (1,)), ((), ()))
_TN = (((0,), (0,)), ((), ()))


def _dot(a, b, dims=_NN):
    return lax.dot_general(a, b, dims, preferred_element_type=_F32)


def _pieces(x, n):
    out = []
    for _ in range(n - 1):
        hi = x.astype(_BF16)
        out.append(hi)
        x = x - hi.astype(_F32)
    out.append(x.astype(_BF16))
    return tuple(out)


def _split2(x):
    return _pieces(x, 2)


def _split3(x):
    return _pieces(x, 3)


def _dot_pieces(a, b, dims=_NN):
    n = max(len(a), len(b))
    terms = [_dot(a[i], b[j], dims) for i in range(len(a)) for j in range(len(b)) if i + j < n]
    total = terms[0]
    if len(terms) > 1:
        rest = terms[1]
        for t in terms[2:]:
            rest = rest + t
        total = total + rest
    return total


def _dot_f32(a, b, dims=_NN):
    return _dot_pieces(_split2(a), _split2(b), dims)


def _run_steps(steps):
    while True:
        try:
            next(steps)
        except StopIteration as stop:
            return stop.value


def _zip_steps(a, b):
    values, live = {}, {"a": a, "b": b}
    while live:
        for name in ("a", "b"):
            if name in live:
                try:
                    next(live[name])
                except StopIteration as stop:
                    values[name] = stop.value
                    del live[name]
    return values["a"], values["b"]


def _rms_scale(x):
    return x * lax.rsqrt(jnp.mean(x * x, axis=-1, keepdims=True) + RMS_EPS)


def _iota(shape, dim):
    return lax.broadcasted_iota(jnp.int32, shape, dim)


def _first_head(width=PAIR):
    return _iota((1, width), 1) % PAIR < HEAD_SIZE


def _head_sums(x):
    first = _first_head()
    outs = []
    for p in range(x.shape[1] // PAIR):
        xp = x[:, p * PAIR:(p + 1) * PAIR]
        lo = jnp.sum(jnp.where(first, xp, 0.0), axis=1, keepdims=True)
        hi = jnp.sum(jnp.where(first, 0.0, xp), axis=1, keepdims=True)
        outs.append(jnp.where(first, lo, hi))
    return jnp.concatenate(outs, axis=1)


def _by_head_rows(x):
    first = _first_head()
    return jnp.concatenate([jnp.where(first, x, 0.0), jnp.where(first, 0.0, x)], axis=0)


def _diag_blocks(x):
    return jnp.where(_first_head(), x[0:HEAD_SIZE], x[HEAD_SIZE:PAIR])


def _layer_spec(a, layer):
    return pl.BlockSpec((None,) + a.shape[1:], lambda i: (layer,) + (0,) * (a.ndim - 1),
                        pipeline_mode=pl.Buffered(1))


def _token_specs(n_parts, tile=TOKEN_TILE):
    if n_parts == 1:
        return [pl.BlockSpec((tile, D_MODEL), lambda i: (i, 0))]
    n_prompt = SEQ // tile
    return [pl.BlockSpec((1, tile, D_MODEL), lambda i: (0, jnp.minimum(i, n_prompt - 1), 0)),
            pl.BlockSpec((tile // DEC_SEQ, DEC_SEQ, D_MODEL), lambda i: (jnp.maximum(i - n_prompt, 0), 0, 0))]


def _load_tokens(x_refs, tile=TOKEN_TILE):
    if len(x_refs) == 1:
        return x_refs[0][...]
    prompt_ref, stream_ref = x_refs
    return jnp.where(pl.program_id(0) >= SEQ // tile, stream_ref[...].reshape(tile, D_MODEL), prompt_ref[0])


def _proj_kernel(*refs):
    *x_refs, g_ref, w_ref, mix_ref, gate_ref = refs
    h = (_rms_scale(_load_tokens(x_refs, PROJ_TILE)) * g_ref[...]).astype(_BF16)
    p = _dot(h, w_ref[...])
    mix_ref[...] = p[:, :MIX_COLS]
    gate_ref[...] = p[:, MIX_COLS:]


def _proj(x_parts, layer, g, w_in_bf16):
    return pl.pallas_call(
        _proj_kernel,
        grid=(N_TOK // PROJ_TILE,),
        in_specs=_token_specs(len(x_parts), PROJ_TILE) + [_layer_spec(g, layer), _layer_spec(w_in_bf16, layer)],
        out_specs=[
            pl.BlockSpec((PROJ_TILE, MIX_COLS), lambda i: (i, 0)),
            pl.BlockSpec((PROJ_TILE, GATE_COLS), lambda i: (i, 0)),
        ],
        out_shape=[
            jax.ShapeDtypeStruct((N_TOK, MIX_COLS), _F32),
            jax.ShapeDtypeStruct((N_TOK, GATE_COLS), _F32),
        ],
        compiler_params=pltpu.CompilerParams(
            dimension_semantics=("arbitrary",), vmem_limit_bytes=VMEM_LIMIT_BYTES),
        name="proj",
    )(*x_parts, g, w_in_bf16)


def _running_log_decay(lw):
    tri = jnp.where(_iota((CHUNK, CHUNK), 1) <= _iota((CHUNK, CHUNK), 0), 1.0, 0.0).astype(_BF16)
    return _dot_pieces((tri,), _split3(lw))


def _decayed_operands(r, kf, v, lw, cum, kk, a_sig):
    last = cum[CHUNK - 1:CHUNK]
    e_inv = jnp.exp(-cum)
    e_tail = jnp.exp(last - cum)
    b = kk * a_sig
    return dict(a_t=-kk * jnp.exp(cum - lw), r_t=r * jnp.exp(cum), b_t=b * e_inv, k_t=kf * e_inv,
                b_h=b * e_tail, k_h=kf * e_tail, v=v, dec=jnp.exp(last))


def _intra_pairs(chunks):
    first = _first_head()
    first2 = _first_head(2 * PAIR)
    g_row = _iota((4 * CHUNK, PAIR), 0)
    keep = (_iota((4 * CHUNK, PAIR), 1) % CHUNK) < (g_row % CHUNK) + (g_row // CHUNK) % 2
    rc_xor = _iota((PAIR, PAIR), 0) ^ _iota((PAIR, PAIR), 1)
    eye = rc_xor == 0
    zeros = jnp.zeros((CHUNK, PAIR), _F32)
    items = [(c, p) for c in range(len(chunks)) for p in range(N_PAIRS)]
    part = lambda it, name: chunks[it[0]][name][:, it[1] * PAIR:(it[1] + 1) * PAIR]
    out_pieces = lambda x: _pieces(x, PIECES_OUT)

    gm = {}
    for it in items:
        ar = jnp.concatenate([part(it, "a_t"), part(it, "r_t")], axis=0)
        bk = jnp.concatenate([part(it, "b_t"), part(it, "k_t")], axis=0)
        ar_split = jnp.concatenate([jnp.where(first, ar, 0.0), jnp.where(first, 0.0, ar)], axis=0)
        gm[it] = jnp.where(keep, _dot_pieces(_pieces(ar_split, PIECES_A), _pieces(bk, PIECES_A), _NT), 0.0)
    yield

    l_bd, inv = {}, {}
    for it in items:
        l0, l1 = gm[it][0:CHUNK], gm[it][2 * CHUNK:3 * CHUNK]
        l_bd[it] = jnp.concatenate([jnp.where(first, l0, 0.0),
                                    jnp.where(first, 0.0, pltpu.roll(l1, HEAD_SIZE, axis=1))], axis=0)
        inv[it] = jnp.where(eye, 1.0, jnp.where(rc_xor == 1, l_bd[it], 0.0))
    yield
    m = 2
    while m < CHUNK:
        level = jnp.logical_and(rc_xor >= m, rc_xor < 2 * m)
        slabs = m >= SUBLANES
        low = (lambda a: jnp.concatenate([a[s:s + m] for s in range(m, PAIR, 2 * m)], axis=0)) if slabs else (lambda a: a)
        inv_pieces = {it: _pieces(inv[it], PIECES_INV) for it in items}
        x = {it: _dot_pieces(_pieces(low(jnp.where(level, l_bd[it], 0.0)), PIECES_INV), inv_pieces[it])
             for it in items}
        yield
        for it in items:
            if slabs:
                zeros_m = jnp.zeros((m, PAIR), _F32)
                x_rows = [x[it][j * m:(j + 1) * m] for j in range(PAIR // (2 * m))]
                x_full = jnp.concatenate([a for xr in x_rows for a in (zeros_m, xr)], axis=0)
                upd = _dot_pieces(_pieces(low(inv[it]), PIECES_INV), _pieces(x_full, PIECES_INV))
                inv[it] = jnp.concatenate(
                    [a for j in range(PAIR // (2 * m))
                     for a in (inv[it][2 * j * m:(2 * j + 1) * m],
                               inv[it][(2 * j + 1) * m:(2 * j + 2) * m] + upd[j * m:(j + 1) * m])], axis=0)
            else:
                inv[it] = inv[it] + _dot_pieces(inv_pieces[it], _pieces(x[it], PIECES_INV))
        yield
        m *= 2

    akv = {}
    for it in items:
        v_p = part(it, "v")
        vv_pieces = out_pieces(jnp.concatenate([v_p, v_p], axis=0))
        ak = jnp.where(first, 0.0, jnp.concatenate([gm[it][0:CHUNK], gm[it][2 * CHUNK:3 * CHUNK]], axis=0))
        akv2 = _dot_pieces(out_pieces(ak), vv_pieces)
        akv[it] = jnp.where(first, akv2[0:CHUNK], akv2[CHUNK:2 * CHUNK])
    yield
    uva_pieces = {}
    for it in items:
        tx = _dot_pieces(out_pieces(inv[it]), out_pieces(jnp.concatenate(
            [_by_head_rows(akv[it]), _by_head_rows(part(it, "a_t"))], axis=1)))
        u0a = tx[0:CHUNK] + tx[CHUNK:2 * CHUNK]
        uva = jnp.concatenate([u0a, jnp.concatenate([part(it, "v"), zeros], axis=1)], axis=0)
        uva_pieces[it] = out_pieces(uva)
    yield
    rps, y0s, mts, n0s = {}, {}, {}, {}
    for it in items:
        rr = jnp.concatenate([gm[it][CHUNK:2 * CHUNK], gm[it][3 * CHUNK:4 * CHUNK]], axis=0)
        yr2 = _dot_pieces(out_pieces(rr), uva_pieces[it])
        yr = jnp.where(first2, yr2[0:CHUNK], yr2[CHUNK:2 * CHUNK])
        y0s[it] = yr[:, 0:PAIR]
        rps[it] = part(it, "r_t") + yr[:, PAIR:2 * PAIR]
    yield
    for it in items:
        bk_h = jnp.concatenate([part(it, "b_h"), part(it, "k_h")], axis=0)
        nm = _dot_pieces(out_pieces(bk_h), uva_pieces[it], _TN)
        n0s[it] = _diag_blocks(nm[:, 0:PAIR])
        mts[it] = _diag_blocks(nm[:, PAIR:2 * PAIR] + jnp.where(eye, part(it, "dec"), 0.0))
    cat = lambda d, c: jnp.concatenate([d[(c, p)] for p in range(N_PAIRS)], axis=1)
    return ([cat(rps, c) for c in range(len(chunks))], [cat(y0s, c) for c in range(len(chunks))], mts, n0s)


def _intra_kernel(mix_ref, prev_ref, conv0_ref, shift0_ref, mu_ref, cw_ref, lora_w_ref,
                  wd0_ref, a0_ref, kk_w_ref, ka_w_ref, rk_w_ref,
                  bgy_ref, rp_ref, y0_ref, g_ref, bg_ref, mt_ref, n0_ref, conv_out_ref, shift_out_ref):
    i = pl.program_id(0)
    is_sample = i >= N_PROMPT_CHUNKS // INTRA_CHUNKS
    row = _iota((CHUNK, 1), 0)
    cw = cw_ref[...]
    lane = _iota((CHUNK, D_LORA), 1)

    prev = jnp.where(i == 0, 0.0, prev_ref[SUBLANES - 2:SUBLANES, :])
    hist = dict(u=prev[:, 2 * D_CONV:3 * D_CONV] * prev[:, 0:D_CONV],
                pr=prev[1:2, 3 * D_CONV:MIX_COLS])

    def prepare(c):
        rows = slice(c * CHUNK, (c + 1) * CHUNK)
        u_hist = jnp.where(is_sample, conv0_ref[c], hist["u"])
        pr_hist = jnp.where(is_sample, shift0_ref[c], hist["pr"])

        xin = mix_ref[rows, 0:D_CONV]
        bg = mix_ref[rows, D_CONV:2 * D_CONV]
        cg = mix_ref[rows, 2 * D_CONV:3 * D_CONV]
        u = cg * xin
        u1 = jnp.where(row == 0, u_hist[1:2], pltpu.roll(u, 1, axis=0))
        u2 = jnp.where(row == 0, u_hist[0:1], jnp.where(row == 1, u_hist[1:2], pltpu.roll(u, 2, axis=0)))
        bgy_ref[rows, :] = bg * (cw[0:1] * u2 + cw[1:2] * u1 + cw[2:3] * u)
        hist["u"] = u[CHUNK - 2:CHUNK]
        conv_out_ref[c] = hist["u"]
        yield

        pr = mix_ref[rows, 3 * D_CONV:MIX_COLS]
        shifted = jnp.where(row == 0, pr_hist, pltpu.roll(pr, 1, axis=0))
        ps = pr + (shifted - pr) * mu_ref[...]
        hist["pr"] = pr[CHUNK - 1:CHUNK]
        shift_out_ref[c] = hist["pr"]
        r = ps[:, 0:D_RWKV]
        k = ps[:, D_RWKV:2 * D_RWKV]
        v = ps[:, 2 * D_RWKV:3 * D_RWKV]
        yield

        lora_in = ps[:, 3 * D_RWKV:RWKV_COLS]
        act = jnp.where(lane < D_DECAY_LORA, jnp.tanh(lora_in),
                        jnp.where(lane < D_DECAY_LORA + D_AAA_LORA, lora_in, jax.nn.sigmoid(lora_in)))
        lora = _dot(act.astype(_BF16), lora_w_ref[...])
        yield

        w_raw = wd0_ref[...] + lora[:, 0:D_RWKV]
        w_raw = -jax.nn.softplus(-w_raw) - 0.5
        lw = -jnp.exp(w_raw)
        cum = _running_log_decay(lw)
        yield

        a_sig = jax.nn.sigmoid(a0_ref[...] + lora[:, D_RWKV:2 * D_RWKV])
        g = lora[:, 2 * D_RWKV:3 * D_RWKV]
        g_ref[rows, :] = g
        kk = k * kk_w_ref[...]
        yield
        kk = kk / jnp.maximum(jnp.sqrt(_head_sums(kk * kk)), 1e-12)
        yield
        kf = k * (1.0 + (a_sig - 1.0) * ka_w_ref[...])
        bg_ref[rows, :] = _head_sums(r * kf * rk_w_ref[...]) * v * g
        yield
        return _decayed_operands(r, kf, v, lw, cum, kk, a_sig)

    def prepare_group(first_chunk):
        ops = []
        for c in range(first_chunk, first_chunk + INTRA_GROUP):
            ops.append((yield from prepare(c)))
        return ops

    def store(first_chunk, results):
        rps, y0s, mts, n0s = results
        for g in range(INTRA_GROUP):
            c = first_chunk + g
            rows = slice(c * CHUNK, (c + 1) * CHUNK)
            rp_ref[rows, :] = rps[g]
            y0_ref[rows, :] = y0s[g]
            for p in range(N_PAIRS):
                mt_ref[c, p] = mts[(g, p)]
                n0_ref[c, p] = n0s[(g, p)]

    operands = _run_steps(prepare_group(0))
    for first_chunk in range(0, INTRA_CHUNKS, INTRA_GROUP):
        pair_steps = _intra_pairs(operands)
        if first_chunk + INTRA_GROUP < INTRA_CHUNKS:
            results, operands = _zip_steps(pair_steps, prepare_group(first_chunk + INTRA_GROUP))
        else:
            results = _run_steps(pair_steps)
        store(first_chunk, results)


def _intra(mix, layer, conv0, shift0, mu, cw, lora_w, wd0, a0, kk_w, ka_w, rk_w):
    rows = INTRA_CHUNKS * CHUNK
    n_prompt_steps = N_PROMPT_CHUNKS // INTRA_CHUNKS
    tok = lambda width: pl.BlockSpec((rows, width), lambda i: (i, 0))
    full = lambda a: _layer_spec(a, layer)
    seq = lambda a: pl.BlockSpec((None, INTRA_CHUNKS) + a.shape[2:],
                                 lambda i: (layer, jnp.maximum(i - n_prompt_steps, 0), 0, 0))
    per_chunk = lambda *dims: pl.BlockSpec((INTRA_CHUNKS,) + dims, lambda i: (i,) + (0,) * len(dims))
    prev = pl.BlockSpec((SUBLANES, MIX_COLS),
                        lambda i: (jnp.maximum(i * (rows // SUBLANES) - 1, 0), 0))
    tok_out = jax.ShapeDtypeStruct((N_TOK, D_RWKV), _F32)
    blocks = jax.ShapeDtypeStruct((N_CHUNKS, N_PAIRS, HEAD_SIZE, PAIR), _F32)
    return pl.pallas_call(
        _intra_kernel,
        grid=(N_CHUNKS // INTRA_CHUNKS,),
        in_specs=[tok(MIX_COLS), prev, seq(conv0), seq(shift0), full(mu), full(cw), full(lora_w),
                  full(wd0), full(a0), full(kk_w), full(ka_w), full(rk_w)],
        out_specs=[tok(D_CONV)] + [tok(D_RWKV)] * 4 + [
            per_chunk(N_PAIRS, HEAD_SIZE, PAIR), per_chunk(N_PAIRS, HEAD_SIZE, PAIR),
            per_chunk(2, D_CONV), per_chunk(1, RWKV_COLS)],
        out_shape=[jax.ShapeDtypeStruct((N_TOK, D_CONV), _F32)] + [tok_out] * 4 + [
            blocks, blocks,
            jax.ShapeDtypeStruct((N_CHUNKS, 2, D_CONV), _F32),
            jax.ShapeDtypeStruct((N_CHUNKS, 1, RWKV_COLS), _F32)],
        compiler_params=pltpu.CompilerParams(
            dimension_semantics=("arbitrary",), vmem_limit_bytes=VMEM_LIMIT_BYTES),
        name="intra",
    )(mix, mix, conv0, shift0, mu, cw, lora_w, wd0, a0, kk_w, ka_w, rk_w)


def _post_kernel(n_x, *refs):
    x_refs, refs = refs[:n_x], refs[n_x:]
    (bgy_ref, rp_ref, y0_ref, g_ref, bg_ref, gate_ref, mt_ref, n0_ref, s0_ref, lnw_ref, lnb_ref,
     wc_ref, wr_ref, wo_ref, gmix_ref, gpre_ref, wup_ref, wdown_ref, gpost_ref,
     *out_refs, s_prompt_ref, s_sample_ref, state) = refs
    i = pl.program_id(0)
    is_stream = i >= N_PROMPT_TILES

    @pl.when(i == 0)
    def _():
        state[...] = jnp.zeros_like(state)

    eye = (_iota((PAIR, PAIR), 0) == _iota((PAIR, PAIR), 1)).astype(_BF16)
    chunk_rows = lambda c: slice(c * CHUNK, (c + 1) * CHUNK)
    pair_lanes = lambda p: slice(p * PAIR, (p + 1) * PAIR)

    def group_norm(ys):
        y = jnp.concatenate(ys, axis=1)
        inv_n = 1.0 / HEAD_SIZE
        d = y - _head_sums(y) * inv_n
        var = _head_sums(d * d) * inv_n
        return d * lax.rsqrt(var + LN_X_EPS) * lnw_ref[...] + lnb_ref[...]

    def heads_of(s):
        return s[0:HEAD_SIZE, 0:HEAD_SIZE], s[HEAD_SIZE:PAIR, HEAD_SIZE:PAIR]

    def prompt_steps():
        ws = [state[p] for p in range(N_PAIRS)]
        yns = []
        for c in range(TILE_CHUNKS):
            ys = []
            for p in range(N_PAIRS):
                w_pieces = _split2(ws[p])
                ys.append(_dot_pieces(_split2(rp_ref[chunk_rows(c), pair_lanes(p)]), w_pieces)
                          + y0_ref[chunk_rows(c), pair_lanes(p)])
                ws[p] = (_dot_pieces(_split2(_by_head_rows(mt_ref[c, p])), w_pieces)
                         + _by_head_rows(n0_ref[c, p]))
            yield
            yns.append(group_norm(ys))
            yield
        for p in range(N_PAIRS):
            state[p] = ws[p]
            s_prompt_ref[2 * p], s_prompt_ref[2 * p + 1] = heads_of(_dot_pieces((eye,), _split3(ws[p]), _NT))
        return jnp.concatenate(yns, axis=0)

    def stream_steps():
        zeros = jnp.zeros((HEAD_SIZE, HEAD_SIZE), _F32)
        yns = []
        for c in range(TILE_CHUNKS):
            ys = []
            for p in range(N_PAIRS):
                s0 = jnp.concatenate([jnp.concatenate([s0_ref[c, 2 * p], zeros], axis=1),
                                      jnp.concatenate([zeros, s0_ref[c, 2 * p + 1]], axis=1)], axis=0)
                s0_hi, s0_lo = _split2(s0)
                ys.append(_dot_pieces(_split2(rp_ref[chunk_rows(c), pair_lanes(p)]), (s0_hi, s0_lo), _NT)
                          + y0_ref[chunk_rows(c), pair_lanes(p)])
                mn = jnp.concatenate([_by_head_rows(mt_ref[c, p]), _by_head_rows(n0_ref[c, p])], axis=1)
                s_next = _dot_pieces((jnp.concatenate([s0_hi, eye], axis=1),
                                      jnp.concatenate([s0_lo, jnp.zeros_like(eye)], axis=1)),
                                     _split2(mn), _NT)
                s_sample_ref[c, 2 * p], s_sample_ref[c, 2 * p + 1] = heads_of(s_next)
            yield
            yns.append(group_norm(ys))
            yield
        return jnp.concatenate(yns, axis=0)

    def gate_steps():
        branch_a = _dot(bgy_ref[...].astype(_BF16), wc_ref[...])
        yield
        gates = []
        for c in range(TILE_CHUNKS):
            gates.append((jax.nn.sigmoid(gate_ref[chunk_rows(c), 0:D_MODEL]),
                          jax.nn.sigmoid(gate_ref[chunk_rows(c), D_MODEL:GATE_COLS])))
            yield
        gate_a = jnp.concatenate([ga for ga, _ in gates], axis=0)
        gate_b = jnp.concatenate([gb for _, gb in gates], axis=0)
        return branch_a, gate_a, gate_b

    def tile_output(recurrence_steps):
        yn, (branch_a, gate_a, gate_b) = _zip_steps(recurrence_steps, gate_steps())
        z = yn * g_ref[...] + bg_ref[...]
        branch_b = _dot(z.astype(_BF16), wr_ref[...])
        merged = gate_a * branch_a + gate_b * branch_b
        m = _dot(merged.astype(_BF16), wo_ref[...])
        x = _load_tokens(x_refs) + _rms_scale(m) * gmix_ref[...]
        h = (_rms_scale(x) * gpre_ref[...]).astype(_BF16)
        up = _dot(h, wup_ref[...])
        act = jax.nn.silu(up[:, :D_FF]) * up[:, D_FF:]
        f = _dot(act.astype(_BF16), wdown_ref[...])
        return x + _rms_scale(f) * gpost_ref[...]

    @pl.when(jnp.logical_not(is_stream))
    def _():
        out = tile_output(prompt_steps())
        if len(out_refs) == 1:
            out_refs[0][...] = out
        else:
            out_refs[0][0] = out

    @pl.when(is_stream)
    def _():
        out = tile_output(stream_steps())
        if len(out_refs) == 1:
            out_refs[0][...] = out
        else:
            out_refs[1][...] = out.reshape(STREAMS_PER_TILE, DEC_SEQ, D_MODEL)


def _post(x_parts, layer, token_inputs, chunk_inputs, state_wkv, params, split_out):
    tok = lambda a: pl.BlockSpec((TOKEN_TILE, a.shape[1]), lambda i: (i, 0))
    blk = pl.BlockSpec((TILE_CHUNKS, N_PAIRS, HEAD_SIZE, PAIR), lambda i: (i, 0, 0, 0))
    heads = (N_HEADS, HEAD_SIZE, HEAD_SIZE)
    stream_block = lambda i: jnp.maximum(i - N_PROMPT_TILES, 0)
    st_in = pl.BlockSpec((None, STREAMS_PER_TILE) + heads, lambda i: (layer, stream_block(i), 0, 0, 0))
    st_out = pl.BlockSpec((STREAMS_PER_TILE,) + heads, lambda i: (stream_block(i), 0, 0, 0))
    if split_out:
        out_specs = _token_specs(2)
        out_shape = [jax.ShapeDtypeStruct((1, SEQ, D_MODEL), _F32),
                     jax.ShapeDtypeStruct((DEC_BATCH, DEC_SEQ, D_MODEL), _F32)]
    else:
        out_specs = _token_specs(1)
        out_shape = [jax.ShapeDtypeStruct((N_TOK, D_MODEL), _F32)]
    *x_out, s_prompt, s_sample = pl.pallas_call(
        functools.partial(_post_kernel, len(x_parts)),
        grid=(N_TOK // TOKEN_TILE,),
        in_specs=(_token_specs(len(x_parts)) + [tok(a) for a in token_inputs] + [blk] * len(chunk_inputs)
                  + [st_in] + [_layer_spec(a, layer) for a in params]),
        out_specs=out_specs + [pl.BlockSpec(heads, lambda i: (0, 0, 0)), st_out],
        out_shape=out_shape + [jax.ShapeDtypeStruct(heads, _F32),
                               jax.ShapeDtypeStruct((DEC_BATCH,) + heads, _F32)],
        scratch_shapes=[pltpu.VMEM((N_PAIRS, PAIR, PAIR), _F32)],
        compiler_params=pltpu.CompilerParams(
            dimension_semantics=("arbitrary",), vmem_limit_bytes=VMEM_LIMIT_BYTES),
        name="post",
    )(*x_parts, *token_inputs, *chunk_inputs, state_wkv, *params)
    return x_out, s_prompt, s_sample


def _lora_weight(w_decay2, a2, g2):
    w = jnp.zeros((DEPTH, D_LORA, 3 * D_RWKV), _BF16)
    w = w.at[:, 0:D_DECAY_LORA, 0:D_RWKV].set(w_decay2.astype(_BF16))
    w = w.at[:, D_DECAY_LORA:D_DECAY_LORA + D_AAA_LORA, D_RWKV:2 * D_RWKV].set(a2.astype(_BF16))
    return w.at[:, D_DECAY_LORA + D_AAA_LORA:, 2 * D_RWKV:].set(g2.astype(_BF16))


def kernel(x_prompt, x_sample, state_conv, state_shift, state_wkv, norm_mix_pre, norm_mix_post, w_in, mu_shift, conv_w, w_decay0, w_decay2, a0, a2, g2, k_k, k_a, r_k, ln_x_w, ln_x_b, w_conv_out, w_rwkv_out, w_o, norm_ffn_pre, norm_ffn_post, w_ffn_up, w_ffn_down):
    x_parts = (x_prompt, x_sample)
    rows = lambda a: a.reshape(DEPTH, 1, -1)
    bf16 = lambda a: a.astype(_BF16)
    lora_w = _lora_weight(w_decay2, a2, g2)
    w_in_b = bf16(w_in)
    intra_params = (rows(mu_shift), conv_w, lora_w, rows(w_decay0), rows(a0), rows(k_k), rows(k_a), rows(r_k))
    post_params = (rows(ln_x_w), rows(ln_x_b), bf16(w_conv_out), bf16(w_rwkv_out), bf16(w_o),
                   rows(norm_mix_post), rows(norm_ffn_pre), bf16(w_ffn_up), bf16(w_ffn_down), rows(norm_ffn_post))
    norm_pre = rows(norm_mix_pre)
    shift0 = state_shift[:, :, None]
    conv_p, shift_p, wkv_p, conv_s, shift_s, wkv_s = [], [], [], [], [], []
    for l in range(DEPTH):
        mix, gates = _proj(x_parts, l, norm_pre, w_in_b)
        (bgy, rp, y0, g, bg, mt, n0, conv_out, shift_out) = _intra(mix, l, state_conv, shift0, *intra_params)
        x_parts, s_prompt, s_sample = _post(x_parts, l, (bgy, rp, y0, g, bg, gates), (mt, n0), state_wkv,
                                            post_params, split_out=(l == DEPTH - 1))

        last = N_PROMPT_CHUNKS - 1
        conv_p.append(conv_out[last:last + 1])
        conv_s.append(conv_out[N_PROMPT_CHUNKS:])
        shift_p.append(shift_out[last:last + 1, 0])
        shift_s.append(shift_out[N_PROMPT_CHUNKS:, 0])
        wkv_p.append(s_prompt[None])
        wkv_s.append(s_sample)
    y_prompt, y_sample = x_parts
    return (y_prompt, y_sample,
            jnp.stack(conv_p, 0), jnp.stack(shift_p, 0), jnp.stack(wkv_p, 0),
            jnp.stack(conv_s, 0), jnp.stack(shift_s, 0), jnp.stack(wkv_s, 0))
```

```python
import functools

import jax
import jax.numpy as jnp
from jax import lax
from jax.experimental import pallas as pl
from jax.experimental.pallas import tpu as pltpu

D_MODEL = 1024
SEQ = 16384
DEPTH = 2
DEC_BATCH = 32
DEC_SEQ = 64
D_CONV = 512
D_RWKV = 512
HEAD_SIZE = 64
N_HEADS = 8
D_DECAY_LORA = 64
D_AAA_LORA = 64
D_GATE_LORA = 128
D_LORA = D_DECAY_LORA + D_AAA_LORA + D_GATE_LORA
LN_X_EPS = 64e-5
D_FF = 2816
RMS_EPS = 1e-6
RWKV_COLS = 3 * D_RWKV + D_LORA
MIX_COLS = 3 * D_CONV + RWKV_COLS
GATE_COLS = 2 * D_MODEL

CHUNK = 64
N_TOK = SEQ + DEC_BATCH * DEC_SEQ
N_CHUNKS = N_TOK // CHUNK
N_PROMPT_CHUNKS = SEQ // CHUNK
PAIR = 2 * HEAD_SIZE
N_PAIRS = N_HEADS // 2
TOKEN_TILE = 256
N_PROMPT_TILES = SEQ // TOKEN_TILE
STREAMS_PER_TILE = TOKEN_TILE // DEC_SEQ
TILE_CHUNKS = TOKEN_TILE // CHUNK
INTRA_CHUNKS = 4
INTRA_GROUP = 4
FRONT_TILE = INTRA_CHUNKS * CHUNK
PROJ_SLAB = 256
SUBLANES = 8
PIECES_A = 1
PIECES_INV = 1
PIECES_OUT = 1
VMEM_LIMIT_BYTES = 56 * 1024 * 1024

_BF16 = jnp.bfloat16
_F32 = jnp.float32
_NN = (((1,), (0,)), ((), ()))
_NT = (((1,), (1,)), ((), ()))
_TN = (((0,), (0,)), ((), ()))


def _dot(a, b, dims=_NN):
    return lax.dot_general(a, b, dims, preferred_element_type=_F32)


def _pieces(x, n):
    out = []
    for _ in range(n - 1):
        hi = x.astype(_BF16)
        out.append(hi)
        x = x - hi.astype(_F32)
    out.append(x.astype(_BF16))
    return tuple(out)


def _split2(x):
    return _pieces(x, 2)


def _split3(x):
    return _pieces(x, 3)


def _dot_pieces(a, b, dims=_NN):
    n = max(len(a), len(b))
    terms = [_dot(a[i], b[j], dims) for i in range(len(a)) for j in range(len(b)) if i + j < n]
    total = terms[0]
    if len(terms) > 1:
        rest = terms[1]
        for t in terms[2:]:
            rest = rest + t
        total = total + rest
    return total


def _dot_f32(a, b, dims=_NN):
    return _dot_pieces(_split2(a), _split2(b), dims)


class _Steps:
    def __init__(self, gen):
        self.gen, self.done, self.value = gen, False, None

    def step(self):
        if not self.done:
            try:
                next(self.gen)
            except StopIteration as stop:
                self.done, self.value = True, stop.value

    def finish(self):
        while not self.done:
            self.step()
        return self.value


def _zip_steps(main, *sides):
    count = 0
    while not main.done:
        main.step()
        count += 1
        for side, n, every in sides:
            if count % every == 0:
                for _ in range(n):
                    side.step()
    return main.value


def _rms_scale(x):
    return x * lax.rsqrt(jnp.mean(x * x, axis=-1, keepdims=True) + RMS_EPS)


def _iota(shape, dim):
    return lax.broadcasted_iota(jnp.int32, shape, dim)


def _first_head(width=PAIR):
    return _iota((1, width), 1) % PAIR < HEAD_SIZE


def _head_sums(x):
    first = _first_head()
    outs = []
    for p in range(x.shape[1] // PAIR):
        xp = x[:, p * PAIR:(p + 1) * PAIR]
        lo = jnp.sum(jnp.where(first, xp, 0.0), axis=1, keepdims=True)
        hi = jnp.sum(jnp.where(first, 0.0, xp), axis=1, keepdims=True)
        outs.append(jnp.where(first, lo, hi))
    return jnp.concatenate(outs, axis=1)


def _by_head_rows(x):
    first = _first_head()
    return jnp.concatenate([jnp.where(first, x, 0.0), jnp.where(first, 0.0, x)], axis=0)


def _diag_blocks(x):
    return jnp.where(_first_head(), x[0:HEAD_SIZE], x[HEAD_SIZE:PAIR])


def _layer_spec(a, layer):
    return pl.BlockSpec((None,) + a.shape[1:], lambda i: (layer,) + (0,) * (a.ndim - 1),
                        pipeline_mode=pl.Buffered(1))


def _token_specs(n_parts, tile=TOKEN_TILE, tile_of_step=lambda i: i):
    if n_parts == 1:
        return [pl.BlockSpec((tile, D_MODEL), lambda i: (tile_of_step(i), 0))]
    n_prompt = SEQ // tile
    return [pl.BlockSpec((1, tile, D_MODEL), lambda i: (0, jnp.minimum(tile_of_step(i), n_prompt - 1), 0)),
            pl.BlockSpec((tile // DEC_SEQ, DEC_SEQ, D_MODEL),
                         lambda i: (jnp.maximum(tile_of_step(i) - n_prompt, 0), 0, 0))]


def _load_tokens(x_refs, tile=TOKEN_TILE, tile_index=None):
    if len(x_refs) == 1:
        return x_refs[0][...]
    prompt_ref, stream_ref = x_refs
    tile_index = pl.program_id(0) if tile_index is None else tile_index
    return jnp.where(tile_index >= SEQ // tile, stream_ref[...].reshape(tile, D_MODEL), prompt_ref[0])


def _running_log_decay(lw):
    tri = jnp.where(_iota((CHUNK, CHUNK), 1) <= _iota((CHUNK, CHUNK), 0), 1.0, 0.0).astype(_BF16)
    return _dot_pieces((tri,), _split3(lw))


def _decayed_operands(r, kf, v, lw, cum, kk, a_sig):
    last = cum[CHUNK - 1:CHUNK]
    e_inv = jnp.exp(-cum)
    e_tail = jnp.exp(last - cum)
    b = kk * a_sig
    return dict(a_t=-kk * jnp.exp(cum - lw), r_t=r * jnp.exp(cum), b_t=b * e_inv, k_t=kf * e_inv,
                b_h=b * e_tail, k_h=kf * e_tail, v=v, dec=jnp.exp(last))


def _intra_pairs(chunks):
    first = _first_head()
    first2 = _first_head(2 * PAIR)
    g_row = _iota((4 * CHUNK, PAIR), 0)
    keep = (_iota((4 * CHUNK, PAIR), 1) % CHUNK) < (g_row % CHUNK) + (g_row // CHUNK) % 2
    rc_xor = _iota((PAIR, PAIR), 0) ^ _iota((PAIR, PAIR), 1)
    eye = rc_xor == 0
    zeros = jnp.zeros((CHUNK, PAIR), _F32)
    items = [(c, p) for c in range(len(chunks)) for p in range(N_PAIRS)]
    part = lambda it, name: chunks[it[0]][name][:, it[1] * PAIR:(it[1] + 1) * PAIR]
    out_pieces = lambda x: _pieces(x, PIECES_OUT)

    gm = {}
    for it in items:
        ar = jnp.concatenate([part(it, "a_t"), part(it, "r_t")], axis=0)
        bk = jnp.concatenate([part(it, "b_t"), part(it, "k_t")], axis=0)
        ar_split = jnp.concatenate([jnp.where(first, ar, 0.0), jnp.where(first, 0.0, ar)], axis=0)
        gm[it] = jnp.where(keep, _dot_pieces(_pieces(ar_split, PIECES_A), _pieces(bk, PIECES_A), _NT), 0.0)
    yield

    l_bd, inv = {}, {}
    for it in items:
        l0, l1 = gm[it][0:CHUNK], gm[it][2 * CHUNK:3 * CHUNK]
        l_bd[it] = jnp.concatenate([jnp.where(first, l0, 0.0),
                                    jnp.where(first, 0.0, pltpu.roll(l1, HEAD_SIZE, axis=1))], axis=0)
        inv[it] = jnp.where(eye, 1.0, jnp.where(rc_xor == 1, l_bd[it], 0.0))
    yield
    m = 2
    while m < CHUNK:
        level = jnp.logical_and(rc_xor >= m, rc_xor < 2 * m)
        slabs = m >= SUBLANES
        low = (lambda a: jnp.concatenate([a[s:s + m] for s in range(m, PAIR, 2 * m)], axis=0)) if slabs else (lambda a: a)
        inv_pieces = {it: _pieces(inv[it], PIECES_INV) for it in items}
        x = {it: _dot_pieces(_pieces(low(jnp.where(level, l_bd[it], 0.0)), PIECES_INV), inv_pieces[it])
             for it in items}
        yield
        for it in items:
            if slabs:
                zeros_m = jnp.zeros((m, PAIR), _F32)
                x_rows = [x[it][j * m:(j + 1) * m] for j in range(PAIR // (2 * m))]
                x_full = jnp.concatenate([a for xr in x_rows for a in (zeros_m, xr)], axis=0)
                upd = _dot_pieces(_pieces(low(inv[it]), PIECES_INV), _pieces(x_full, PIECES_INV))
                inv[it] = jnp.concatenate(
                    [a for j in range(PAIR // (2 * m))
                     for a in (inv[it][2 * j * m:(2 * j + 1) * m],
                               inv[it][(2 * j + 1) * m:(2 * j + 2) * m] + upd[j * m:(j + 1) * m])], axis=0)
            else:
                inv[it] = inv[it] + _dot_pieces(inv_pieces[it], _pieces(x[it], PIECES_INV))
        yield
        m *= 2

    akv = {}
    for it in items:
        v_p = part(it, "v")
        vv_pieces = out_pieces(jnp.concatenate([v_p, v_p], axis=0))
        ak = jnp.where(first, 0.0, jnp.concatenate([gm[it][0:CHUNK], gm[it][2 * CHUNK:3 * CHUNK]], axis=0))
        akv2 = _dot_pieces(out_pieces(ak), vv_pieces)
        akv[it] = jnp.where(first, akv2[0:CHUNK], akv2[CHUNK:2 * CHUNK])
    yield
    uva_pieces = {}
    for it in items:
        tx = _dot_pieces(out_pieces(inv[it]), out_pieces(jnp.concatenate(
            [_by_head_rows(akv[it]), _by_head_rows(part(it, "a_t"))], axis=1)))
        u0a = tx[0:CHUNK] + tx[CHUNK:2 * CHUNK]
        uva = jnp.concatenate([u0a, jnp.concatenate([part(it, "v"), zeros], axis=1)], axis=0)
        uva_pieces[it] = out_pieces(uva)
    yield
    rps, y0s, mts, n0s = {}, {}, {}, {}
    for it in items:
        rr = jnp.concatenate([gm[it][CHUNK:2 * CHUNK], gm[it][3 * CHUNK:4 * CHUNK]], axis=0)
        yr2 = _dot_pieces(out_pieces(rr), uva_pieces[it])
        yr = jnp.where(first2, yr2[0:CHUNK], yr2[CHUNK:2 * CHUNK])
        y0s[it] = yr[:, 0:PAIR]
        rps[it] = part(it, "r_t") + yr[:, PAIR:2 * PAIR]
    yield
    for it in items:
        bk_h = jnp.concatenate([part(it, "b_h"), part(it, "k_h")], axis=0)
        nm = _dot_pieces(out_pieces(bk_h), uva_pieces[it], _TN)
        n0s[it] = _diag_blocks(nm[:, 0:PAIR])
        mts[it] = _diag_blocks(nm[:, PAIR:2 * PAIR] + jnp.where(eye, part(it, "dec"), 0.0))
    cat = lambda d, c: jnp.concatenate([d[(c, p)] for p in range(N_PAIRS)], axis=1)
    return ([cat(rps, c) for c in range(len(chunks))], [cat(y0s, c) for c in range(len(chunks))], mts, n0s)


def _front_kernel(n_x, *refs):
    x_refs, refs = refs[:n_x], refs[n_x:]
    (gpre_ref, w_ref, conv0_ref, shift0_ref, mu_ref, cw_ref, lora_w_ref,
     wd0_ref, a0_ref, kk_w_ref, ka_w_ref, rk_w_ref,
     gate_ref, bgy_ref, rp_ref, y0_ref, g_ref, bg_ref, mt_ref, n0_ref, conv_out_ref, shift_out_ref,
     mix_next, mix_cur, hist_u_scr, hist_pr_scr) = refs
    s = pl.program_id(0)
    is_sample = s - 1 >= N_PROMPT_CHUNKS // INTRA_CHUNKS
    row = _iota((CHUNK, 1), 0)
    cw = cw_ref[...]
    lane = _iota((CHUNK, D_LORA), 1)

    @pl.when(s == 0)
    def _():
        mix_cur[...] = jnp.zeros_like(mix_cur)
        hist_u_scr[...] = jnp.zeros_like(hist_u_scr)
        hist_pr_scr[...] = jnp.zeros_like(hist_pr_scr)

    def project():
        x = _load_tokens(x_refs, FRONT_TILE, jnp.minimum(s, N_TOK // FRONT_TILE - 1))
        h = (_rms_scale(x) * gpre_ref[...]).astype(_BF16)
        yield
        for j in range((MIX_COLS + GATE_COLS) // PROJ_SLAB):
            cols = slice(j * PROJ_SLAB, (j + 1) * PROJ_SLAB)
            p = _dot(h, w_ref[:, cols])
            if j < MIX_COLS // PROJ_SLAB:
                mix_next[:, cols] = p
            else:
                gate_ref[:, j * PROJ_SLAB - MIX_COLS:(j + 1) * PROJ_SLAB - MIX_COLS] = p
            yield

    hist = dict(u=hist_u_scr[...], pr=hist_pr_scr[...])

    def prepare(c):
        rows = slice(c * CHUNK, (c + 1) * CHUNK)
        u_hist = jnp.where(is_sample, conv0_ref[c], hist["u"])
        pr_hist = jnp.where(is_sample, shift0_ref[c], hist["pr"])

        xin = mix_cur[rows, 0:D_CONV]
        bg = mix_cur[rows, D_CONV:2 * D_CONV]
        cg = mix_cur[rows, 2 * D_CONV:3 * D_CONV]
        u = cg * xin
        u1 = jnp.where(row == 0, u_hist[1:2], pltpu.roll(u, 1, axis=0))
        u2 = jnp.where(row == 0, u_hist[0:1], jnp.where(row == 1, u_hist[1:2], pltpu.roll(u, 2, axis=0)))
        bgy_ref[rows, :] = bg * (cw[0:1] * u2 + cw[1:2] * u1 + cw[2:3] * u)
        hist["u"] = u[CHUNK - 2:CHUNK]
        conv_out_ref[c] = hist["u"]
        yield

        pr = mix_cur[rows, 3 * D_CONV:MIX_COLS]
        shifted = jnp.where(row == 0, pr_hist, pltpu.roll(pr, 1, axis=0))
        ps = pr + (shifted - pr) * mu_ref[...]
        hist["pr"] = pr[CHUNK - 1:CHUNK]
        shift_out_ref[c] = hist["pr"]
        r = ps[:, 0:D_RWKV]
        k = ps[:, D_RWKV:2 * D_RWKV]
        v = ps[:, 2 * D_RWKV:3 * D_RWKV]
        yield

        lora_in = ps[:, 3 * D_RWKV:RWKV_COLS]
        act = jnp.where(lane < D_DECAY_LORA, jnp.tanh(lora_in),
                        jnp.where(lane < D_DECAY_LORA + D_AAA_LORA, lora_in, jax.nn.sigmoid(lora_in)))
        lora = _dot(act.astype(_BF16), lora_w_ref[...])
        yield

        w_raw = wd0_ref[...] + lora[:, 0:D_RWKV]
        w_raw = -jax.nn.softplus(-w_raw) - 0.5
        lw = -jnp.exp(w_raw)
        cum = _running_log_decay(lw)
        yield

        a_sig = jax.nn.sigmoid(a0_ref[...] + lora[:, D_RWKV:2 * D_RWKV])
        g = lora[:, 2 * D_RWKV:3 * D_RWKV]
        g_ref[rows, :] = g
        kk = k * kk_w_ref[...]
        yield
        kk = kk / jnp.maximum(jnp.sqrt(_head_sums(kk * kk)), 1e-12)
        yield
        kf = k * (1.0 + (a_sig - 1.0) * ka_w_ref[...])
        bg_ref[rows, :] = _head_sums(r * kf * rk_w_ref[...]) * v * g
        yield
        return _decayed_operands(r, kf, v, lw, cum, kk, a_sig)

    def prepare_group(first_chunk):
        ops = []
        for c in range(first_chunk, first_chunk + INTRA_GROUP):
            ops.append((yield from prepare(c)))
        return ops

    def store(first_chunk, results):
        rps, y0s, mts, n0s = results
        for g in range(INTRA_GROUP):
            c = first_chunk + g
            rows = slice(c * CHUNK, (c + 1) * CHUNK)
            rp_ref[rows, :] = rps[g]
            y0_ref[rows, :] = y0s[g]
            for p in range(N_PAIRS):
                mt_ref[c, p] = mts[(g, p)]
                n0_ref[c, p] = n0s[(g, p)]

    def hand_over():
        for j in range(MIX_COLS // PROJ_SLAB):
            cols = slice(j * PROJ_SLAB, (j + 1) * PROJ_SLAB)
            mix_cur[:, cols] = mix_next[:, cols]
            yield

    assert INTRA_CHUNKS == INTRA_GROUP
    projection = _Steps(project())
    operands = _zip_steps(_Steps(prepare_group(0)), (projection, 1, 1))
    projection.finish()
    hist_u_scr[...] = hist["u"]
    hist_pr_scr[...] = hist["pr"]
    copy = _Steps(hand_over())
    results = _zip_steps(_Steps(_intra_pairs(operands)), (copy, 1, 1))
    copy.finish()
    store(0, results)


def _front(x_parts, layer, norm_pre, w_in_bf16, conv0, shift0, mu, cw, lora_w, wd0, a0, kk_w, ka_w, rk_w):
    n_tiles = N_TOK // FRONT_TILE
    n_prompt_tiles = SEQ // FRONT_TILE
    projected = lambda s: jnp.minimum(s, n_tiles - 1)
    prepared = lambda s: jnp.maximum(s - 1, 0)
    tok = lambda width: pl.BlockSpec((FRONT_TILE, width), lambda s: (prepared(s), 0))
    full = lambda a: _layer_spec(a, layer)
    seq = lambda a: pl.BlockSpec((None, INTRA_CHUNKS) + a.shape[2:],
                                 lambda s: (layer, jnp.maximum(prepared(s) - n_prompt_tiles, 0), 0, 0))
    per_chunk = lambda *dims: pl.BlockSpec((INTRA_CHUNKS,) + dims, lambda s: (prepared(s),) + (0,) * len(dims))
    tok_out = jax.ShapeDtypeStruct((N_TOK, D_RWKV), _F32)
    blocks = jax.ShapeDtypeStruct((N_CHUNKS, N_PAIRS, HEAD_SIZE, PAIR), _F32)
    return pl.pallas_call(
        functools.partial(_front_kernel, len(x_parts)),
        grid=(n_tiles + 1,),
        in_specs=(_token_specs(len(x_parts), FRONT_TILE, projected)
                  + [full(norm_pre), full(w_in_bf16), seq(conv0), seq(shift0), full(mu), full(cw), full(lora_w),
                     full(wd0), full(a0), full(kk_w), full(ka_w), full(rk_w)]),
        out_specs=[pl.BlockSpec((FRONT_TILE, GATE_COLS), lambda s: (projected(s), 0)), tok(D_CONV)]
        + [tok(D_RWKV)] * 4 + [
            per_chunk(N_PAIRS, HEAD_SIZE, PAIR), per_chunk(N_PAIRS, HEAD_SIZE, PAIR),
            per_chunk(2, D_CONV), per_chunk(1, RWKV_COLS)],
        out_shape=[jax.ShapeDtypeStruct((N_TOK, GATE_COLS), _F32), jax.ShapeDtypeStruct((N_TOK, D_CONV), _F32)]
        + [tok_out] * 4 + [
            blocks, blocks,
            jax.ShapeDtypeStruct((N_CHUNKS, 2, D_CONV), _F32),
            jax.ShapeDtypeStruct((N_CHUNKS, 1, RWKV_COLS), _F32)],
        scratch_shapes=[pltpu.VMEM((FRONT_TILE, MIX_COLS), _F32), pltpu.VMEM((FRONT_TILE, MIX_COLS), _F32),
                        pltpu.VMEM((2, D_CONV), _F32), pltpu.VMEM((1, RWKV_COLS), _F32)],
        compiler_params=pltpu.CompilerParams(
            dimension_semantics=("arbitrary",), vmem_limit_bytes=VMEM_LIMIT_BYTES),
        name="front",
    )(*x_parts, norm_pre, w_in_bf16, conv0, shift0, mu, cw, lora_w, wd0, a0, kk_w, ka_w, rk_w)


def _post_kernel(n_x, *refs):
    x_refs, refs = refs[:n_x], refs[n_x:]
    (bgy_ref, rp_ref, y0_ref, g_ref, bg_ref, gate_ref, mt_ref, n0_ref, s0_ref, lnw_ref, lnb_ref,
     wc_ref, wr_ref, wo_ref, gmix_ref, gpre_ref, wup_ref, wdown_ref, gpost_ref,
     *out_refs, s_prompt_ref, s_sample_ref, state) = refs
    i = pl.program_id(0)
    is_stream = i >= N_PROMPT_TILES

    @pl.when(i == 0)
    def _():
        state[...] = jnp.zeros_like(state)

    eye = (_iota((PAIR, PAIR), 0) == _iota((PAIR, PAIR), 1)).astype(_BF16)
    chunk_rows = lambda c: slice(c * CHUNK, (c + 1) * CHUNK)
    pair_lanes = lambda p: slice(p * PAIR, (p + 1) * PAIR)

    def group_norm(ys):
        y = jnp.concatenate(ys, axis=1)
        inv_n = 1.0 / HEAD_SIZE
        d = y - _head_sums(y) * inv_n
        var = _head_sums(d * d) * inv_n
        return d * lax.rsqrt(var + LN_X_EPS) * lnw_ref[...] + lnb_ref[...]

    def heads_of(s):
        return s[0:HEAD_SIZE, 0:HEAD_SIZE], s[HEAD_SIZE:PAIR, HEAD_SIZE:PAIR]

    def prompt_steps():
        ws = [state[p] for p in range(N_PAIRS)]
        yns = []
        for c in range(TILE_CHUNKS):
            ys = []
            for p in range(N_PAIRS):
                w_pieces = _split2(ws[p])
                ys.append(_dot(rp_ref[chunk_rows(c), pair_lanes(p)].astype(_BF16), w_pieces[0])
                          + y0_ref[chunk_rows(c), pair_lanes(p)])
                ws[p] = (_dot_pieces(_split2(_by_head_rows(mt_ref[c, p])), w_pieces)
                         + _by_head_rows(n0_ref[c, p]))
            yield
            yns.append(group_norm(ys))
            yield
        for p in range(N_PAIRS):
            state[p] = ws[p]
            s_prompt_ref[2 * p], s_prompt_ref[2 * p + 1] = heads_of(_dot_pieces((eye,), _split3(ws[p]), _NT))
        return jnp.concatenate(yns, axis=0)

    def stream_steps():
        zeros = jnp.zeros((HEAD_SIZE, HEAD_SIZE), _F32)
        yns = []
        for c in range(TILE_CHUNKS):
            ys = []
            for p in range(N_PAIRS):
                s0 = jnp.concatenate([jnp.concatenate([s0_ref[c, 2 * p], zeros], axis=1),
                                      jnp.concatenate([zeros, s0_ref[c, 2 * p + 1]], axis=1)], axis=0)
                s0_hi, s0_lo = _split2(s0)
                ys.append(_dot(rp_ref[chunk_rows(c), pair_lanes(p)].astype(_BF16), s0_hi, _NT)
                          + y0_ref[chunk_rows(c), pair_lanes(p)])
                mn = jnp.concatenate([_by_head_rows(mt_ref[c, p]), _by_head_rows(n0_ref[c, p])], axis=1)
                s_next = _dot_pieces((jnp.concatenate([s0_hi, eye], axis=1),
                                      jnp.concatenate([s0_lo, jnp.zeros_like(eye)], axis=1)),
                                     _split2(mn), _NT)
                s_sample_ref[c, 2 * p], s_sample_ref[c, 2 * p + 1] = heads_of(s_next)
            yield
            yns.append(group_norm(ys))
            yield
        return jnp.concatenate(yns, axis=0)

    def gate_steps():
        branch_a = _dot(bgy_ref[...].astype(_BF16), wc_ref[...])
        yield
        gates = []
        for c in range(TILE_CHUNKS):
            gates.append((jax.nn.sigmoid(gate_ref[chunk_rows(c), 0:D_MODEL]),
                          jax.nn.sigmoid(gate_ref[chunk_rows(c), D_MODEL:GATE_COLS])))
            yield
        gate_a = jnp.concatenate([ga for ga, _ in gates], axis=0)
        gate_b = jnp.concatenate([gb for _, gb in gates], axis=0)
        return branch_a, gate_a, gate_b

    def tile_output(recurrence_steps):
        gate_work = _Steps(gate_steps())
        yn = _zip_steps(_Steps(recurrence_steps), (gate_work, 1, 1))
        branch_a, gate_a, gate_b = gate_work.finish()
        z = yn * g_ref[...] + bg_ref[...]
        branch_b = _dot(z.astype(_BF16), wr_ref[...])
        merged = gate_a * branch_a + gate_b * branch_b
        m = _dot(merged.astype(_BF16), wo_ref[...])
        x = _load_tokens(x_refs) + _rms_scale(m) * gmix_ref[...]
        h = (_rms_scale(x) * gpre_ref[...]).astype(_BF16)
        up = _dot(h, wup_ref[...])
        act = jax.nn.silu(up[:, :D_FF]) * up[:, D_FF:]
        f = _dot(act.astype(_BF16), wdown_ref[...])
        return x + _rms_scale(f) * gpost_ref[...]

    @pl.when(jnp.logical_not(is_stream))
    def _():
        out = tile_output(prompt_steps())
        if len(out_refs) == 1:
            out_refs[0][...] = out
        else:
            out_refs[0][0] = out

    @pl.when(is_stream)
    def _():
        out = tile_output(stream_steps())
        if len(out_refs) == 1:
            out_refs[0][...] = out
        else:
            out_refs[1][...] = out.reshape(STREAMS_PER_TILE, DEC_SEQ, D_MODEL)


def _post(x_parts, layer, token_inputs, chunk_inputs, state_wkv, params, split_out):
    tok = lambda a: pl.BlockSpec((TOKEN_TILE, a.shape[1]), lambda i: (i, 0))
    blk = pl.BlockSpec((TILE_CHUNKS, N_PAIRS, HEAD_SIZE, PAIR), lambda i: (i, 0, 0, 0))
    heads = (N_HEADS, HEAD_SIZE, HEAD_SIZE)
    stream_block = lambda i: jnp.maximum(i - N_PROMPT_TILES, 0)
    st_in = pl.BlockSpec((None, STREAMS_PER_TILE) + heads, lambda i: (layer, stream_block(i), 0, 0, 0))
    st_out = pl.BlockSpec((STREAMS_PER_TILE,) + heads, lambda i: (stream_block(i), 0, 0, 0))
    if split_out:
        out_specs = _token_specs(2)
        out_shape = [jax.ShapeDtypeStruct((1, SEQ, D_MODEL), _F32),
                     jax.ShapeDtypeStruct((DEC_BATCH, DEC_SEQ, D_MODEL), _F32)]
    else:
        out_specs = _token_specs(1)
        out_shape = [jax.ShapeDtypeStruct((N_TOK, D_MODEL), _F32)]
    *x_out, s_prompt, s_sample = pl.pallas_call(
        functools.partial(_post_kernel, len(x_parts)),
        grid=(N_TOK // TOKEN_TILE,),
        in_specs=(_token_specs(len(x_parts)) + [tok(a) for a in token_inputs] + [blk] * len(chunk_inputs)
                  + [st_in] + [_layer_spec(a, layer) for a in params]),
        out_specs=out_specs + [pl.BlockSpec(heads, lambda i: (0, 0, 0)), st_out],
        out_shape=out_shape + [jax.ShapeDtypeStruct(heads, _F32),
                               jax.ShapeDtypeStruct((DEC_BATCH,) + heads, _F32)],
        scratch_shapes=[pltpu.VMEM((N_PAIRS, PAIR, PAIR), _F32)],
        compiler_params=pltpu.CompilerParams(
            dimension_semantics=("arbitrary",), vmem_limit_bytes=VMEM_LIMIT_BYTES),
        name="post",
    )(*x_parts, *token_inputs, *chunk_inputs, state_wkv, *params)
    return x_out, s_prompt, s_sample


def _lora_weight(w_decay2, a2, g2):
    w = jnp.zeros((DEPTH, D_LORA, 3 * D_RWKV), _BF16)
    w = w.at[:, 0:D_DECAY_LORA, 0:D_RWKV].set(w_decay2.astype(_BF16))
    w = w.at[:, D_DECAY_LORA:D_DECAY_LORA + D_AAA_LORA, D_RWKV:2 * D_RWKV].set(a2.astype(_BF16))
    return w.at[:, D_DECAY_LORA + D_AAA_LORA:, 2 * D_RWKV:].set(g2.astype(_BF16))


def kernel(x_prompt, x_sample, state_conv, state_shift, state_wkv, norm_mix_pre, norm_mix_post, w_in, mu_shift, conv_w, w_decay0, w_decay2, a0, a2, g2, k_k, k_a, r_k, ln_x_w, ln_x_b, w_conv_out, w_rwkv_out, w_o, norm_ffn_pre, norm_ffn_post, w_ffn_up, w_ffn_down):
    x_parts = (x_prompt, x_sample)
    rows = lambda a: a.reshape(DEPTH, 1, -1)
    bf16 = lambda a: a.astype(_BF16)
    lora_w = _lora_weight(w_decay2, a2, g2)
    w_in_b = bf16(w_in)
    intra_params = (rows(mu_shift), conv_w, lora_w, rows(w_decay0), rows(a0), rows(k_k), rows(k_a), rows(r_k))
    post_params = (rows(ln_x_w), rows(ln_x_b), bf16(w_conv_out), bf16(w_rwkv_out), bf16(w_o),
                   rows(norm_mix_post), rows(norm_ffn_pre), bf16(w_ffn_up), bf16(w_ffn_down), rows(norm_ffn_post))
    norm_pre = rows(norm_mix_pre)
    shift0 = state_shift[:, :, None]
    conv_p, shift_p, wkv_p, conv_s, shift_s, wkv_s = [], [], [], [], [], []
    for l in range(DEPTH):
        (gates, bgy, rp, y0, g, bg, mt, n0, conv_out, shift_out) = _front(
            x_parts, l, norm_pre, w_in_b, state_conv, shift0, *intra_params)
        x_parts, s_prompt, s_sample = _post(x_parts, l, (bgy, rp, y0, g, bg, gates), (mt, n0), state_wkv,
                                            post_params, split_out=(l == DEPTH - 1))

        last = N_PROMPT_CHUNKS - 1
        conv_p.append(conv_out[last:last + 1])
        conv_s.append(conv_out[N_PROMPT_CHUNKS:])
        shift_p.append(shift_out[last:last + 1, 0])
        shift_s.append(shift_out[N_PROMPT_CHUNKS:, 0])
        wkv_p.append(s_prompt[None])
        wkv_s.append(s_sample)
    y_prompt, y_sample = x_parts
    return (y_prompt, y_sample,
            jnp.stack(conv_p, 0), jnp.stack(shift_p, 0), jnp.stack(wkv_p, 0),
            jnp.stack(conv_s, 0), jnp.stack(shift_s, 0), jnp.stack(wkv_s, 0))
```

```python
import functools

import jax
import jax.numpy as jnp
from jax import lax
from jax.experimental import pallas as pl
from jax.experimental.pallas import tpu as pltpu

D_MODEL = 1024
SEQ = 16384
DEPTH = 2
DEC_BATCH = 32
DEC_SEQ = 64
D_CONV = 512
D_RWKV = 512
HEAD_SIZE = 64
N_HEADS = 8
D_DECAY_LORA = 64
D_AAA_LORA = 64
D_GATE_LORA = 128
D_LORA = D_DECAY_LORA + D_AAA_LORA + D_GATE_LORA
LN_X_EPS = 64e-5
D_FF = 2816
RMS_EPS = 1e-6
RWKV_COLS = 3 * D_RWKV + D_LORA
MIX_COLS = 3 * D_CONV + RWKV_COLS
GATE_COLS = 2 * D_MODEL

CHUNK = 64
N_TOK = SEQ + DEC_BATCH * DEC_SEQ
N_CHUNKS = N_TOK // CHUNK
N_PROMPT_CHUNKS = SEQ // CHUNK
PAIR = 2 * HEAD_SIZE
N_PAIRS = N_HEADS // 2
TOKEN_TILE = 256
N_PROMPT_TILES = SEQ // TOKEN_TILE
STREAMS_PER_TILE = TOKEN_TILE // DEC_SEQ
TILE_CHUNKS = TOKEN_TILE // CHUNK
INTRA_CHUNKS = 4
INTRA_GROUP = 4
FRONT_TILE = INTRA_CHUNKS * CHUNK
PROJ_SLAB = 256
SUBLANES = 8
PIECES_A = 1
PIECES_INV = 1
PIECES_OUT = 1
VMEM_LIMIT_BYTES = 56 * 1024 * 1024

_BF16 = jnp.bfloat16
_F32 = jnp.float32
_NN = (((1,), (0,)), ((), ()))
_NT = (((1,), (1,)), ((), ()))
_TN = (((0,), (0,)), ((), ()))


def _dot(a, b, dims=_NN):
    return lax.dot_general(a, b, dims, preferred_element_type=_F32)


def _pieces(x, n):
    out = []
    for _ in range(n - 1):
        hi = x.astype(_BF16)
        out.append(hi)
        x = x - hi.astype(_F32)
    out.append(x.astype(_BF16))
    return tuple(out)


def _split2(x):
    return _pieces(x, 2)


def _split3(x):
    return _pieces(x, 3)


def _dot_pieces(a, b, dims=_NN):
    n = max(len(a), len(b))
    terms = [_dot(a[i], b[j], dims) for i in range(len(a)) for j in range(len(b)) if i + j < n]
    total = terms[0]
    if len(terms) > 1:
        rest = terms[1]
        for t in terms[2:]:
            rest = rest + t
        total = total + rest
    return total


def _dot_f32(a, b, dims=_NN):
    return _dot_pieces(_split2(a), _split2(b), dims)


class _Steps:
    def __init__(self, gen):
        self.gen, self.done, self.value = gen, False, None

    def step(self):
        if not self.done:
            try:
                next(self.gen)
            except StopIteration as stop:
                self.done, self.value = True, stop.value

    def finish(self):
        while not self.done:
            self.step()
        return self.value


def _zip_steps(main, *sides):
    count = 0
    while not main.done:
        main.step()
        count += 1
        for side, n, every in sides:
            if count % every == 0:
                for _ in range(n):
                    side.step()
    return main.value


def _rms_scale(x):
    return x * lax.rsqrt(jnp.mean(x * x, axis=-1, keepdims=True) + RMS_EPS)


def _iota(shape, dim):
    return lax.broadcasted_iota(jnp.int32, shape, dim)


def _first_head(width=PAIR):
    return _iota((1, width), 1) % PAIR < HEAD_SIZE


def _head_sums(x):
    first = _first_head()
    outs = []
    for p in range(x.shape[1] // PAIR):
        xp = x[:, p * PAIR:(p + 1) * PAIR]
        lo = jnp.sum(jnp.where(first, xp, 0.0), axis=1, keepdims=True)
        hi = jnp.sum(jnp.where(first, 0.0, xp), axis=1, keepdims=True)
        outs.append(jnp.where(first, lo, hi))
    return jnp.concatenate(outs, axis=1)


def _by_head_rows(x):
    first = _first_head()
    return jnp.concatenate([jnp.where(first, x, 0.0), jnp.where(first, 0.0, x)], axis=0)


def _diag_blocks(x):
    return jnp.where(_first_head(), x[0:HEAD_SIZE], x[HEAD_SIZE:PAIR])


def _layer_spec(a, layer):
    return pl.BlockSpec((None,) + a.shape[1:], lambda i: (layer,) + (0,) * (a.ndim - 1),
                        pipeline_mode=pl.Buffered(1))


def _token_specs(n_parts, tile=TOKEN_TILE, tile_of_step=lambda i: i):
    if n_parts == 1:
        return [pl.BlockSpec((tile, D_MODEL), lambda i: (tile_of_step(i), 0))]
    n_prompt = SEQ // tile
    return [pl.BlockSpec((1, tile, D_MODEL), lambda i: (0, jnp.minimum(tile_of_step(i), n_prompt - 1), 0)),
            pl.BlockSpec((tile // DEC_SEQ, DEC_SEQ, D_MODEL),
                         lambda i: (jnp.maximum(tile_of_step(i) - n_prompt, 0), 0, 0))]


def _load_tokens(x_refs, tile=TOKEN_TILE, tile_index=None):
    if len(x_refs) == 1:
        return x_refs[0][...]
    prompt_ref, stream_ref = x_refs
    tile_index = pl.program_id(0) if tile_index is None else tile_index
    return jnp.where(tile_index >= SEQ // tile, stream_ref[...].reshape(tile, D_MODEL), prompt_ref[0])


N_TOKEN_FIELDS = 5


def _token_fields(tok_ref):
    return [tok_ref.at[:, f * D_RWKV:(f + 1) * D_RWKV] for f in range(N_TOKEN_FIELDS)]


def _running_log_decay(lw):
    tri = jnp.where(_iota((CHUNK, CHUNK), 1) <= _iota((CHUNK, CHUNK), 0), 1.0, 0.0).astype(_BF16)
    return _dot_pieces((tri,), _split3(lw))


def _decayed_operands(r, kf, v, lw, cum, kk, a_sig):
    last = cum[CHUNK - 1:CHUNK]
    e_inv = jnp.exp(-cum)
    e_tail = jnp.exp(last - cum)
    b = kk * a_sig
    return dict(a_t=-kk * jnp.exp(cum - lw), r_t=r * jnp.exp(cum), b_t=b * e_inv, k_t=kf * e_inv,
                b_h=b * e_tail, k_h=kf * e_tail, v=v, dec=jnp.exp(last))


def _intra_pairs(chunks):
    first = _first_head()
    first2 = _first_head(2 * PAIR)
    g_row = _iota((4 * CHUNK, PAIR), 0)
    keep = (_iota((4 * CHUNK, PAIR), 1) % CHUNK) < (g_row % CHUNK) + (g_row // CHUNK) % 2
    rc_xor = _iota((PAIR, PAIR), 0) ^ _iota((PAIR, PAIR), 1)
    eye = rc_xor == 0
    zeros = jnp.zeros((CHUNK, PAIR), _F32)
    items = [(c, p) for c in range(len(chunks)) for p in range(N_PAIRS)]
    part = lambda it, name: chunks[it[0]][name][:, it[1] * PAIR:(it[1] + 1) * PAIR]
    out_pieces = lambda x: _pieces(x, PIECES_OUT)

    gm = {}
    for it in items:
        ar = jnp.concatenate([part(it, "a_t"), part(it, "r_t")], axis=0)
        bk = jnp.concatenate([part(it, "b_t"), part(it, "k_t")], axis=0)
        ar_split = jnp.concatenate([jnp.where(first, ar, 0.0), jnp.where(first, 0.0, ar)], axis=0)
        gm[it] = jnp.where(keep, _dot_pieces(_pieces(ar_split, PIECES_A), _pieces(bk, PIECES_A), _NT), 0.0)
    yield

    l_bd, inv = {}, {}
    for it in items:
        l0, l1 = gm[it][0:CHUNK], gm[it][2 * CHUNK:3 * CHUNK]
        l_bd[it] = jnp.concatenate([jnp.where(first, l0, 0.0),
                                    jnp.where(first, 0.0, pltpu.roll(l1, HEAD_SIZE, axis=1))], axis=0)
        inv[it] = jnp.where(eye, 1.0, jnp.where(rc_xor == 1, l_bd[it], 0.0))
    yield
    m = 2
    while m < CHUNK:
        level = jnp.logical_and(rc_xor >= m, rc_xor < 2 * m)
        slabs = m >= SUBLANES
        low = (lambda a: jnp.concatenate([a[s:s + m] for s in range(m, PAIR, 2 * m)], axis=0)) if slabs else (lambda a: a)
        inv_pieces = {it: _pieces(inv[it], PIECES_INV) for it in items}
        x = {it: _dot_pieces(_pieces(low(jnp.where(level, l_bd[it], 0.0)), PIECES_INV), inv_pieces[it])
             for it in items}
        yield
        for it in items:
            if slabs:
                zeros_m = jnp.zeros((m, PAIR), _F32)
                x_rows = [x[it][j * m:(j + 1) * m] for j in range(PAIR // (2 * m))]
                x_full = jnp.concatenate([a for xr in x_rows for a in (zeros_m, xr)], axis=0)
                upd = _dot_pieces(_pieces(low(inv[it]), PIECES_INV), _pieces(x_full, PIECES_INV))
                inv[it] = jnp.concatenate(
                    [a for j in range(PAIR // (2 * m))
                     for a in (inv[it][2 * j * m:(2 * j + 1) * m],
                               inv[it][(2 * j + 1) * m:(2 * j + 2) * m] + upd[j * m:(j + 1) * m])], axis=0)
            else:
                inv[it] = inv[it] + _dot_pieces(inv_pieces[it], _pieces(x[it], PIECES_INV))
        yield
        m *= 2

    akv = {}
    for it in items:
        v_p = part(it, "v")
        vv_pieces = out_pieces(jnp.concatenate([v_p, v_p], axis=0))
        ak = jnp.where(first, 0.0, jnp.concatenate([gm[it][0:CHUNK], gm[it][2 * CHUNK:3 * CHUNK]], axis=0))
        akv2 = _dot_pieces(out_pieces(ak), vv_pieces)
        akv[it] = jnp.where(first, akv2[0:CHUNK], akv2[CHUNK:2 * CHUNK])
    yield
    uva_pieces = {}
    for it in items:
        tx = _dot_pieces(out_pieces(inv[it]), out_pieces(jnp.concatenate(
            [_by_head_rows(akv[it]), _by_head_rows(part(it, "a_t"))], axis=1)))
        u0a = tx[0:CHUNK] + tx[CHUNK:2 * CHUNK]
        uva = jnp.concatenate([u0a, jnp.concatenate([part(it, "v"), zeros], axis=1)], axis=0)
        uva_pieces[it] = out_pieces(uva)
    yield
    rps, y0s, mts, n0s = {}, {}, {}, {}
    for it in items:
        rr = jnp.concatenate([gm[it][CHUNK:2 * CHUNK], gm[it][3 * CHUNK:4 * CHUNK]], axis=0)
        yr2 = _dot_pieces(out_pieces(rr), uva_pieces[it])
        yr = jnp.where(first2, yr2[0:CHUNK], yr2[CHUNK:2 * CHUNK])
        y0s[it] = yr[:, 0:PAIR]
        rps[it] = part(it, "r_t") + yr[:, PAIR:2 * PAIR]
    yield
    for it in items:
        bk_h = jnp.concatenate([part(it, "b_h"), part(it, "k_h")], axis=0)
        nm = _dot_pieces(out_pieces(bk_h), uva_pieces[it], _TN)
        n0s[it] = _diag_blocks(nm[:, 0:PAIR])
        mts[it] = _diag_blocks(nm[:, PAIR:2 * PAIR] + jnp.where(eye, part(it, "dec"), 0.0))
    cat = lambda d, c: jnp.concatenate([d[(c, p)] for p in range(N_PAIRS)], axis=1)
    return ([cat(rps, c) for c in range(len(chunks))], [cat(y0s, c) for c in range(len(chunks))], mts, n0s)


def _front_kernel(n_x, *refs):
    x_refs, refs = refs[:n_x], refs[n_x:]
    (gpre_ref, w_ref, conv0_ref, shift0_ref, mu_ref, cw_ref, lora_w_ref,
     wd0_ref, a0_ref, kk_w_ref, ka_w_ref, rk_w_ref,
     gate_ref, tok_ref, mn_ref, conv_out_ref, shift_out_ref,
     mix_next, mix_cur, hist_u_scr, hist_pr_scr) = refs
    bgy_ref, rp_ref, y0_ref, g_ref, bg_ref = _token_fields(tok_ref)
    mt_ref, n0_ref = mn_ref.at[:, 0], mn_ref.at[:, 1]
    s = pl.program_id(0)
    is_sample = s - 1 >= N_PROMPT_CHUNKS // INTRA_CHUNKS
    row = _iota((CHUNK, 1), 0)
    cw = cw_ref[...]
    lane = _iota((CHUNK, D_LORA), 1)

    @pl.when(s == 0)
    def _():
        mix_cur[...] = jnp.zeros_like(mix_cur)
        hist_u_scr[...] = jnp.zeros_like(hist_u_scr)
        hist_pr_scr[...] = jnp.zeros_like(hist_pr_scr)

    def project():
        x = _load_tokens(x_refs, FRONT_TILE, jnp.minimum(s, N_TOK // FRONT_TILE - 1))
        h = (_rms_scale(x) * gpre_ref[...]).astype(_BF16)
        yield
        for j in range((MIX_COLS + GATE_COLS) // PROJ_SLAB):
            cols = slice(j * PROJ_SLAB, (j + 1) * PROJ_SLAB)
            p = _dot(h, w_ref[:, cols])
            if j < MIX_COLS // PROJ_SLAB:
                mix_next[:, cols] = p
            else:
                gate_ref[:, j * PROJ_SLAB - MIX_COLS:(j + 1) * PROJ_SLAB - MIX_COLS] = p
            yield

    hist = dict(u=hist_u_scr[...], pr=hist_pr_scr[...])

    def prepare(c):
        rows = slice(c * CHUNK, (c + 1) * CHUNK)
        u_hist = jnp.where(is_sample, conv0_ref[c], hist["u"])
        pr_hist = jnp.where(is_sample, shift0_ref[c], hist["pr"])

        xin = mix_cur[rows, 0:D_CONV]
        bg = mix_cur[rows, D_CONV:2 * D_CONV]
        cg = mix_cur[rows, 2 * D_CONV:3 * D_CONV]
        u = cg * xin
        u1 = jnp.where(row == 0, u_hist[1:2], pltpu.roll(u, 1, axis=0))
        u2 = jnp.where(row == 0, u_hist[0:1], jnp.where(row == 1, u_hist[1:2], pltpu.roll(u, 2, axis=0)))
        bgy_ref[rows, :] = bg * (cw[0:1] * u2 + cw[1:2] * u1 + cw[2:3] * u)
        hist["u"] = u[CHUNK - 2:CHUNK]
        conv_out_ref[c] = hist["u"]
        yield

        pr = mix_cur[rows, 3 * D_CONV:MIX_COLS]
        shifted = jnp.where(row == 0, pr_hist, pltpu.roll(pr, 1, axis=0))
        ps = pr + (shifted - pr) * mu_ref[...]
        hist["pr"] = pr[CHUNK - 1:CHUNK]
        shift_out_ref[c] = hist["pr"]
        r = ps[:, 0:D_RWKV]
        k = ps[:, D_RWKV:2 * D_RWKV]
        v = ps[:, 2 * D_RWKV:3 * D_RWKV]
        yield

        lora_in = ps[:, 3 * D_RWKV:RWKV_COLS]
        act = jnp.where(lane < D_DECAY_LORA, jnp.tanh(lora_in),
                        jnp.where(lane < D_DECAY_LORA + D_AAA_LORA, lora_in, jax.nn.sigmoid(lora_in)))
        lora = _dot(act.astype(_BF16), lora_w_ref[...])
        yield

        w_raw = wd0_ref[...] + lora[:, 0:D_RWKV]
        w_raw = -jax.nn.softplus(-w_raw) - 0.5
        lw = -jnp.exp(w_raw)
        cum = _running_log_decay(lw)
        yield

        a_sig = jax.nn.sigmoid(a0_ref[...] + lora[:, D_RWKV:2 * D_RWKV])
        g = lora[:, 2 * D_RWKV:3 * D_RWKV]
        g_ref[rows, :] = g
        kk = k * kk_w_ref[...]
        yield
        kk = kk / jnp.maximum(jnp.sqrt(_head_sums(kk * kk)), 1e-12)
        yield
        kf = k * (1.0 + (a_sig - 1.0) * ka_w_ref[...])
        bg_ref[rows, :] = _head_sums(r * kf * rk_w_ref[...]) * v * g
        yield
        return _decayed_operands(r, kf, v, lw, cum, kk, a_sig)

    def prepare_group(first_chunk):
        ops = []
        for c in range(first_chunk, first_chunk + INTRA_GROUP):
            ops.append((yield from prepare(c)))
        return ops

    def store(first_chunk, results):
        rps, y0s, mts, n0s = results
        for g in range(INTRA_GROUP):
            c = first_chunk + g
            rows = slice(c * CHUNK, (c + 1) * CHUNK)
            rp_ref[rows, :] = rps[g]
            y0_ref[rows, :] = y0s[g]
            for p in range(N_PAIRS):
                mt_ref[c, p] = mts[(g, p)]
                n0_ref[c, p] = n0s[(g, p)]

    def hand_over():
        for j in range(MIX_COLS // PROJ_SLAB):
            cols = slice(j * PROJ_SLAB, (j + 1) * PROJ_SLAB)
            mix_cur[:, cols] = mix_next[:, cols]
            yield

    assert INTRA_CHUNKS == INTRA_GROUP
    projection = _Steps(project())
    operands = _zip_steps(_Steps(prepare_group(0)), (projection, 1, 1))
    projection.finish()
    hist_u_scr[...] = hist["u"]
    hist_pr_scr[...] = hist["pr"]
    copy = _Steps(hand_over())
    results = _zip_steps(_Steps(_intra_pairs(operands)), (copy, 1, 1))
    copy.finish()
    store(0, results)


def _front(x_parts, layer, norm_pre, w_in_bf16, conv0, shift0, mu, cw, lora_w, wd0, a0, kk_w, ka_w, rk_w):
    n_tiles = N_TOK // FRONT_TILE
    n_prompt_tiles = SEQ // FRONT_TILE
    projected = lambda s: jnp.minimum(s, n_tiles - 1)
    prepared = lambda s: jnp.maximum(s - 1, 0)
    tok = lambda width: pl.BlockSpec((FRONT_TILE, width), lambda s: (prepared(s), 0))
    full = lambda a: _layer_spec(a, layer)
    seq = lambda a: pl.BlockSpec((None, INTRA_CHUNKS) + a.shape[2:],
                                 lambda s: (layer, jnp.maximum(prepared(s) - n_prompt_tiles, 0), 0, 0))
    per_chunk = lambda *dims: pl.BlockSpec((INTRA_CHUNKS,) + dims, lambda s: (prepared(s),) + (0,) * len(dims))
    return pl.pallas_call(
        functools.partial(_front_kernel, len(x_parts)),
        grid=(n_tiles + 1,),
        in_specs=(_token_specs(len(x_parts), FRONT_TILE, projected)
                  + [full(norm_pre), full(w_in_bf16), seq(conv0), seq(shift0), full(mu), full(cw), full(lora_w),
                     full(wd0), full(a0), full(kk_w), full(ka_w), full(rk_w)]),
        out_specs=[pl.BlockSpec((FRONT_TILE, GATE_COLS), lambda s: (projected(s), 0)),
                   tok(N_TOKEN_FIELDS * D_RWKV), per_chunk(2, N_PAIRS, HEAD_SIZE, PAIR),
                   per_chunk(2, D_CONV), per_chunk(1, RWKV_COLS)],
        out_shape=[jax.ShapeDtypeStruct((N_TOK, GATE_COLS), _F32),
                   jax.ShapeDtypeStruct((N_TOK, N_TOKEN_FIELDS * D_RWKV), _F32),
                   jax.ShapeDtypeStruct((N_CHUNKS, 2, N_PAIRS, HEAD_SIZE, PAIR), _F32),
                   jax.ShapeDtypeStruct((N_CHUNKS, 2, D_CONV), _F32),
                   jax.ShapeDtypeStruct((N_CHUNKS, 1, RWKV_COLS), _F32)],
        scratch_shapes=[pltpu.VMEM((FRONT_TILE, MIX_COLS), _F32), pltpu.VMEM((FRONT_TILE, MIX_COLS), _F32),
                        pltpu.VMEM((2, D_CONV), _F32), pltpu.VMEM((1, RWKV_COLS), _F32)],
        compiler_params=pltpu.CompilerParams(
            dimension_semantics=("arbitrary",), vmem_limit_bytes=VMEM_LIMIT_BYTES),
        name="front",
    )(*x_parts, norm_pre, w_in_bf16, conv0, shift0, mu, cw, lora_w, wd0, a0, kk_w, ka_w, rk_w)


def _post_kernel(n_x, *refs):
    x_refs, refs = refs[:n_x], refs[n_x:]
    (tok_ref, gate_ref, mn_ref, s0_ref, lnw_ref, lnb_ref,
     wc_ref, wr_ref, wo_ref, gmix_ref, gpre_ref, wup_ref, wdown_ref, gpost_ref,
     *out_refs, s_prompt_ref, s_sample_ref, state) = refs
    bgy_ref, rp_ref, y0_ref, g_ref, bg_ref = _token_fields(tok_ref)
    mt_ref, n0_ref = mn_ref.at[:, 0], mn_ref.at[:, 1]
    i = pl.program_id(0)
    is_stream = i >= N_PROMPT_TILES

    @pl.when(i == 0)
    def _():
        state[...] = jnp.zeros_like(state)

    eye = (_iota((PAIR, PAIR), 0) == _iota((PAIR, PAIR), 1)).astype(_BF16)
    chunk_rows = lambda c: slice(c * CHUNK, (c + 1) * CHUNK)
    pair_lanes = lambda p: slice(p * PAIR, (p + 1) * PAIR)

    def group_norm(ys):
        y = jnp.concatenate(ys, axis=1)
        inv_n = 1.0 / HEAD_SIZE
        d = y - _head_sums(y) * inv_n
        var = _head_sums(d * d) * inv_n
        return d * lax.rsqrt(var + LN_X_EPS) * lnw_ref[...] + lnb_ref[...]

    def heads_of(s):
        return s[0:HEAD_SIZE, 0:HEAD_SIZE], s[HEAD_SIZE:PAIR, HEAD_SIZE:PAIR]

    def prompt_steps():
        ws = [state[p] for p in range(N_PAIRS)]
        yns = []
        for c in range(TILE_CHUNKS):
            ys = []
            for p in range(N_PAIRS):
                w_pieces = _split2(ws[p])
                ys.append(_dot(rp_ref[chunk_rows(c), pair_lanes(p)].astype(_BF16), w_pieces[0])
                          + y0_ref[chunk_rows(c), pair_lanes(p)])
                ws[p] = (_dot_pieces(_split2(_by_head_rows(mt_ref[c, p])), w_pieces)
                         + _by_head_rows(n0_ref[c, p]))
            yield
            yns.append(group_norm(ys))
            yield
        for p in range(N_PAIRS):
            state[p] = ws[p]
            s_prompt_ref[2 * p], s_prompt_ref[2 * p + 1] = heads_of(_dot_pieces((eye,), _split3(ws[p]), _NT))
        return jnp.concatenate(yns, axis=0)

    def stream_steps():
        zeros = jnp.zeros((HEAD_SIZE, HEAD_SIZE), _F32)
        yns = []
        for c in range(TILE_CHUNKS):
            ys = []
            for p in range(N_PAIRS):
                s0 = jnp.concatenate([jnp.concatenate([s0_ref[c, 2 * p], zeros], axis=1),
                                      jnp.concatenate([zeros, s0_ref[c, 2 * p + 1]], axis=1)], axis=0)
                s0_hi, s0_lo = _split2(s0)
                ys.append(_dot(rp_ref[chunk_rows(c), pair_lanes(p)].astype(_BF16), s0_hi, _NT)
                          + y0_ref[chunk_rows(c), pair_lanes(p)])
                mn = jnp.concatenate([_by_head_rows(mt_ref[c, p]), _by_head_rows(n0_ref[c, p])], axis=1)
                s_next = _dot_pieces((jnp.concatenate([s0_hi, eye], axis=1),
                                      jnp.concatenate([s0_lo, jnp.zeros_like(eye)], axis=1)),
                                     _split2(mn), _NT)
                s_sample_ref[c, 2 * p], s_sample_ref[c, 2 * p + 1] = heads_of(s_next)
            yield
            yns.append(group_norm(ys))
            yield
        return jnp.concatenate(yns, axis=0)

    def gate_steps():
        branch_a = _dot(bgy_ref[...].astype(_BF16), wc_ref[...])
        yield
        gates = []
        for c in range(TILE_CHUNKS):
            gates.append((jax.nn.sigmoid(gate_ref[chunk_rows(c), 0:D_MODEL]),
                          jax.nn.sigmoid(gate_ref[chunk_rows(c), D_MODEL:GATE_COLS])))
            yield
        gate_a = jnp.concatenate([ga for ga, _ in gates], axis=0)
        gate_b = jnp.concatenate([gb for _, gb in gates], axis=0)
        return branch_a, gate_a, gate_b

    def tile_output(recurrence_steps):
        gate_work = _Steps(gate_steps())
        yn = _zip_steps(_Steps(recurrence_steps), (gate_work, 1, 1))
        branch_a, gate_a, gate_b = gate_work.finish()
        z = yn * g_ref[...] + bg_ref[...]
        branch_b = _dot(z.astype(_BF16), wr_ref[...])
        merged = gate_a * branch_a + gate_b * branch_b
        m = _dot(merged.astype(_BF16), wo_ref[...])
        x = _load_tokens(x_refs) + _rms_scale(m) * gmix_ref[...]
        h = (_rms_scale(x) * gpre_ref[...]).astype(_BF16)
        up = _dot(h, wup_ref[...])
        act = jax.nn.silu(up[:, :D_FF]) * up[:, D_FF:]
        f = _dot(act.astype(_BF16), wdown_ref[...])
        return x + _rms_scale(f) * gpost_ref[...]

    @pl.when(jnp.logical_not(is_stream))
    def _():
        out = tile_output(prompt_steps())
        if len(out_refs) == 1:
            out_refs[0][...] = out
        else:
            out_refs[0][0] = out

    @pl.when(is_stream)
    def _():
        out = tile_output(stream_steps())
        if len(out_refs) == 1:
            out_refs[0][...] = out
        else:
            out_refs[1][...] = out.reshape(STREAMS_PER_TILE, DEC_SEQ, D_MODEL)


def _post(x_parts, layer, token_inputs, mn, state_wkv, params, split_out):
    tok = lambda a: pl.BlockSpec((TOKEN_TILE, a.shape[1]), lambda i: (i, 0))
    blk = pl.BlockSpec((TILE_CHUNKS, 2, N_PAIRS, HEAD_SIZE, PAIR), lambda i: (i, 0, 0, 0, 0))
    heads = (N_HEADS, HEAD_SIZE, HEAD_SIZE)
    stream_block = lambda i: jnp.maximum(i - N_PROMPT_TILES, 0)
    st_in = pl.BlockSpec((None, STREAMS_PER_TILE) + heads, lambda i: (layer, stream_block(i), 0, 0, 0))
    st_out = pl.BlockSpec((STREAMS_PER_TILE,) + heads, lambda i: (stream_block(i), 0, 0, 0))
    if split_out:
        out_specs = _token_specs(2)
        out_shape = [jax.ShapeDtypeStruct((1, SEQ, D_MODEL), _F32),
                     jax.ShapeDtypeStruct((DEC_BATCH, DEC_SEQ, D_MODEL), _F32)]
    else:
        out_specs = _token_specs(1)
        out_shape = [jax.ShapeDtypeStruct((N_TOK, D_MODEL), _F32)]
    *x_out, s_prompt, s_sample = pl.pallas_call(
        functools.partial(_post_kernel, len(x_parts)),
        grid=(N_TOK // TOKEN_TILE,),
        in_specs=(_token_specs(len(x_parts)) + [tok(a) for a in token_inputs] + [blk, st_in]
                  + [_layer_spec(a, layer) for a in params]),
        out_specs=out_specs + [pl.BlockSpec(heads, lambda i: (0, 0, 0)), st_out],
        out_shape=out_shape + [jax.ShapeDtypeStruct(heads, _F32),
                               jax.ShapeDtypeStruct((DEC_BATCH,) + heads, _F32)],
        scratch_shapes=[pltpu.VMEM((N_PAIRS, PAIR, PAIR), _F32)],
        compiler_params=pltpu.CompilerParams(
            dimension_semantics=("arbitrary",), vmem_limit_bytes=VMEM_LIMIT_BYTES),
        name="post",
    )(*x_parts, *token_inputs, mn, state_wkv, *params)
    return x_out, s_prompt, s_sample


def _lora_weight(w_decay2, a2, g2):
    w = jnp.zeros((DEPTH, D_LORA, 3 * D_RWKV), _BF16)
    w = w.at[:, 0:D_DECAY_LORA, 0:D_RWKV].set(w_decay2.astype(_BF16))
    w = w.at[:, D_DECAY_LORA:D_DECAY_LORA + D_AAA_LORA, D_RWKV:2 * D_RWKV].set(a2.astype(_BF16))
    return w.at[:, D_DECAY_LORA + D_AAA_LORA:, 2 * D_RWKV:].set(g2.astype(_BF16))


def kernel(x_prompt, x_sample, state_conv, state_shift, state_wkv, norm_mix_pre, norm_mix_post, w_in, mu_shift, conv_w, w_decay0, w_decay2, a0, a2, g2, k_k, k_a, r_k, ln_x_w, ln_x_b, w_conv_out, w_rwkv_out, w_o, norm_ffn_pre, norm_ffn_post, w_ffn_up, w_ffn_down):
    x_parts = (x_prompt, x_sample)
    rows = lambda a: a.reshape(DEPTH, 1, -1)
    bf16 = lambda a: a.astype(_BF16)
    lora_w = _lora_weight(w_decay2, a2, g2)
    w_in_b = bf16(w_in)
    intra_params = (rows(mu_shift), conv_w, lora_w, rows(w_decay0), rows(a0), rows(k_k), rows(k_a), rows(r_k))
    post_params = (rows(ln_x_w), rows(ln_x_b), bf16(w_conv_out), bf16(w_rwkv_out), bf16(w_o),
                   rows(norm_mix_post), rows(norm_ffn_pre), bf16(w_ffn_up), bf16(w_ffn_down), rows(norm_ffn_post))
    norm_pre = rows(norm_mix_pre)
    shift0 = state_shift[:, :, None]
    conv_p, shift_p, wkv_p, conv_s, shift_s, wkv_s = [], [], [], [], [], []
    for l in range(DEPTH):
        gates, fields, mn, conv_out, shift_out = _front(
            x_parts, l, norm_pre, w_in_b, state_conv, shift0, *intra_params)
        x_parts, s_prompt, s_sample = _post(x_parts, l, (fields, gates), mn, state_wkv,
                                            post_params, split_out=(l == DEPTH - 1))

        last = N_PROMPT_CHUNKS - 1
        conv_p.append(conv_out[last:last + 1])
        conv_s.append(conv_out[N_PROMPT_CHUNKS:])
        shift_p.append(shift_out[last:last + 1, 0])
        shift_s.append(shift_out[N_PROMPT_CHUNKS:, 0])
        wkv_p.append(s_prompt[None])
        wkv_s.append(s_sample)
    y_prompt, y_sample = x_parts
    return (y_prompt, y_sample,
            jnp.stack(conv_p, 0), jnp.stack(shift_p, 0), jnp.stack(wkv_p, 0),
            jnp.stack(conv_s, 0), jnp.stack(shift_s, 0), jnp.stack(wkv_s, 0))
```

```python
import functools

import jax
import jax.numpy as jnp
from jax import lax
from jax.experimental import pallas as pl
from jax.experimental.pallas import tpu as pltpu

D_MODEL = 1024
SEQ = 16384
DEPTH = 2
DEC_BATCH = 32
DEC_SEQ = 64
D_CONV = 512
D_RWKV = 512
HEAD_SIZE = 64
N_HEADS = 8
D_DECAY_LORA = 64
D_AAA_LORA = 64
D_GATE_LORA = 128
D_LORA = D_DECAY_LORA + D_AAA_LORA + D_GATE_LORA
LN_X_EPS = 64e-5
D_FF = 2816
RMS_EPS = 1e-6
RWKV_COLS = 3 * D_RWKV + D_LORA
MIX_COLS = 3 * D_CONV + RWKV_COLS
GATE_COLS = 2 * D_MODEL

CHUNK = 64
N_TOK = SEQ + DEC_BATCH * DEC_SEQ
N_CHUNKS = N_TOK // CHUNK
N_PROMPT_CHUNKS = SEQ // CHUNK
PAIR = 2 * HEAD_SIZE
N_PAIRS = N_HEADS // 2
TOKEN_TILE = 256
N_PROMPT_TILES = SEQ // TOKEN_TILE
STREAMS_PER_TILE = TOKEN_TILE // DEC_SEQ
TILE_CHUNKS = TOKEN_TILE // CHUNK
INTRA_CHUNKS = 4
INTRA_GROUP = 4
FRONT_TILE = INTRA_CHUNKS * CHUNK
PROJ_SLAB = 256
SUBLANES = 8
PIECES_A = 1
PIECES_INV = 1
PIECES_OUT = 1
VMEM_LIMIT_BYTES = 56 * 1024 * 1024

_BF16 = jnp.bfloat16
_F32 = jnp.float32
_NN = (((1,), (0,)), ((), ()))
_NT = (((1,), (1,)), ((), ()))
_TN = (((0,), (0,)), ((), ()))


def _dot(a, b, dims=_NN):
    return lax.dot_general(a, b, dims, preferred_element_type=_F32)


def _pieces(x, n):
    out = []
    for _ in range(n - 1):
        hi = x.astype(_BF16)
        out.append(hi)
        x = x - hi.astype(_F32)
    out.append(x.astype(_BF16))
    return tuple(out)


def _split2(x):
    return _pieces(x, 2)


def _split3(x):
    return _pieces(x, 3)


def _dot_pieces(a, b, dims=_NN):
    n = max(len(a), len(b))
    terms = [_dot(a[i], b[j], dims) for i in range(len(a)) for j in range(len(b)) if i + j < n]
    total = terms[0]
    if len(terms) > 1:
        rest = terms[1]
        for t in terms[2:]:
            rest = rest + t
        total = total + rest
    return total


def _dot_f32(a, b, dims=_NN):
    return _dot_pieces(_split2(a), _split2(b), dims)


class _Steps:
    def __init__(self, gen):
        self.gen, self.done, self.value = gen, False, None

    def step(self):
        if not self.done:
            try:
                next(self.gen)
            except StopIteration as stop:
                self.done, self.value = True, stop.value

    def finish(self):
        while not self.done:
            self.step()
        return self.value


def _zip_steps(main, *sides):
    count = 0
    while not main.done:
        main.step()
        count += 1
        for side, n, every in sides:
            if count % every == 0:
                for _ in range(n):
                    side.step()
    return main.value


def _rms_scale(x):
    return x * lax.rsqrt(jnp.mean(x * x, axis=-1, keepdims=True) + RMS_EPS)


def _iota(shape, dim):
    return lax.broadcasted_iota(jnp.int32, shape, dim)


def _first_head(width=PAIR):
    return _iota((1, width), 1) % PAIR < HEAD_SIZE


def _head_sums(x):
    first = _first_head()
    outs = []
    for p in range(x.shape[1] // PAIR):
        xp = x[:, p * PAIR:(p + 1) * PAIR]
        lo = jnp.sum(jnp.where(first, xp, 0.0), axis=1, keepdims=True)
        hi = jnp.sum(jnp.where(first, 0.0, xp), axis=1, keepdims=True)
        outs.append(jnp.where(first, lo, hi))
    return jnp.concatenate(outs, axis=1)


def _by_head_rows(x):
    first = _first_head()
    return jnp.concatenate([jnp.where(first, x, 0.0), jnp.where(first, 0.0, x)], axis=0)


def _diag_blocks(x):
    return jnp.where(_first_head(), x[0:HEAD_SIZE], x[HEAD_SIZE:PAIR])


def _layer_spec(a, layer):
    return pl.BlockSpec((None,) + a.shape[1:], lambda i: (layer,) + (0,) * (a.ndim - 1),
                        pipeline_mode=pl.Buffered(1))


def _token_specs(n_parts, tile=TOKEN_TILE, tile_of_step=lambda i: i):
    if n_parts == 1:
        return [pl.BlockSpec((tile, D_MODEL), lambda i: (tile_of_step(i), 0))]
    n_prompt = SEQ // tile
    return [pl.BlockSpec((1, tile, D_MODEL), lambda i: (0, jnp.minimum(tile_of_step(i), n_prompt - 1), 0)),
            pl.BlockSpec((tile // DEC_SEQ, DEC_SEQ, D_MODEL),
                         lambda i: (jnp.maximum(tile_of_step(i) - n_prompt, 0), 0, 0))]


def _load_tokens(x_refs, tile=TOKEN_TILE, tile_index=None):
    if len(x_refs) == 1:
        return x_refs[0][...]
    prompt_ref, stream_ref = x_refs
    tile_index = pl.program_id(0) if tile_index is None else tile_index
    return jnp.where(tile_index >= SEQ // tile, stream_ref[...].reshape(tile, D_MODEL), prompt_ref[0])


N_TOKEN_FIELDS = 5


def _token_fields(tok_ref):
    return [tok_ref.at[:, f * D_RWKV:(f + 1) * D_RWKV] for f in range(N_TOKEN_FIELDS)]


def _running_log_decay(lw):
    tri = jnp.where(_iota((CHUNK, CHUNK), 1) <= _iota((CHUNK, CHUNK), 0), 1.0, 0.0).astype(_BF16)
    return _dot_pieces((tri,), _split3(lw))


def _decayed_operands(r, kf, v, lw, cum, kk, a_sig):
    last = cum[CHUNK - 1:CHUNK]
    e_inv = jnp.exp(-cum)
    e_tail = jnp.exp(last - cum)
    b = kk * a_sig
    return dict(a_t=-kk * jnp.exp(cum - lw), r_t=r * jnp.exp(cum), b_t=b * e_inv, k_t=kf * e_inv,
                b_h=b * e_tail, k_h=kf * e_tail, v=v, dec=jnp.exp(last))


def _intra_pairs(chunks):
    first = _first_head()
    first2 = _first_head(2 * PAIR)
    g_row = _iota((2 * CHUNK, 2 * PAIR), 0)
    keep = (_iota((2 * CHUNK, 2 * PAIR), 1) % CHUNK) < (g_row % CHUNK) + g_row // CHUNK
    rc_xor = _iota((CHUNK, PAIR), 0) ^ (_iota((CHUNK, PAIR), 1) % HEAD_SIZE)
    eye = _iota((PAIR, PAIR), 0) == _iota((PAIR, PAIR), 1)
    zeros = jnp.zeros((CHUNK, PAIR), _F32)
    items = [(c, p) for c in range(len(chunks)) for p in range(N_PAIRS)]
    part = lambda it, name: chunks[it[0]][name][:, it[1] * PAIR:(it[1] + 1) * PAIR]
    out_pieces = lambda x: _pieces(x, PIECES_OUT)
    inv_pieces = lambda x: _pieces(x, PIECES_INV)

    gm, l_c, ak_c = {}, {}, {}
    for it in items:
        ar = jnp.concatenate([part(it, "a_t"), part(it, "r_t")], axis=0)
        bk = jnp.concatenate([part(it, "b_t"), part(it, "k_t")], axis=0)
        bk_heads = jnp.concatenate([jnp.where(first, bk, 0.0), jnp.where(first, 0.0, bk)], axis=0)
        gm[it] = jnp.where(keep, _dot_pieces(_pieces(ar, PIECES_A), _pieces(bk_heads, PIECES_A), _NT), 0.0)
        a_h0, a_h1 = gm[it][0:CHUNK, 0:PAIR], gm[it][0:CHUNK, PAIR:2 * PAIR]
        l_c[it] = jnp.where(first, a_h0, pltpu.roll(a_h1, HEAD_SIZE, axis=1))
        ak_c[it] = jnp.where(first, pltpu.roll(a_h0, HEAD_SIZE, axis=1), a_h1)
    yield

    inv = {it: jnp.where(rc_xor == 0, 1.0, jnp.where(rc_xor == 1, l_c[it], 0.0)) for it in items}
    yield
    m = 2
    while m < CHUNK:
        level = jnp.logical_and(rc_xor >= m, rc_xor < 2 * m)
        slabs = m >= SUBLANES
        n_blocks = CHUNK // (2 * m)
        low = (lambda a: jnp.concatenate([a[s:s + m] for s in range(m, CHUNK, 2 * m)], axis=0)) if slabs else (lambda a: a)
        inv_bd = {it: inv_pieces(_by_head_rows(inv[it])) for it in items}
        x = {it: _dot_pieces(inv_pieces(low(jnp.where(level, l_c[it], 0.0))), inv_bd[it]) for it in items}
        yield
        for it in items:
            if slabs:
                zeros_m = jnp.zeros((m, PAIR), _F32)
                x_full = jnp.concatenate([a for j in range(n_blocks)
                                          for a in (zeros_m, x[it][j * m:(j + 1) * m])], axis=0)
                upd = _dot_pieces(inv_pieces(low(inv[it])), inv_pieces(_by_head_rows(x_full)))
                inv[it] = jnp.concatenate(
                    [a for j in range(n_blocks)
                     for a in (inv[it][2 * j * m:(2 * j + 1) * m],
                               inv[it][(2 * j + 1) * m:(2 * j + 2) * m] + upd[j * m:(j + 1) * m])], axis=0)
            else:
                inv[it] = inv[it] + _dot_pieces(inv_pieces(inv[it]), inv_pieces(_by_head_rows(x[it])))
        yield
        m *= 2

    akv = {}
    for it in items:
        akv[it] = _dot_pieces(out_pieces(ak_c[it]), out_pieces(_by_head_rows(part(it, "v"))))
    yield
    uva_pieces = {}
    for it in items:
        u0a = _dot_pieces(out_pieces(inv[it]), out_pieces(jnp.concatenate(
            [_by_head_rows(akv[it]), _by_head_rows(part(it, "a_t"))], axis=1)))
        uva = jnp.concatenate([u0a, jnp.concatenate([part(it, "v"), zeros], axis=1)], axis=0)
        uva_pieces[it] = out_pieces(uva)
    yield
    rps, y0s, mts, n0s = {}, {}, {}, {}
    for it in items:
        rr = jnp.concatenate([gm[it][CHUNK:2 * CHUNK, 0:PAIR], gm[it][CHUNK:2 * CHUNK, PAIR:2 * PAIR]],
                             axis=0)
        yr2 = _dot_pieces(out_pieces(rr), uva_pieces[it])
        yr = jnp.where(first2, yr2[0:CHUNK], yr2[CHUNK:2 * CHUNK])
        y0s[it] = yr[:, 0:PAIR]
        rps[it] = part(it, "r_t") + yr[:, PAIR:2 * PAIR]
    yield
    for it in items:
        bk_h = jnp.concatenate([part(it, "b_h"), part(it, "k_h")], axis=0)
        nm = _dot_pieces(out_pieces(bk_h), uva_pieces[it], _TN)
        n0s[it] = _diag_blocks(nm[:, 0:PAIR])
        mts[it] = _diag_blocks(nm[:, PAIR:2 * PAIR] + jnp.where(eye, part(it, "dec"), 0.0))
    cat = lambda d, c: jnp.concatenate([d[(c, p)] for p in range(N_PAIRS)], axis=1)
    return ([cat(rps, c) for c in range(len(chunks))], [cat(y0s, c) for c in range(len(chunks))], mts, n0s)


def _front_kernel(n_x, *refs):
    x_refs, refs = refs[:n_x], refs[n_x:]
    (gpre_ref, w_ref, conv0_ref, shift0_ref, mu_ref, cw_ref, lora_w_ref,
     wd0_ref, a0_ref, kk_w_ref, ka_w_ref, rk_w_ref,
     gate_ref, tok_ref, mn_ref, conv_out_ref, shift_out_ref,
     mix_next, mix_cur, hist_u_scr, hist_pr_scr) = refs
    bgy_ref, rp_ref, y0_ref, g_ref, bg_ref = _token_fields(tok_ref)
    mt_ref, n0_ref = mn_ref.at[:, 0], mn_ref.at[:, 1]
    s = pl.program_id(0)
    is_sample = s - 1 >= N_PROMPT_CHUNKS // INTRA_CHUNKS
    row = _iota((CHUNK, 1), 0)
    cw = cw_ref[...]
    lane = _iota((CHUNK, D_LORA), 1)

    @pl.when(s == 0)
    def _():
        mix_cur[...] = jnp.zeros_like(mix_cur)
        hist_u_scr[...] = jnp.zeros_like(hist_u_scr)
        hist_pr_scr[...] = jnp.zeros_like(hist_pr_scr)

    def project():
        x = _load_tokens(x_refs, FRONT_TILE, jnp.minimum(s, N_TOK // FRONT_TILE - 1))
        h = (_rms_scale(x) * gpre_ref[...]).astype(_BF16)
        yield
        for j in range((MIX_COLS + GATE_COLS) // PROJ_SLAB):
            cols = slice(j * PROJ_SLAB, (j + 1) * PROJ_SLAB)
            p = _dot(h, w_ref[:, cols])
            if j < MIX_COLS // PROJ_SLAB:
                mix_next[:, cols] = p
            else:
                gate_ref[:, j * PROJ_SLAB - MIX_COLS:(j + 1) * PROJ_SLAB - MIX_COLS] = p
            yield

    hist = dict(u=hist_u_scr[...], pr=hist_pr_scr[...])

    def prepare(c):
        rows = slice(c * CHUNK, (c + 1) * CHUNK)
        u_hist = jnp.where(is_sample, conv0_ref[c], hist["u"])
        pr_hist = jnp.where(is_sample, shift0_ref[c], hist["pr"])

        xin = mix_cur[rows, 0:D_CONV]
        bg = mix_cur[rows, D_CONV:2 * D_CONV]
        cg = mix_cur[rows, 2 * D_CONV:3 * D_CONV]
        u = cg * xin
        u1 = jnp.where(row == 0, u_hist[1:2], pltpu.roll(u, 1, axis=0))
        u2 = jnp.where(row == 0, u_hist[0:1], jnp.where(row == 1, u_hist[1:2], pltpu.roll(u, 2, axis=0)))
        bgy_ref[rows, :] = bg * (cw[0:1] * u2 + cw[1:2] * u1 + cw[2:3] * u)
        hist["u"] = u[CHUNK - 2:CHUNK]
        conv_out_ref[c] = hist["u"]
        yield

        pr = mix_cur[rows, 3 * D_CONV:MIX_COLS]
        shifted = jnp.where(row == 0, pr_hist, pltpu.roll(pr, 1, axis=0))
        ps = pr + (shifted - pr) * mu_ref[...]
        hist["pr"] = pr[CHUNK - 1:CHUNK]
        shift_out_ref[c] = hist["pr"]
        r = ps[:, 0:D_RWKV]
        k = ps[:, D_RWKV:2 * D_RWKV]
        v = ps[:, 2 * D_RWKV:3 * D_RWKV]
        yield

        lora_in = ps[:, 3 * D_RWKV:RWKV_COLS]
        act = jnp.where(lane < D_DECAY_LORA, jnp.tanh(lora_in),
                        jnp.where(lane < D_DECAY_LORA + D_AAA_LORA, lora_in, jax.nn.sigmoid(lora_in)))
        lora = _dot(act.astype(_BF16), lora_w_ref[...])
        yield

        w_raw = wd0_ref[...] + lora[:, 0:D_RWKV]
        w_raw = -jax.nn.softplus(-w_raw) - 0.5
        lw = -jnp.exp(w_raw)
        cum = _running_log_decay(lw)
        yield

        a_sig = jax.nn.sigmoid(a0_ref[...] + lora[:, D_RWKV:2 * D_RWKV])
        g = lora[:, 2 * D_RWKV:3 * D_RWKV]
        g_ref[rows, :] = g
        kk = k * kk_w_ref[...]
        yield
        kk = kk / jnp.maximum(jnp.sqrt(_head_sums(kk * kk)), 1e-12)
        yield
        kf = k * (1.0 + (a_sig - 1.0) * ka_w_ref[...])
        bg_ref[rows, :] = _head_sums(r * kf * rk_w_ref[...]) * v * g
        yield
        return _decayed_operands(r, kf, v, lw, cum, kk, a_sig)

    def prepare_group(first_chunk):
        ops = []
        for c in range(first_chunk, first_chunk + INTRA_GROUP):
            ops.append((yield from prepare(c)))
        return ops

    def store(first_chunk, results):
        rps, y0s, mts, n0s = results
        for g in range(INTRA_GROUP):
            c = first_chunk + g
            rows = slice(c * CHUNK, (c + 1) * CHUNK)
            rp_ref[rows, :] = rps[g]
            y0_ref[rows, :] = y0s[g]
            for p in range(N_PAIRS):
                mt_ref[c, p] = mts[(g, p)]
                n0_ref[c, p] = n0s[(g, p)]

    def hand_over():
        for j in range(MIX_COLS // PROJ_SLAB):
            cols = slice(j * PROJ_SLAB, (j + 1) * PROJ_SLAB)
            mix_cur[:, cols] = mix_next[:, cols]
            yield

    assert INTRA_CHUNKS == INTRA_GROUP
    n_mix_steps = 1 + MIX_COLS // PROJ_SLAB
    n_prepare_steps = 7 * INTRA_GROUP
    assert n_prepare_steps % n_mix_steps == 0
    projection = _Steps(project())
    operands = _zip_steps(_Steps(prepare_group(0)), (projection, 1, n_prepare_steps // n_mix_steps))
    hist_u_scr[...] = hist["u"]
    hist_pr_scr[...] = hist["pr"]
    copy = _Steps(hand_over())
    results = _zip_steps(_Steps(_intra_pairs(operands)), (projection, 1, 1), (copy, 1, 1))
    projection.finish()
    copy.finish()
    store(0, results)


def _front(x_parts, layer, norm_pre, w_in_bf16, conv0, shift0, mu, cw, lora_w, wd0, a0, kk_w, ka_w, rk_w):
    n_tiles = N_TOK // FRONT_TILE
    n_prompt_tiles = SEQ // FRONT_TILE
    projected = lambda s: jnp.minimum(s, n_tiles - 1)
    prepared = lambda s: jnp.maximum(s - 1, 0)
    tok = lambda width: pl.BlockSpec((FRONT_TILE, width), lambda s: (prepared(s), 0))
    full = lambda a: _layer_spec(a, layer)
    seq = lambda a: pl.BlockSpec((None, INTRA_CHUNKS) + a.shape[2:],
                                 lambda s: (layer, jnp.maximum(prepared(s) - n_prompt_tiles, 0), 0, 0))
    per_chunk = lambda *dims: pl.BlockSpec((INTRA_CHUNKS,) + dims, lambda s: (prepared(s),) + (0,) * len(dims))
    return pl.pallas_call(
        functools.partial(_front_kernel, len(x_parts)),
        grid=(n_tiles + 1,),
        in_specs=(_token_specs(len(x_parts), FRONT_TILE, projected)
                  + [full(norm_pre), full(w_in_bf16), seq(conv0), seq(shift0), full(mu), full(cw), full(lora_w),
                     full(wd0), full(a0), full(kk_w), full(ka_w), full(rk_w)]),
        out_specs=[pl.BlockSpec((FRONT_TILE, GATE_COLS), lambda s: (projected(s), 0)),
                   tok(N_TOKEN_FIELDS * D_RWKV), per_chunk(2, N_PAIRS, HEAD_SIZE, PAIR),
                   per_chunk(2, D_CONV), per_chunk(1, RWKV_COLS)],
        out_shape=[jax.ShapeDtypeStruct((N_TOK, GATE_COLS), _F32),
                   jax.ShapeDtypeStruct((N_TOK, N_TOKEN_FIELDS * D_RWKV), _F32),
                   jax.ShapeDtypeStruct((N_CHUNKS, 2, N_PAIRS, HEAD_SIZE, PAIR), _F32),
                   jax.ShapeDtypeStruct((N_CHUNKS, 2, D_CONV), _F32),
                   jax.ShapeDtypeStruct((N_CHUNKS, 1, RWKV_COLS), _F32)],
        scratch_shapes=[pltpu.VMEM((FRONT_TILE, MIX_COLS), _F32), pltpu.VMEM((FRONT_TILE, MIX_COLS), _F32),
                        pltpu.VMEM((2, D_CONV), _F32), pltpu.VMEM((1, RWKV_COLS), _F32)],
        compiler_params=pltpu.CompilerParams(
            dimension_semantics=("arbitrary",), vmem_limit_bytes=VMEM_LIMIT_BYTES),
        name="front",
    )(*x_parts, norm_pre, w_in_bf16, conv0, shift0, mu, cw, lora_w, wd0, a0, kk_w, ka_w, rk_w)


def _post_kernel(n_x, *refs):
    x_refs, refs = refs[:n_x], refs[n_x:]
    (tok_ref, gate_ref, mn_ref, s0_ref, lnw_ref, lnb_ref,
     wc_ref, wr_ref, wo_ref, gmix_ref, gpre_ref, wup_ref, wdown_ref, gpost_ref,
     *out_refs, s_prompt_ref, s_sample_ref, state) = refs
    bgy_ref, rp_ref, y0_ref, g_ref, bg_ref = _token_fields(tok_ref)
    mt_ref, n0_ref = mn_ref.at[:, 0], mn_ref.at[:, 1]
    i = pl.program_id(0)
    is_stream = i >= N_PROMPT_TILES

    @pl.when(i == 0)
    def _():
        state[...] = jnp.zeros_like(state)

    eye = (_iota((PAIR, PAIR), 0) == _iota((PAIR, PAIR), 1)).astype(_BF16)
    chunk_rows = lambda c: slice(c * CHUNK, (c + 1) * CHUNK)
    pair_lanes = lambda p: slice(p * PAIR, (p + 1) * PAIR)

    def group_norm(ys):
        y = jnp.concatenate(ys, axis=1)
        inv_n = 1.0 / HEAD_SIZE
        d = y - _head_sums(y) * inv_n
        var = _head_sums(d * d) * inv_n
        return d * lax.rsqrt(var + LN_X_EPS) * lnw_ref[...] + lnb_ref[...]

    def heads_of(s):
        return s[0:HEAD_SIZE, 0:HEAD_SIZE], s[HEAD_SIZE:PAIR, HEAD_SIZE:PAIR]

    def prompt_steps():
        ws = [state[p] for p in range(N_PAIRS)]
        yns = []
        for c in range(TILE_CHUNKS):
            ys = []
            for p in range(N_PAIRS):
                w_pieces = _split2(ws[p])
                ys.append(_dot(rp_ref[chunk_rows(c), pair_lanes(p)].astype(_BF16), w_pieces[0])
                          + y0_ref[chunk_rows(c), pair_lanes(p)])
                ws[p] = _by_head_rows(_dot_pieces(_split2(mt_ref[c, p]), w_pieces) + n0_ref[c, p])
            yield
            yns.append(group_norm(ys))
            yield
        for p in range(N_PAIRS):
            state[p] = ws[p]
            s_prompt_ref[2 * p], s_prompt_ref[2 * p + 1] = heads_of(_dot_pieces((eye,), _split3(ws[p]), _NT))
        return jnp.concatenate(yns, axis=0)

    def stream_steps():
        zeros = jnp.zeros((HEAD_SIZE, HEAD_SIZE), _F32)
        yns = []
        for c in range(TILE_CHUNKS):
            ys = []
            for p in range(N_PAIRS):
                s0 = jnp.concatenate([jnp.concatenate([s0_ref[c, 2 * p], zeros], axis=1),
                                      jnp.concatenate([zeros, s0_ref[c, 2 * p + 1]], axis=1)], axis=0)
                s0_hi, s0_lo = _split2(s0)
                ys.append(_dot(rp_ref[chunk_rows(c), pair_lanes(p)].astype(_BF16), s0_hi, _NT)
                          + y0_ref[chunk_rows(c), pair_lanes(p)])
                mn = jnp.concatenate([_by_head_rows(mt_ref[c, p]), _by_head_rows(n0_ref[c, p])], axis=1)
                s_next = _dot_pieces((jnp.concatenate([s0_hi, eye], axis=1),
                                      jnp.concatenate([s0_lo, jnp.zeros_like(eye)], axis=1)),
                                     _split2(mn), _NT)
                s_sample_ref[c, 2 * p], s_sample_ref[c, 2 * p + 1] = heads_of(s_next)
            yield
            yns.append(group_norm(ys))
            yield
        return jnp.concatenate(yns, axis=0)

    def gate_steps():
        branch_a = _dot(bgy_ref[...].astype(_BF16), wc_ref[...])
        yield
        gates = []
        for c in range(TILE_CHUNKS):
            gates.append((jax.nn.sigmoid(gate_ref[chunk_rows(c), 0:D_MODEL]),
                          jax.nn.sigmoid(gate_ref[chunk_rows(c), D_MODEL:GATE_COLS])))
            yield
        gate_a = jnp.concatenate([ga for ga, _ in gates], axis=0)
        gate_b = jnp.concatenate([gb for _, gb in gates], axis=0)
        return branch_a, gate_a, gate_b

    def tile_output(recurrence_steps):
        gate_work = _Steps(gate_steps())
        yn = _zip_steps(_Steps(recurrence_steps), (gate_work, 1, 1))
        branch_a, gate_a, gate_b = gate_work.finish()
        z = yn * g_ref[...] + bg_ref[...]
        branch_b = _dot(z.astype(_BF16), wr_ref[...])
        merged = gate_a * branch_a + gate_b * branch_b
        m = _dot(merged.astype(_BF16), wo_ref[...])
        x = _load_tokens(x_refs) + _rms_scale(m) * gmix_ref[...]
        h = (_rms_scale(x) * gpre_ref[...]).astype(_BF16)
        up = _dot(h, wup_ref[...])
        act = jax.nn.silu(up[:, :D_FF]) * up[:, D_FF:]
        f = _dot(act.astype(_BF16), wdown_ref[...])
        return x + _rms_scale(f) * gpost_ref[...]

    @pl.when(jnp.logical_not(is_stream))
    def _():
        out = tile_output(prompt_steps())
        if len(out_refs) == 1:
            out_refs[0][...] = out
        else:
            out_refs[0][0] = out

    @pl.when(is_stream)
    def _():
        out = tile_output(stream_steps())
        if len(out_refs) == 1:
            out_refs[0][...] = out
        else:
            out_refs[1][...] = out.reshape(STREAMS_PER_TILE, DEC_SEQ, D_MODEL)


def _post(x_parts, layer, token_inputs, mn, state_wkv, params, split_out):
    tok = lambda a: pl.BlockSpec((TOKEN_TILE, a.shape[1]), lambda i: (i, 0))
    blk = pl.BlockSpec((TILE_CHUNKS, 2, N_PAIRS, HEAD_SIZE, PAIR), lambda i: (i, 0, 0, 0, 0))
    heads = (N_HEADS, HEAD_SIZE, HEAD_SIZE)
    stream_block = lambda i: jnp.maximum(i - N_PROMPT_TILES, 0)
    st_in = pl.BlockSpec((None, STREAMS_PER_TILE) + heads, lambda i: (layer, stream_block(i), 0, 0, 0))
    st_out = pl.BlockSpec((STREAMS_PER_TILE,) + heads, lambda i: (stream_block(i), 0, 0, 0))
    if split_out:
        out_specs = _token_specs(2)
        out_shape = [jax.ShapeDtypeStruct((1, SEQ, D_MODEL), _F32),
                     jax.ShapeDtypeStruct((DEC_BATCH, DEC_SEQ, D_MODEL), _F32)]
    else:
        out_specs = _token_specs(1)
        out_shape = [jax.ShapeDtypeStruct((N_TOK, D_MODEL), _F32)]
    *x_out, s_prompt, s_sample = pl.pallas_call(
        functools.partial(_post_kernel, len(x_parts)),
        grid=(N_TOK // TOKEN_TILE,),
        in_specs=(_token_specs(len(x_parts)) + [tok(a) for a in token_inputs] + [blk, st_in]
                  + [_layer_spec(a, layer) for a in params]),
        out_specs=out_specs + [pl.BlockSpec(heads, lambda i: (0, 0, 0)), st_out],
        out_shape=out_shape + [jax.ShapeDtypeStruct(heads, _F32),
                               jax.ShapeDtypeStruct((DEC_BATCH,) + heads, _F32)],
        scratch_shapes=[pltpu.VMEM((N_PAIRS, PAIR, PAIR), _F32)],
        compiler_params=pltpu.CompilerParams(
            dimension_semantics=("arbitrary",), vmem_limit_bytes=VMEM_LIMIT_BYTES),
        name="post",
    )(*x_parts, *token_inputs, mn, state_wkv, *params)
    return x_out, s_prompt, s_sample


def _lora_weight(w_decay2, a2, g2):
    w = jnp.zeros((DEPTH, D_LORA, 3 * D_RWKV), _BF16)
    w = w.at[:, 0:D_DECAY_LORA, 0:D_RWKV].set(w_decay2.astype(_BF16))
    w = w.at[:, D_DECAY_LORA:D_DECAY_LORA + D_AAA_LORA, D_RWKV:2 * D_RWKV].set(a2.astype(_BF16))
    return w.at[:, D_DECAY_LORA + D_AAA_LORA:, 2 * D_RWKV:].set(g2.astype(_BF16))


def kernel(x_prompt, x_sample, state_conv, state_shift, state_wkv, norm_mix_pre, norm_mix_post, w_in, mu_shift, conv_w, w_decay0, w_decay2, a0, a2, g2, k_k, k_a, r_k, ln_x_w, ln_x_b, w_conv_out, w_rwkv_out, w_o, norm_ffn_pre, norm_ffn_post, w_ffn_up, w_ffn_down):
    x_parts = (x_prompt, x_sample)
    rows = lambda a: a.reshape(DEPTH, 1, -1)
    bf16 = lambda a: a.astype(_BF16)
    lora_w = _lora_weight(w_decay2, a2, g2)
    w_in_b = bf16(w_in)
    intra_params = (rows(mu_shift), conv_w, lora_w, rows(w_decay0), rows(a0), rows(k_k), rows(k_a), rows(r_k))
    post_params = (rows(ln_x_w), rows(ln_x_b), bf16(w_conv_out), bf16(w_rwkv_out), bf16(w_o),
                   rows(norm_mix_post), rows(norm_ffn_pre), bf16(w_ffn_up), bf16(w_ffn_down), rows(norm_ffn_post))
    norm_pre = rows(norm_mix_pre)
    shift0 = state_shift[:, :, None]
    conv_p, shift_p, wkv_p, conv_s, shift_s, wkv_s = [], [], [], [], [], []
    for l in range(DEPTH):
        gates, fields, mn, conv_out, shift_out = _front(
            x_parts, l, norm_pre, w_in_b, state_conv, shift0, *intra_params)
        x_parts, s_prompt, s_sample = _post(x_parts, l, (fields, gates), mn, state_wkv,
                                            post_params, split_out=(l == DEPTH - 1))

        last = N_PROMPT_CHUNKS - 1
        conv_p.append(conv_out[last:last + 1])
        conv_s.append(conv_out[N_PROMPT_CHUNKS:])
        shift_p.append(shift_out[last:last + 1, 0])
        shift_s.append(shift_out[N_PROMPT_CHUNKS:, 0])
        wkv_p.append(s_prompt[None])
        wkv_s.append(s_sample)
    y_prompt, y_sample = x_parts
    return (y_prompt, y_sample,
            jnp.stack(conv_p, 0), jnp.stack(shift_p, 0), jnp.stack(wkv_p, 0),
            jnp.stack(conv_s, 0), jnp.stack(shift_s, 0), jnp.stack(wkv_s, 0))
```

```python
import functools

import jax
import jax.numpy as jnp
from jax import lax
from jax.experimental import pallas as pl
from jax.experimental.pallas import tpu as pltpu

D_MODEL = 1024
SEQ = 16384
DEPTH = 2
DEC_BATCH = 32
DEC_SEQ = 64
D_CONV = 512
D_RWKV = 512
HEAD_SIZE = 64
N_HEADS = 8
D_DECAY_LORA = 64
D_AAA_LORA = 64
D_GATE_LORA = 128
D_LORA = D_DECAY_LORA + D_AAA_LORA + D_GATE_LORA
LN_X_EPS = 64e-5
D_FF = 2816
RMS_EPS = 1e-6
RWKV_COLS = 3 * D_RWKV + D_LORA
MIX_COLS = 3 * D_CONV + RWKV_COLS
GATE_COLS = 2 * D_MODEL

CHUNK = 64
N_TOK = SEQ + DEC_BATCH * DEC_SEQ
N_CHUNKS = N_TOK // CHUNK
N_PROMPT_CHUNKS = SEQ // CHUNK
PAIR = 2 * HEAD_SIZE
N_PAIRS = N_HEADS // 2
TOKEN_TILE = 256
N_PROMPT_TILES = SEQ // TOKEN_TILE
STREAMS_PER_TILE = TOKEN_TILE // DEC_SEQ
TILE_CHUNKS = TOKEN_TILE // CHUNK
INTRA_CHUNKS = 4
INTRA_GROUP = 4
FRONT_TILE = INTRA_CHUNKS * CHUNK
PROJ_SLAB = 256
SUBLANES = 8
BF16_SUBLANES = 16
PIECES_A = 1
PIECES_INV = 1
PIECES_OUT = 1
VMEM_LIMIT_BYTES = 56 * 1024 * 1024

_BF16 = jnp.bfloat16
_F32 = jnp.float32
_NN = (((1,), (0,)), ((), ()))
_NT = (((1,), (1,)), ((), ()))
_TN = (((0,), (0,)), ((), ()))


def _dot(a, b, dims=_NN):
    return lax.dot_general(a, b, dims, preferred_element_type=_F32)


def _pieces(x, n):
    out = []
    for _ in range(n - 1):
        hi = x.astype(_BF16)
        out.append(hi)
        x = x - hi.astype(_F32)
    out.append(x.astype(_BF16))
    return tuple(out)


def _split2(x):
    return _pieces(x, 2)


def _split3(x):
    return _pieces(x, 3)


def _dot_pieces(a, b, dims=_NN):
    n = max(len(a), len(b))
    terms = [_dot(a[i], b[j], dims) for i in range(len(a)) for j in range(len(b)) if i + j < n]
    total = terms[0]
    if len(terms) > 1:
        rest = terms[1]
        for t in terms[2:]:
            rest = rest + t
        total = total + rest
    return total


def _dot_f32(a, b, dims=_NN):
    return _dot_pieces(_split2(a), _split2(b), dims)


class _Steps:
    def __init__(self, gen):
        self.gen, self.done, self.value = gen, False, None

    def step(self):
        if not self.done:
            try:
                next(self.gen)
            except StopIteration as stop:
                self.done, self.value = True, stop.value

    def finish(self):
        while not self.done:
            self.step()
        return self.value


def _zip_steps(main, *sides):
    count = 0
    while not main.done:
        main.step()
        count += 1
        for side, n, every in sides:
            if count % every == 0:
                for _ in range(n):
                    side.step()
    return main.value


def _rms_scale(x):
    return x * lax.rsqrt(jnp.mean(x * x, axis=-1, keepdims=True) + RMS_EPS)


def _iota(shape, dim):
    return lax.broadcasted_iota(jnp.int32, shape, dim)


def _first_head(width=PAIR):
    return _iota((1, width), 1) % PAIR < HEAD_SIZE


def _head_sums(x):
    first = _first_head()
    outs = []
    for p in range(x.shape[1] // PAIR):
        xp = x[:, p * PAIR:(p + 1) * PAIR]
        lo = jnp.sum(jnp.where(first, xp, 0.0), axis=1, keepdims=True)
        hi = jnp.sum(jnp.where(first, 0.0, xp), axis=1, keepdims=True)
        outs.append(jnp.where(first, lo, hi))
    return jnp.concatenate(outs, axis=1)


def _by_head_rows(x):
    first = _first_head()
    return jnp.concatenate([jnp.where(first, x, 0.0), jnp.where(first, 0.0, x)], axis=0)


def _diag_blocks(x):
    return jnp.where(_first_head(), x[0:HEAD_SIZE], x[HEAD_SIZE:PAIR])


def _layer_spec(a, layer):
    if layer is None:
        return pl.BlockSpec(a.shape, lambda i: (0,) * a.ndim, pipeline_mode=pl.Buffered(1))
    return pl.BlockSpec((None,) + a.shape[1:], lambda i: (layer,) + (0,) * (a.ndim - 1),
                        pipeline_mode=pl.Buffered(1))


def _cast_specs(w, layer, n_steps):
    _, k, n = w.shape
    rows = BF16_SUBLANES
    while k % rows or k // rows > n_steps:
        rows += BF16_SUBLANES
    block = lambda s: jnp.minimum(s, k // rows - 1)
    return (pl.BlockSpec((None, rows, n), lambda s: (layer, block(s), 0)),
            pl.BlockSpec((rows, n), lambda s: (block(s), 0)),
            jax.ShapeDtypeStruct((k, n), _BF16))


def _cast_blocks(in_refs, out_refs):
    for in_ref, out_ref in zip(in_refs, out_refs):
        out_ref[...] = in_ref[...].astype(_BF16)
        yield


def _token_specs(n_parts, tile=TOKEN_TILE, tile_of_step=lambda i: i):
    if n_parts == 1:
        return [pl.BlockSpec((tile, D_MODEL), lambda i: (tile_of_step(i), 0))]
    n_prompt = SEQ // tile
    return [pl.BlockSpec((1, tile, D_MODEL), lambda i: (0, jnp.minimum(tile_of_step(i), n_prompt - 1), 0)),
            pl.BlockSpec((tile // DEC_SEQ, DEC_SEQ, D_MODEL),
                         lambda i: (jnp.maximum(tile_of_step(i) - n_prompt, 0), 0, 0))]


def _load_tokens(x_refs, tile=TOKEN_TILE, tile_index=None):
    if len(x_refs) == 1:
        return x_refs[0][...]
    prompt_ref, stream_ref = x_refs
    tile_index = pl.program_id(0) if tile_index is None else tile_index
    return jnp.where(tile_index >= SEQ // tile, stream_ref[...].reshape(tile, D_MODEL), prompt_ref[0])


N_TOKEN_FIELDS = 5


def _token_fields(tok_ref):
    return [tok_ref.at[:, f * D_RWKV:(f + 1) * D_RWKV] for f in range(N_TOKEN_FIELDS)]


def _running_log_decay(lw):
    tri = jnp.where(_iota((CHUNK, CHUNK), 1) <= _iota((CHUNK, CHUNK), 0), 1.0, 0.0).astype(_BF16)
    return _dot_pieces((tri,), _split3(lw))


def _decayed_operands(r, kf, v, lw, cum, kk, a_sig):
    last = cum[CHUNK - 1:CHUNK]
    e_inv = jnp.exp(-cum)
    e_tail = jnp.exp(last - cum)
    b = kk * a_sig
    return dict(a_t=-kk * jnp.exp(cum - lw), r_t=r * jnp.exp(cum), b_t=b * e_inv, k_t=kf * e_inv,
                b_h=b * e_tail, k_h=kf * e_tail, v=v, dec=jnp.exp(last))


def _intra_pairs(chunks):
    first = _first_head()
    first2 = _first_head(2 * PAIR)
    g_row = _iota((2 * CHUNK, 2 * PAIR), 0)
    keep = (_iota((2 * CHUNK, 2 * PAIR), 1) % CHUNK) < (g_row % CHUNK) + g_row // CHUNK
    rc_xor = _iota((CHUNK, PAIR), 0) ^ (_iota((CHUNK, PAIR), 1) % HEAD_SIZE)
    eye = _iota((PAIR, PAIR), 0) == _iota((PAIR, PAIR), 1)
    zeros = jnp.zeros((CHUNK, PAIR), _F32)
    items = [(c, p) for c in range(len(chunks)) for p in range(N_PAIRS)]
    part = lambda it, name: chunks[it[0]][name][:, it[1] * PAIR:(it[1] + 1) * PAIR]
    out_pieces = lambda x: _pieces(x, PIECES_OUT)
    inv_pieces = lambda x: _pieces(x, PIECES_INV)

    gm, l_c, ak_c = {}, {}, {}
    for it in items:
        ar = jnp.concatenate([part(it, "a_t"), part(it, "r_t")], axis=0)
        bk = jnp.concatenate([part(it, "b_t"), part(it, "k_t")], axis=0)
        bk_heads = jnp.concatenate([jnp.where(first, bk, 0.0), jnp.where(first, 0.0, bk)], axis=0)
        gm[it] = jnp.where(keep, _dot_pieces(_pieces(ar, PIECES_A), _pieces(bk_heads, PIECES_A), _NT), 0.0)
        a_h0, a_h1 = gm[it][0:CHUNK, 0:PAIR], gm[it][0:CHUNK, PAIR:2 * PAIR]
        l_c[it] = jnp.where(first, a_h0, pltpu.roll(a_h1, HEAD_SIZE, axis=1))
        ak_c[it] = jnp.where(first, pltpu.roll(a_h0, HEAD_SIZE, axis=1), a_h1)
    yield

    inv = {it: jnp.where(rc_xor == 0, 1.0, jnp.where(rc_xor == 1, l_c[it], 0.0)) for it in items}
    yield
    m = 2
    while m < CHUNK:
        level = jnp.logical_and(rc_xor >= m, rc_xor < 2 * m)
        slabs = m >= SUBLANES
        n_blocks = CHUNK // (2 * m)
        low = (lambda a: jnp.concatenate([a[s:s + m] for s in range(m, CHUNK, 2 * m)], axis=0)) if slabs else (lambda a: a)
        inv_bd = {it: inv_pieces(_by_head_rows(inv[it])) for it in items}
        x = {it: _dot_pieces(inv_pieces(low(jnp.where(level, l_c[it], 0.0))), inv_bd[it]) for it in items}
        yield
        for it in items:
            if slabs:
                zeros_m = jnp.zeros((m, PAIR), _F32)
                x_full = jnp.concatenate([a for j in range(n_blocks)
                                          for a in (zeros_m, x[it][j * m:(j + 1) * m])], axis=0)
                upd = _dot_pieces(inv_pieces(low(inv[it])), inv_pieces(_by_head_rows(x_full)))
                inv[it] = jnp.concatenate(
                    [a for j in range(n_blocks)
                     for a in (inv[it][2 * j * m:(2 * j + 1) * m],
                               inv[it][(2 * j + 1) * m:(2 * j + 2) * m] + upd[j * m:(j + 1) * m])], axis=0)
            else:
                inv[it] = inv[it] + _dot_pieces(inv_pieces(inv[it]), inv_pieces(_by_head_rows(x[it])))
        yield
        m *= 2

    akv = {}
    for it in items:
        akv[it] = _dot_pieces(out_pieces(ak_c[it]), out_pieces(_by_head_rows(part(it, "v"))))
    yield
    uva_pieces = {}
    for it in items:
        u0a = _dot_pieces(out_pieces(inv[it]), out_pieces(jnp.concatenate(
            [_by_head_rows(akv[it]), _by_head_rows(part(it, "a_t"))], axis=1)))
        uva = jnp.concatenate([u0a, jnp.concatenate([part(it, "v"), zeros], axis=1)], axis=0)
        uva_pieces[it] = out_pieces(uva)
    yield
    rps, y0s, mts, n0s = {}, {}, {}, {}
    for it in items:
        rr = jnp.concatenate([gm[it][CHUNK:2 * CHUNK, 0:PAIR], gm[it][CHUNK:2 * CHUNK, PAIR:2 * PAIR]],
                             axis=0)
        yr2 = _dot_pieces(out_pieces(rr), uva_pieces[it])
        yr = jnp.where(first2, yr2[0:CHUNK], yr2[CHUNK:2 * CHUNK])
        y0s[it] = yr[:, 0:PAIR]
        rps[it] = part(it, "r_t") + yr[:, PAIR:2 * PAIR]
    yield
    for it in items:
        bk_h = jnp.concatenate([part(it, "b_h"), part(it, "k_h")], axis=0)
        nm = _dot_pieces(out_pieces(bk_h), uva_pieces[it], _TN)
        n0s[it] = _diag_blocks(nm[:, 0:PAIR])
        mts[it] = _diag_blocks(nm[:, PAIR:2 * PAIR] + jnp.where(eye, part(it, "dec"), 0.0))
    cat = lambda d, c: jnp.concatenate([d[(c, p)] for p in range(N_PAIRS)], axis=1)
    return ([cat(rps, c) for c in range(len(chunks))], [cat(y0s, c) for c in range(len(chunks))], mts, n0s)


N_FRONT_PARAMS = 12
N_FRONT_OUTPUTS = 5


def _front_kernel(n_x, n_cast, *refs):
    x_refs, refs = refs[:n_x], refs[n_x:]
    (gpre_ref, w_ref, conv0_ref, shift0_ref, mu_ref, cw_ref, lora_w_ref,
     wd0_ref, a0_ref, kk_w_ref, ka_w_ref, rk_w_ref) = refs[:N_FRONT_PARAMS]
    cast_in_refs, refs = refs[N_FRONT_PARAMS:N_FRONT_PARAMS + n_cast], refs[N_FRONT_PARAMS + n_cast:]
    gate_ref, tok_ref, mn_ref, conv_out_ref, shift_out_ref = refs[:N_FRONT_OUTPUTS]
    cast_out_refs = refs[N_FRONT_OUTPUTS:N_FRONT_OUTPUTS + n_cast]
    mix_next, mix_cur, hist_u_scr, hist_pr_scr = refs[N_FRONT_OUTPUTS + n_cast:]
    bgy_ref, rp_ref, y0_ref, g_ref, bg_ref = _token_fields(tok_ref)
    mt_ref, n0_ref = mn_ref.at[:, 0], mn_ref.at[:, 1]
    s = pl.program_id(0)
    is_sample = s - 1 >= N_PROMPT_CHUNKS // INTRA_CHUNKS
    row = _iota((CHUNK, 1), 0)
    cw = cw_ref[...]
    lane = _iota((CHUNK, D_LORA), 1)

    @pl.when(s == 0)
    def _():
        mix_cur[...] = jnp.zeros_like(mix_cur)
        hist_u_scr[...] = jnp.zeros_like(hist_u_scr)
        hist_pr_scr[...] = jnp.zeros_like(hist_pr_scr)

    def project():
        x = _load_tokens(x_refs, FRONT_TILE, jnp.minimum(s, N_TOK // FRONT_TILE - 1))
        h = (_rms_scale(x) * gpre_ref[...]).astype(_BF16)
        yield
        for j in range((MIX_COLS + GATE_COLS) // PROJ_SLAB):
            cols = slice(j * PROJ_SLAB, (j + 1) * PROJ_SLAB)
            p = _dot(h, w_ref[:, cols])
            if j < MIX_COLS // PROJ_SLAB:
                mix_next[:, cols] = p
            else:
                gate_ref[:, j * PROJ_SLAB - MIX_COLS:(j + 1) * PROJ_SLAB - MIX_COLS] = p
            yield

    hist = dict(u=hist_u_scr[...], pr=hist_pr_scr[...])

    def prepare(c):
        rows = slice(c * CHUNK, (c + 1) * CHUNK)
        u_hist = jnp.where(is_sample, conv0_ref[c], hist["u"])
        pr_hist = jnp.where(is_sample, shift0_ref[c], hist["pr"])

        xin = mix_cur[rows, 0:D_CONV]
        bg = mix_cur[rows, D_CONV:2 * D_CONV]
        cg = mix_cur[rows, 2 * D_CONV:3 * D_CONV]
        u = cg * xin
        u1 = jnp.where(row == 0, u_hist[1:2], pltpu.roll(u, 1, axis=0))
        u2 = jnp.where(row == 0, u_hist[0:1], jnp.where(row == 1, u_hist[1:2], pltpu.roll(u, 2, axis=0)))
        bgy_ref[rows, :] = bg * (cw[0:1] * u2 + cw[1:2] * u1 + cw[2:3] * u)
        hist["u"] = u[CHUNK - 2:CHUNK]
        conv_out_ref[c] = hist["u"]
        yield

        pr = mix_cur[rows, 3 * D_CONV:MIX_COLS]
        shifted = jnp.where(row == 0, pr_hist, pltpu.roll(pr, 1, axis=0))
        ps = pr + (shifted - pr) * mu_ref[...]
        hist["pr"] = pr[CHUNK - 1:CHUNK]
        shift_out_ref[c] = hist["pr"]
        r = ps[:, 0:D_RWKV]
        k = ps[:, D_RWKV:2 * D_RWKV]
        v = ps[:, 2 * D_RWKV:3 * D_RWKV]
        yield

        lora_in = ps[:, 3 * D_RWKV:RWKV_COLS]
        act = jnp.where(lane < D_DECAY_LORA, jnp.tanh(lora_in),
                        jnp.where(lane < D_DECAY_LORA + D_AAA_LORA, lora_in, jax.nn.sigmoid(lora_in)))
        lora = _dot(act.astype(_BF16), lora_w_ref[...])
        yield

        w_raw = wd0_ref[...] + lora[:, 0:D_RWKV]
        w_raw = -jax.nn.softplus(-w_raw) - 0.5
        lw = -jnp.exp(w_raw)
        cum = _running_log_decay(lw)
        yield

        a_sig = jax.nn.sigmoid(a0_ref[...] + lora[:, D_RWKV:2 * D_RWKV])
        g = lora[:, 2 * D_RWKV:3 * D_RWKV]
        g_ref[rows, :] = g
        kk = k * kk_w_ref[...]
        yield
        kk = kk / jnp.maximum(jnp.sqrt(_head_sums(kk * kk)), 1e-12)
        yield
        kf = k * (1.0 + (a_sig - 1.0) * ka_w_ref[...])
        bg_ref[rows, :] = _head_sums(r * kf * rk_w_ref[...]) * v * g
        yield
        return _decayed_operands(r, kf, v, lw, cum, kk, a_sig)

    def prepare_group(first_chunk):
        ops = []
        for c in range(first_chunk, first_chunk + INTRA_GROUP):
            ops.append((yield from prepare(c)))
        return ops

    def store(first_chunk, results):
        rps, y0s, mts, n0s = results
        for g in range(INTRA_GROUP):
            c = first_chunk + g
            rows = slice(c * CHUNK, (c + 1) * CHUNK)
            rp_ref[rows, :] = rps[g]
            y0_ref[rows, :] = y0s[g]
            for p in range(N_PAIRS):
                mt_ref[c, p] = mts[(g, p)]
                n0_ref[c, p] = n0s[(g, p)]

    def hand_over():
        for j in range(MIX_COLS // PROJ_SLAB):
            cols = slice(j * PROJ_SLAB, (j + 1) * PROJ_SLAB)
            mix_cur[:, cols] = mix_next[:, cols]
            yield

    assert INTRA_CHUNKS == INTRA_GROUP
    n_mix_steps = 1 + MIX_COLS // PROJ_SLAB
    n_prepare_steps = 7 * INTRA_GROUP
    assert n_prepare_steps % n_mix_steps == 0
    projection = _Steps(project())
    operands = _zip_steps(_Steps(prepare_group(0)), (projection, 1, n_prepare_steps // n_mix_steps))
    hist_u_scr[...] = hist["u"]
    hist_pr_scr[...] = hist["pr"]
    copy = _Steps(hand_over())
    casts = _Steps(_cast_blocks(cast_in_refs, cast_out_refs))
    results = _zip_steps(_Steps(_intra_pairs(operands)), (projection, 1, 1), (copy, 1, 1), (casts, 1, 2))
    projection.finish()
    copy.finish()
    casts.finish()
    store(0, results)


def _front(x_parts, layer, norm_pre, w_in_bf16, conv0, shift0, mu, cw, lora_w, wd0, a0, kk_w, ka_w, rk_w,
           weights_to_cast):
    n_tiles = N_TOK // FRONT_TILE
    cast_in, cast_out, cast_shape = zip(*[_cast_specs(w, layer, n_tiles + 1) for w in weights_to_cast])
    n_prompt_tiles = SEQ // FRONT_TILE
    projected = lambda s: jnp.minimum(s, n_tiles - 1)
    prepared = lambda s: jnp.maximum(s - 1, 0)
    tok = lambda width: pl.BlockSpec((FRONT_TILE, width), lambda s: (prepared(s), 0))
    full = lambda a: _layer_spec(a, layer)
    seq = lambda a: pl.BlockSpec((None, INTRA_CHUNKS) + a.shape[2:],
                                 lambda s: (layer, jnp.maximum(prepared(s) - n_prompt_tiles, 0), 0, 0))
    per_chunk = lambda *dims: pl.BlockSpec((INTRA_CHUNKS,) + dims, lambda s: (prepared(s),) + (0,) * len(dims))
    return pl.pallas_call(
        functools.partial(_front_kernel, len(x_parts), len(weights_to_cast)),
        grid=(n_tiles + 1,),
        in_specs=(_token_specs(len(x_parts), FRONT_TILE, projected)
                  + [full(norm_pre), _layer_spec(w_in_bf16, None), seq(conv0), seq(shift0), full(mu), full(cw),
                     full(lora_w), full(wd0), full(a0), full(kk_w), full(ka_w), full(rk_w)] + list(cast_in)),
        out_specs=[pl.BlockSpec((FRONT_TILE, GATE_COLS), lambda s: (projected(s), 0)),
                   tok(N_TOKEN_FIELDS * D_RWKV), per_chunk(2, N_PAIRS, HEAD_SIZE, PAIR),
                   per_chunk(2, D_CONV), per_chunk(1, RWKV_COLS)] + list(cast_out),
        out_shape=[jax.ShapeDtypeStruct((N_TOK, GATE_COLS), _F32),
                   jax.ShapeDtypeStruct((N_TOK, N_TOKEN_FIELDS * D_RWKV), _F32),
                   jax.ShapeDtypeStruct((N_CHUNKS, 2, N_PAIRS, HEAD_SIZE, PAIR), _F32),
                   jax.ShapeDtypeStruct((N_CHUNKS, 2, D_CONV), _F32),
                   jax.ShapeDtypeStruct((N_CHUNKS, 1, RWKV_COLS), _F32)] + list(cast_shape),
        scratch_shapes=[pltpu.VMEM((FRONT_TILE, MIX_COLS), _F32), pltpu.VMEM((FRONT_TILE, MIX_COLS), _F32),
                        pltpu.VMEM((2, D_CONV), _F32), pltpu.VMEM((1, RWKV_COLS), _F32)],
        compiler_params=pltpu.CompilerParams(
            dimension_semantics=("arbitrary",), vmem_limit_bytes=VMEM_LIMIT_BYTES),
        name="front",
    )(*x_parts, norm_pre, w_in_bf16, conv0, shift0, mu, cw, lora_w, wd0, a0, kk_w, ka_w, rk_w, *weights_to_cast)


N_POST_INPUTS = 14


def _post_kernel(n_x, n_out, n_cast, *refs):
    x_refs, refs = refs[:n_x], refs[n_x:]
    (tok_ref, gate_ref, mn_ref, s0_ref, lnw_ref, lnb_ref,
     wc_ref, wr_ref, wo_ref, gmix_ref, gpre_ref, wup_ref, wdown_ref, gpost_ref) = refs[:N_POST_INPUTS]
    cast_in_refs, refs = refs[N_POST_INPUTS:N_POST_INPUTS + n_cast], refs[N_POST_INPUTS + n_cast:]
    out_refs, (s_prompt_ref, s_sample_ref) = refs[:n_out], refs[n_out:n_out + 2]
    cast_out_refs, (state,) = refs[n_out + 2:n_out + 2 + n_cast], refs[n_out + 2 + n_cast:]
    _Steps(_cast_blocks(cast_in_refs, cast_out_refs)).finish()
    bgy_ref, rp_ref, y0_ref, g_ref, bg_ref = _token_fields(tok_ref)
    mt_ref, n0_ref = mn_ref.at[:, 0], mn_ref.at[:, 1]
    i = pl.program_id(0)
    is_stream = i >= N_PROMPT_TILES

    @pl.when(i == 0)
    def _():
        state[...] = jnp.zeros_like(state)

    eye = (_iota((PAIR, PAIR), 0) == _iota((PAIR, PAIR), 1)).astype(_BF16)
    chunk_rows = lambda c: slice(c * CHUNK, (c + 1) * CHUNK)
    pair_lanes = lambda p: slice(p * PAIR, (p + 1) * PAIR)

    def group_norm(ys):
        y = jnp.concatenate(ys, axis=1)
        inv_n = 1.0 / HEAD_SIZE
        d = y - _head_sums(y) * inv_n
        var = _head_sums(d * d) * inv_n
        return d * lax.rsqrt(var + LN_X_EPS) * lnw_ref[...] + lnb_ref[...]

    def heads_of(s):
        return s[0:HEAD_SIZE, 0:HEAD_SIZE], s[HEAD_SIZE:PAIR, HEAD_SIZE:PAIR]

    def prompt_steps():
        ws = [state[p] for p in range(N_PAIRS)]
        yns = []
        for c in range(TILE_CHUNKS):
            ys = []
            for p in range(N_PAIRS):
                w_pieces = _split2(ws[p])
                ys.append(_dot(rp_ref[chunk_rows(c), pair_lanes(p)].astype(_BF16), w_pieces[0])
                          + y0_ref[chunk_rows(c), pair_lanes(p)])
                ws[p] = _by_head_rows(_dot_pieces(_split2(mt_ref[c, p]), w_pieces) + n0_ref[c, p])
            yield
            yns.append(group_norm(ys))
            yield
        for p in range(N_PAIRS):
            state[p] = ws[p]
            s_prompt_ref[2 * p], s_prompt_ref[2 * p + 1] = heads_of(_dot_pieces((eye,), _split3(ws[p]), _NT))
        return jnp.concatenate(yns, axis=0)

    def stream_steps():
        zeros = jnp.zeros((HEAD_SIZE, HEAD_SIZE), _F32)
        yns = []
        for c in range(TILE_CHUNKS):
            ys = []
            for p in range(N_PAIRS):
                s0 = jnp.concatenate([jnp.concatenate([s0_ref[c, 2 * p], zeros], axis=1),
                                      jnp.concatenate([zeros, s0_ref[c, 2 * p + 1]], axis=1)], axis=0)
                s0_hi, s0_lo = _split2(s0)
                ys.append(_dot(rp_ref[chunk_rows(c), pair_lanes(p)].astype(_BF16), s0_hi, _NT)
                          + y0_ref[chunk_rows(c), pair_lanes(p)])
                mn = jnp.concatenate([_by_head_rows(mt_ref[c, p]), _by_head_rows(n0_ref[c, p])], axis=1)
                s_next = _dot_pieces((jnp.concatenate([s0_hi, eye], axis=1),
                                      jnp.concatenate([s0_lo, jnp.zeros_like(eye)], axis=1)),
                                     _split2(mn), _NT)
                s_sample_ref[c, 2 * p], s_sample_ref[c, 2 * p + 1] = heads_of(s_next)
            yield
            yns.append(group_norm(ys))
            yield
        return jnp.concatenate(yns, axis=0)

    def gate_steps():
        branch_a = _dot(bgy_ref[...].astype(_BF16), wc_ref[...])
        yield
        gates = []
        for c in range(TILE_CHUNKS):
            gates.append((jax.nn.sigmoid(gate_ref[chunk_rows(c), 0:D_MODEL]),
                          jax.nn.sigmoid(gate_ref[chunk_rows(c), D_MODEL:GATE_COLS])))
            yield
        gate_a = jnp.concatenate([ga for ga, _ in gates], axis=0)
        gate_b = jnp.concatenate([gb for _, gb in gates], axis=0)
        return branch_a, gate_a, gate_b

    def tile_output(recurrence_steps):
        gate_work = _Steps(gate_steps())
        yn = _zip_steps(_Steps(recurrence_steps), (gate_work, 1, 1))
        branch_a, gate_a, gate_b = gate_work.finish()
        z = yn * g_ref[...] + bg_ref[...]
        branch_b = _dot(z.astype(_BF16), wr_ref[...])
        merged = gate_a * branch_a + gate_b * branch_b
        m = _dot(merged.astype(_BF16), wo_ref[...])
        x = _load_tokens(x_refs) + _rms_scale(m) * gmix_ref[...]
        h = (_rms_scale(x) * gpre_ref[...]).astype(_BF16)
        up = _dot(h, wup_ref[...])
        act = jax.nn.silu(up[:, :D_FF]) * up[:, D_FF:]
        f = _dot(act.astype(_BF16), wdown_ref[...])
        return x + _rms_scale(f) * gpost_ref[...]

    @pl.when(jnp.logical_not(is_stream))
    def _():
        out = tile_output(prompt_steps())
        if len(out_refs) == 1:
            out_refs[0][...] = out
        else:
            out_refs[0][0] = out

    @pl.when(is_stream)
    def _():
        out = tile_output(stream_steps())
        if len(out_refs) == 1:
            out_refs[0][...] = out
        else:
            out_refs[1][...] = out.reshape(STREAMS_PER_TILE, DEC_SEQ, D_MODEL)


def _post(x_parts, layer, token_inputs, mn, state_wkv, params, weights_to_cast, split_out):
    n_steps = N_TOK // TOKEN_TILE
    cast = [_cast_specs(w, layer + 1, n_steps) for w in weights_to_cast]
    param_spec = lambda a: _layer_spec(a, layer if a.ndim == 3 else None)
    tok = lambda a: pl.BlockSpec((TOKEN_TILE, a.shape[1]), lambda i: (i, 0))
    blk = pl.BlockSpec((TILE_CHUNKS, 2, N_PAIRS, HEAD_SIZE, PAIR), lambda i: (i, 0, 0, 0, 0))
    heads = (N_HEADS, HEAD_SIZE, HEAD_SIZE)
    stream_block = lambda i: jnp.maximum(i - N_PROMPT_TILES, 0)
    st_in = pl.BlockSpec((None, STREAMS_PER_TILE) + heads, lambda i: (layer, stream_block(i), 0, 0, 0))
    st_out = pl.BlockSpec((STREAMS_PER_TILE,) + heads, lambda i: (stream_block(i), 0, 0, 0))
    if split_out:
        out_specs = _token_specs(2)
        out_shape = [jax.ShapeDtypeStruct((1, SEQ, D_MODEL), _F32),
                     jax.ShapeDtypeStruct((DEC_BATCH, DEC_SEQ, D_MODEL), _F32)]
    else:
        out_specs = _token_specs(1)
        out_shape = [jax.ShapeDtypeStruct((N_TOK, D_MODEL), _F32)]
    outs = pl.pallas_call(
        functools.partial(_post_kernel, len(x_parts), len(out_shape), len(cast)),
        grid=(n_steps,),
        in_specs=(_token_specs(len(x_parts)) + [tok(a) for a in token_inputs] + [blk, st_in]
                  + [param_spec(a) for a in params] + [c[0] for c in cast]),
        out_specs=out_specs + [pl.BlockSpec(heads, lambda i: (0, 0, 0)), st_out] + [c[1] for c in cast],
        out_shape=out_shape + [jax.ShapeDtypeStruct(heads, _F32),
                               jax.ShapeDtypeStruct((DEC_BATCH,) + heads, _F32)] + [c[2] for c in cast],
        scratch_shapes=[pltpu.VMEM((N_PAIRS, PAIR, PAIR), _F32)],
        compiler_params=pltpu.CompilerParams(
            dimension_semantics=("arbitrary",), vmem_limit_bytes=VMEM_LIMIT_BYTES),
        name="post",
    )(*x_parts, *token_inputs, mn, state_wkv, *params, *weights_to_cast)
    n_x_out = len(out_shape)
    return outs[:n_x_out], outs[n_x_out], outs[n_x_out + 1], outs[n_x_out + 2:]


def _lora_weight(w_decay2, a2, g2):
    w = jnp.zeros((DEPTH, D_LORA, 3 * D_RWKV), _BF16)
    w = w.at[:, 0:D_DECAY_LORA, 0:D_RWKV].set(w_decay2.astype(_BF16))
    w = w.at[:, D_DECAY_LORA:D_DECAY_LORA + D_AAA_LORA, D_RWKV:2 * D_RWKV].set(a2.astype(_BF16))
    return w.at[:, D_DECAY_LORA + D_AAA_LORA:, 2 * D_RWKV:].set(g2.astype(_BF16))


def kernel(x_prompt, x_sample, state_conv, state_shift, state_wkv, norm_mix_pre, norm_mix_post, w_in, mu_shift, conv_w, w_decay0, w_decay2, a0, a2, g2, k_k, k_a, r_k, ln_x_w, ln_x_b, w_conv_out, w_rwkv_out, w_o, norm_ffn_pre, norm_ffn_post, w_ffn_up, w_ffn_down):
    x_parts = (x_prompt, x_sample)
    rows = lambda a: a.reshape(DEPTH, 1, -1)
    lora_w = _lora_weight(w_decay2, a2, g2)
    intra_params = (rows(mu_shift), conv_w, lora_w, rows(w_decay0), rows(a0), rows(k_k), rows(k_a), rows(r_k))
    norm_pre = rows(norm_mix_pre)
    shift0 = state_shift[:, :, None]
    w_in_b = w_in[0].astype(_BF16)
    conv_p, shift_p, wkv_p, conv_s, shift_s, wkv_s = [], [], [], [], [], []
    for l in range(DEPTH):
        gates, fields, mn, conv_out, shift_out, wc, wr, wo, wup, wdown = _front(
            x_parts, l, norm_pre, w_in_b, state_conv, shift0, *intra_params,
            weights_to_cast=(w_conv_out, w_rwkv_out, w_o, w_ffn_up, w_ffn_down))
        post_params = (rows(ln_x_w), rows(ln_x_b), wc, wr, wo, rows(norm_mix_post), rows(norm_ffn_pre),
                       wup, wdown, rows(norm_ffn_post))
        last_layer = l == DEPTH - 1
        x_parts, s_prompt, s_sample, next_w_in = _post(
            x_parts, l, (fields, gates), mn, state_wkv, post_params,
            weights_to_cast=() if last_layer else (w_in,), split_out=last_layer)
        if not last_layer:
            w_in_b, = next_w_in

        last = N_PROMPT_CHUNKS - 1
        conv_p.append(conv_out[last:last + 1])
        conv_s.append(conv_out[N_PROMPT_CHUNKS:])
        shift_p.append(shift_out[last:last + 1, 0])
        shift_s.append(shift_out[N_PROMPT_CHUNKS:, 0])
        wkv_p.append(s_prompt[None])
        wkv_s.append(s_sample)
    y_prompt, y_sample = x_parts
    return (y_prompt, y_sample,
            jnp.stack(conv_p, 0), jnp.stack(shift_p, 0), jnp.stack(wkv_p, 0),
            jnp.stack(conv_s, 0), jnp.stack(shift_s, 0), jnp.stack(wkv_s, 0))
```

```python
import functools

import jax
import jax.numpy as jnp
from jax import lax
from jax.experimental import pallas as pl
from jax.experimental.pallas import tpu as pltpu

D_MODEL = 1024
SEQ = 16384
DEPTH = 2
DEC_BATCH = 32
DEC_SEQ = 64
D_CONV = 512
D_RWKV = 512
HEAD_SIZE = 64
N_HEADS = 8
D_DECAY_LORA = 64
D_AAA_LORA = 64
D_GATE_LORA = 128
D_LORA = D_DECAY_LORA + D_AAA_LORA + D_GATE_LORA
LN_X_EPS = 64e-5
D_FF = 2816
RMS_EPS = 1e-6
RWKV_COLS = 3 * D_RWKV + D_LORA
MIX_COLS = 3 * D_CONV + RWKV_COLS
GATE_COLS = 2 * D_MODEL

CHUNK = 64
N_TOK = SEQ + DEC_BATCH * DEC_SEQ
N_CHUNKS = N_TOK // CHUNK
N_PROMPT_CHUNKS = SEQ // CHUNK
PAIR = 2 * HEAD_SIZE
N_PAIRS = N_HEADS // 2
TOKEN_TILE = 256
N_PROMPT_TILES = SEQ // TOKEN_TILE
STREAMS_PER_TILE = TOKEN_TILE // DEC_SEQ
TILE_CHUNKS = TOKEN_TILE // CHUNK
INTRA_CHUNKS = 4
INTRA_GROUP = 4
FRONT_TILE = INTRA_CHUNKS * CHUNK
PROJ_SLAB = 256
SUBLANES = 8
BF16_SUBLANES = 16
PIECES_A = 1
PIECES_INV = 1
PIECES_OUT = 1
VMEM_LIMIT_BYTES = 56 * 1024 * 1024

_BF16 = jnp.bfloat16
_F32 = jnp.float32
_NN = (((1,), (0,)), ((), ()))
_NT = (((1,), (1,)), ((), ()))
_TN = (((0,), (0,)), ((), ()))


def _dot(a, b, dims=_NN):
    return lax.dot_general(a, b, dims, preferred_element_type=_F32)


def _pieces(x, n):
    out = []
    for _ in range(n - 1):
        hi = x.astype(_BF16)
        out.append(hi)
        x = x - hi.astype(_F32)
    out.append(x.astype(_BF16))
    return tuple(out)


def _split2(x):
    return _pieces(x, 2)


def _split3(x):
    return _pieces(x, 3)


def _dot_pieces(a, b, dims=_NN):
    n = max(len(a), len(b))
    terms = [_dot(a[i], b[j], dims) for i in range(len(a)) for j in range(len(b)) if i + j < n]
    total = terms[0]
    if len(terms) > 1:
        rest = terms[1]
        for t in terms[2:]:
            rest = rest + t
        total = total + rest
    return total


def _dot_f32(a, b, dims=_NN):
    return _dot_pieces(_split2(a), _split2(b), dims)


class _Steps:
    def __init__(self, gen):
        self.gen, self.done, self.value = gen, False, None

    def step(self):
        if not self.done:
            try:
                next(self.gen)
            except StopIteration as stop:
                self.done, self.value = True, stop.value

    def finish(self):
        while not self.done:
            self.step()
        return self.value


def _zip_steps(main, *sides):
    count = 0
    while not main.done:
        main.step()
        count += 1
        for side, n, every in sides:
            if count % every == 0:
                for _ in range(n):
                    side.step()
    return main.value


def _rms_scale(x):
    return x * lax.rsqrt(jnp.mean(x * x, axis=-1, keepdims=True) + RMS_EPS)


def _iota(shape, dim):
    return lax.broadcasted_iota(jnp.int32, shape, dim)


def _first_head(width=PAIR):
    return _iota((1, width), 1) % PAIR < HEAD_SIZE


def _head_sums(x):
    first = _first_head()
    outs = []
    for p in range(x.shape[1] // PAIR):
        xp = x[:, p * PAIR:(p + 1) * PAIR]
        lo = jnp.sum(jnp.where(first, xp, 0.0), axis=1, keepdims=True)
        hi = jnp.sum(jnp.where(first, 0.0, xp), axis=1, keepdims=True)
        outs.append(jnp.where(first, lo, hi))
    return jnp.concatenate(outs, axis=1)


def _by_head_rows(x):
    first = _first_head()
    return jnp.concatenate([jnp.where(first, x, 0.0), jnp.where(first, 0.0, x)], axis=0)


def _diag_blocks(x):
    return jnp.where(_first_head(), x[0:HEAD_SIZE], x[HEAD_SIZE:PAIR])


def _layer_spec(a, layer):
    if layer is None:
        return pl.BlockSpec(a.shape, lambda i: (0,) * a.ndim, pipeline_mode=pl.Buffered(1))
    return pl.BlockSpec((None,) + a.shape[1:], lambda i: (layer,) + (0,) * (a.ndim - 1),
                        pipeline_mode=pl.Buffered(1))


def _cast_specs(w, layer, n_steps):
    _, k, n = w.shape
    rows = BF16_SUBLANES
    while k % rows or k // rows > n_steps:
        rows += BF16_SUBLANES
    block = lambda s: jnp.minimum(s, k // rows - 1)
    return (pl.BlockSpec((None, rows, n), lambda s: (layer, block(s), 0)),
            pl.BlockSpec((rows, n), lambda s: (block(s), 0)),
            jax.ShapeDtypeStruct((k, n), _BF16))


def _cast_blocks(in_refs, out_refs):
    for in_ref, out_ref in zip(in_refs, out_refs):
        out_ref[...] = in_ref[...].astype(_BF16)
        yield


def _token_specs(n_parts, tile=TOKEN_TILE, tile_of_step=lambda i: i):
    if n_parts == 1:
        return [pl.BlockSpec((tile, D_MODEL), lambda i: (tile_of_step(i), 0))]
    n_prompt = SEQ // tile
    return [pl.BlockSpec((1, tile, D_MODEL), lambda i: (0, jnp.minimum(tile_of_step(i), n_prompt - 1), 0)),
            pl.BlockSpec((tile // DEC_SEQ, DEC_SEQ, D_MODEL),
                         lambda i: (jnp.maximum(tile_of_step(i) - n_prompt, 0), 0, 0))]


def _load_tokens(x_refs, tile=TOKEN_TILE, tile_index=None):
    if len(x_refs) == 1:
        return x_refs[0][...]
    prompt_ref, stream_ref = x_refs
    tile_index = pl.program_id(0) if tile_index is None else tile_index
    return jnp.where(tile_index >= SEQ // tile, stream_ref[...].reshape(tile, D_MODEL), prompt_ref[0])


VECTOR_FIELDS = (("mu_shift", RWKV_COLS), ("w_decay0", D_RWKV), ("a0", D_RWKV), ("k_k", D_RWKV), ("k_a", D_RWKV),
                 ("r_k", D_RWKV), ("norm_mix_pre", D_MODEL), ("ln_x_w", D_RWKV), ("ln_x_b", D_RWKV),
                 ("norm_mix_post", D_MODEL), ("norm_ffn_pre", D_MODEL), ("norm_ffn_post", D_MODEL))


def _vector_views(vec_ref):
    views, start = {}, 0
    for name, width in VECTOR_FIELDS:
        views[name] = vec_ref.at[:, start:start + width]
        start += width
    return views


N_TOKEN_FIELDS = 5


def _token_fields(tok_ref):
    return [tok_ref.at[:, f * D_RWKV:(f + 1) * D_RWKV] for f in range(N_TOKEN_FIELDS)]


def _running_log_decay(lw):
    tri = jnp.where(_iota((CHUNK, CHUNK), 1) <= _iota((CHUNK, CHUNK), 0), 1.0, 0.0).astype(_BF16)
    return _dot_pieces((tri,), _split3(lw))


def _decayed_operands(r, kf, v, lw, cum, kk, a_sig):
    last = cum[CHUNK - 1:CHUNK]
    e_inv = jnp.exp(-cum)
    e_tail = jnp.exp(last - cum)
    b = kk * a_sig
    return dict(a_t=-kk * jnp.exp(cum - lw), r_t=r * jnp.exp(cum), b_t=b * e_inv, k_t=kf * e_inv,
                b_h=b * e_tail, k_h=kf * e_tail, v=v, dec=jnp.exp(last))


def _intra_pairs(chunks):
    first = _first_head()
    first2 = _first_head(2 * PAIR)
    g_row = _iota((2 * CHUNK, 2 * PAIR), 0)
    keep = (_iota((2 * CHUNK, 2 * PAIR), 1) % CHUNK) < (g_row % CHUNK) + g_row // CHUNK
    rc_xor = _iota((CHUNK, PAIR), 0) ^ (_iota((CHUNK, PAIR), 1) % HEAD_SIZE)
    eye = _iota((PAIR, PAIR), 0) == _iota((PAIR, PAIR), 1)
    zeros = jnp.zeros((CHUNK, PAIR), _F32)
    items = [(c, p) for c in range(len(chunks)) for p in range(N_PAIRS)]
    part = lambda it, name: chunks[it[0]][name][:, it[1] * PAIR:(it[1] + 1) * PAIR]
    out_pieces = lambda x: _pieces(x, PIECES_OUT)
    inv_pieces = lambda x: _pieces(x, PIECES_INV)

    gm, l_c, ak_c = {}, {}, {}
    for it in items:
        ar = jnp.concatenate([part(it, "a_t"), part(it, "r_t")], axis=0)
        bk = jnp.concatenate([part(it, "b_t"), part(it, "k_t")], axis=0)
        bk_heads = jnp.concatenate([jnp.where(first, bk, 0.0), jnp.where(first, 0.0, bk)], axis=0)
        gm[it] = jnp.where(keep, _dot_pieces(_pieces(ar, PIECES_A), _pieces(bk_heads, PIECES_A), _NT), 0.0)
        a_h0, a_h1 = gm[it][0:CHUNK, 0:PAIR], gm[it][0:CHUNK, PAIR:2 * PAIR]
        l_c[it] = jnp.where(first, a_h0, pltpu.roll(a_h1, HEAD_SIZE, axis=1))
        ak_c[it] = jnp.where(first, pltpu.roll(a_h0, HEAD_SIZE, axis=1), a_h1)
    yield

    inv = {it: jnp.where(rc_xor == 0, 1.0, jnp.where(rc_xor == 1, l_c[it], 0.0)) for it in items}
    akv = {it: _dot_pieces(out_pieces(ak_c[it]), out_pieces(_by_head_rows(part(it, "v")))) for it in items}
    yield
    m = 2
    while m < CHUNK:
        level = jnp.logical_and(rc_xor >= m, rc_xor < 2 * m)
        slabs = m >= SUBLANES
        n_blocks = CHUNK // (2 * m)
        low = (lambda a: jnp.concatenate([a[s:s + m] for s in range(m, CHUNK, 2 * m)], axis=0)) if slabs else (lambda a: a)
        inv_bd = {it: inv_pieces(_by_head_rows(inv[it])) for it in items}
        x = {it: _dot_pieces(inv_pieces(low(jnp.where(level, l_c[it], 0.0))), inv_bd[it]) for it in items}
        yield
        for it in items:
            if slabs:
                zeros_m = jnp.zeros((m, PAIR), _F32)
                x_full = jnp.concatenate([a for j in range(n_blocks)
                                          for a in (zeros_m, x[it][j * m:(j + 1) * m])], axis=0)
                upd = _dot_pieces(inv_pieces(low(inv[it])), inv_pieces(_by_head_rows(x_full)))
                inv[it] = jnp.concatenate(
                    [a for j in range(n_blocks)
                     for a in (inv[it][2 * j * m:(2 * j + 1) * m],
                               inv[it][(2 * j + 1) * m:(2 * j + 2) * m] + upd[j * m:(j + 1) * m])], axis=0)
            else:
                inv[it] = inv[it] + _dot_pieces(inv_pieces(inv[it]), inv_pieces(_by_head_rows(x[it])))
        yield
        m *= 2

    uva_pieces = {}
    for it in items:
        u0a = _dot_pieces(out_pieces(inv[it]), out_pieces(jnp.concatenate(
            [_by_head_rows(akv[it]), _by_head_rows(part(it, "a_t"))], axis=1)))
        uva = jnp.concatenate([u0a, jnp.concatenate([part(it, "v"), zeros], axis=1)], axis=0)
        uva_pieces[it] = out_pieces(uva)
    yield
    rps, y0s, mts, n0s = {}, {}, {}, {}
    for it in items:
        rr = jnp.concatenate([gm[it][CHUNK:2 * CHUNK, 0:PAIR], gm[it][CHUNK:2 * CHUNK, PAIR:2 * PAIR]],
                             axis=0)
        yr2 = _dot_pieces(out_pieces(rr), uva_pieces[it])
        yr = jnp.where(first2, yr2[0:CHUNK], yr2[CHUNK:2 * CHUNK])
        y0s[it] = yr[:, 0:PAIR]
        rps[it] = part(it, "r_t") + yr[:, PAIR:2 * PAIR]
    for it in items:
        bk_h = jnp.concatenate([part(it, "b_h"), part(it, "k_h")], axis=0)
        nm = _dot_pieces(out_pieces(bk_h), uva_pieces[it], _TN)
        n0s[it] = _diag_blocks(nm[:, 0:PAIR])
        mts[it] = _diag_blocks(nm[:, PAIR:2 * PAIR] + jnp.where(eye, part(it, "dec"), 0.0))
    cat = lambda d, c: jnp.concatenate([d[(c, p)] for p in range(N_PAIRS)], axis=1)
    return ([cat(rps, c) for c in range(len(chunks))], [cat(y0s, c) for c in range(len(chunks))], mts, n0s)


N_FRONT_PARAMS = 6
N_FRONT_OUTPUTS = 5


def _front_kernel(n_x, n_cast, *refs):
    x_refs, refs = refs[:n_x], refs[n_x:]
    vec_ref, w_ref, conv0_ref, shift0_ref, cw_ref, lora_w_ref = refs[:N_FRONT_PARAMS]
    vec = _vector_views(vec_ref)
    gpre_ref, mu_ref, wd0_ref, a0_ref = vec["norm_mix_pre"], vec["mu_shift"], vec["w_decay0"], vec["a0"]
    kk_w_ref, ka_w_ref, rk_w_ref = vec["k_k"], vec["k_a"], vec["r_k"]
    cast_in_refs, refs = refs[N_FRONT_PARAMS:N_FRONT_PARAMS + n_cast], refs[N_FRONT_PARAMS + n_cast:]
    gate_ref, tok_ref, mn_ref, conv_out_ref, shift_out_ref = refs[:N_FRONT_OUTPUTS]
    cast_out_refs = refs[N_FRONT_OUTPUTS:N_FRONT_OUTPUTS + n_cast]
    mix_next, mix_cur, hist_u_scr, hist_pr_scr = refs[N_FRONT_OUTPUTS + n_cast:]
    bgy_ref, rp_ref, y0_ref, g_ref, bg_ref = _token_fields(tok_ref)
    mt_ref, n0_ref = mn_ref.at[:, 0], mn_ref.at[:, 1]
    s = pl.program_id(0)
    is_sample = s - 1 >= N_PROMPT_CHUNKS // INTRA_CHUNKS
    row = _iota((CHUNK, 1), 0)
    cw = cw_ref[...]
    lane = _iota((CHUNK, D_LORA), 1)

    @pl.when(s == 0)
    def _():
        mix_cur[...] = jnp.zeros_like(mix_cur)
        hist_u_scr[...] = jnp.zeros_like(hist_u_scr)
        hist_pr_scr[...] = jnp.zeros_like(hist_pr_scr)

    def project():
        x = _load_tokens(x_refs, FRONT_TILE, jnp.minimum(s, N_TOK // FRONT_TILE - 1))
        h = (_rms_scale(x) * gpre_ref[...]).astype(_BF16)
        yield
        for j in range((MIX_COLS + GATE_COLS) // PROJ_SLAB):
            cols = slice(j * PROJ_SLAB, (j + 1) * PROJ_SLAB)
            p = _dot(h, w_ref[:, cols])
            if j < MIX_COLS // PROJ_SLAB:
                mix_next[:, cols] = p
            else:
                gate_ref[:, j * PROJ_SLAB - MIX_COLS:(j + 1) * PROJ_SLAB - MIX_COLS] = p
            yield

    hist = dict(u=hist_u_scr[...], pr=hist_pr_scr[...])

    def prepare(c):
        rows = slice(c * CHUNK, (c + 1) * CHUNK)
        u_hist = jnp.where(is_sample, conv0_ref[c], hist["u"])
        pr_hist = jnp.where(is_sample, shift0_ref[c], hist["pr"])

        xin = mix_cur[rows, 0:D_CONV]
        bg = mix_cur[rows, D_CONV:2 * D_CONV]
        cg = mix_cur[rows, 2 * D_CONV:3 * D_CONV]
        u = cg * xin
        u1 = jnp.where(row == 0, u_hist[1:2], pltpu.roll(u, 1, axis=0))
        u2 = jnp.where(row == 0, u_hist[0:1], jnp.where(row == 1, u_hist[1:2], pltpu.roll(u, 2, axis=0)))
        bgy_ref[rows, :] = bg * (cw[0:1] * u2 + cw[1:2] * u1 + cw[2:3] * u)
        hist["u"] = u[CHUNK - 2:CHUNK]
        conv_out_ref[c] = hist["u"]
        yield

        pr = mix_cur[rows, 3 * D_CONV:MIX_COLS]
        shifted = jnp.where(row == 0, pr_hist, pltpu.roll(pr, 1, axis=0))
        ps = pr + (shifted - pr) * mu_ref[...]
        hist["pr"] = pr[CHUNK - 1:CHUNK]
        shift_out_ref[c] = hist["pr"]
        r = ps[:, 0:D_RWKV]
        k = ps[:, D_RWKV:2 * D_RWKV]
        v = ps[:, 2 * D_RWKV:3 * D_RWKV]
        yield

        lora_in = ps[:, 3 * D_RWKV:RWKV_COLS]
        act = jnp.where(lane < D_DECAY_LORA, jnp.tanh(lora_in),
                        jnp.where(lane < D_DECAY_LORA + D_AAA_LORA, lora_in, jax.nn.sigmoid(lora_in)))
        lora = _dot(act.astype(_BF16), lora_w_ref[...])
        yield

        w_raw = wd0_ref[...] + lora[:, 0:D_RWKV]
        w_raw = -jax.nn.softplus(-w_raw) - 0.5
        lw = -jnp.exp(w_raw)
        cum = _running_log_decay(lw)
        yield

        a_sig = jax.nn.sigmoid(a0_ref[...] + lora[:, D_RWKV:2 * D_RWKV])
        g = lora[:, 2 * D_RWKV:3 * D_RWKV]
        g_ref[rows, :] = g
        kk = k * kk_w_ref[...]
        yield
        kk = kk / jnp.maximum(jnp.sqrt(_head_sums(kk * kk)), 1e-12)
        yield
        kf = k * (1.0 + (a_sig - 1.0) * ka_w_ref[...])
        bg_ref[rows, :] = _head_sums(r * kf * rk_w_ref[...]) * v * g
        yield
        return _decayed_operands(r, kf, v, lw, cum, kk, a_sig)

    def prepare_group(first_chunk):
        ops = []
        for c in range(first_chunk, first_chunk + INTRA_GROUP):
            ops.append((yield from prepare(c)))
        return ops

    def store(first_chunk, results):
        rps, y0s, mts, n0s = results
        for g in range(INTRA_GROUP):
            c = first_chunk + g
            rows = slice(c * CHUNK, (c + 1) * CHUNK)
            rp_ref[rows, :] = rps[g]
            y0_ref[rows, :] = y0s[g]
            for p in range(N_PAIRS):
                mt_ref[c, p] = mts[(g, p)]
                n0_ref[c, p] = n0s[(g, p)]

    def hand_over():
        for j in range(MIX_COLS // PROJ_SLAB):
            cols = slice(j * PROJ_SLAB, (j + 1) * PROJ_SLAB)
            mix_cur[:, cols] = mix_next[:, cols]
            yield

    assert INTRA_CHUNKS == INTRA_GROUP
    n_mix_steps = 1 + MIX_COLS // PROJ_SLAB
    n_prepare_steps = 7 * INTRA_GROUP
    assert n_prepare_steps % n_mix_steps == 0
    projection = _Steps(project())
    operands = _zip_steps(_Steps(prepare_group(0)), (projection, 1, n_prepare_steps // n_mix_steps))
    hist_u_scr[...] = hist["u"]
    hist_pr_scr[...] = hist["pr"]
    copy = _Steps(hand_over())
    casts = _Steps(_cast_blocks(cast_in_refs, cast_out_refs))
    results = _zip_steps(_Steps(_intra_pairs(operands)), (projection, 1, 1), (copy, 1, 1), (casts, 1, 2))
    projection.finish()
    copy.finish()
    casts.finish()
    store(0, results)


def _front(x_parts, layer, vectors, w_in_bf16, conv0, shift0, cw, lora_w, weights_to_cast):
    n_tiles = N_TOK // FRONT_TILE
    cast_in, cast_out, cast_shape = zip(*[_cast_specs(w, layer, n_tiles + 1) for w in weights_to_cast])
    n_prompt_tiles = SEQ // FRONT_TILE
    projected = lambda s: jnp.minimum(s, n_tiles - 1)
    prepared = lambda s: jnp.maximum(s - 1, 0)
    tok = lambda width: pl.BlockSpec((FRONT_TILE, width), lambda s: (prepared(s), 0))
    full = lambda a: _layer_spec(a, layer)
    seq = lambda a: pl.BlockSpec((None, INTRA_CHUNKS) + a.shape[2:],
                                 lambda s: (layer, jnp.maximum(prepared(s) - n_prompt_tiles, 0), 0, 0))
    per_chunk = lambda *dims: pl.BlockSpec((INTRA_CHUNKS,) + dims, lambda s: (prepared(s),) + (0,) * len(dims))
    return pl.pallas_call(
        functools.partial(_front_kernel, len(x_parts), len(weights_to_cast)),
        grid=(n_tiles + 1,),
        in_specs=(_token_specs(len(x_parts), FRONT_TILE, projected)
                  + [full(vectors), _layer_spec(w_in_bf16, None), seq(conv0), seq(shift0), full(cw), full(lora_w)]
                  + list(cast_in)),
        out_specs=[pl.BlockSpec((FRONT_TILE, GATE_COLS), lambda s: (projected(s), 0)),
                   tok(N_TOKEN_FIELDS * D_RWKV), per_chunk(2, N_PAIRS, HEAD_SIZE, PAIR),
                   per_chunk(2, D_CONV), per_chunk(1, RWKV_COLS)] + list(cast_out),
        out_shape=[jax.ShapeDtypeStruct((N_TOK, GATE_COLS), _F32),
                   jax.ShapeDtypeStruct((N_TOK, N_TOKEN_FIELDS * D_RWKV), _F32),
                   jax.ShapeDtypeStruct((N_CHUNKS, 2, N_PAIRS, HEAD_SIZE, PAIR), _F32),
                   jax.ShapeDtypeStruct((N_CHUNKS, 2, D_CONV), _F32),
                   jax.ShapeDtypeStruct((N_CHUNKS, 1, RWKV_COLS), _F32)] + list(cast_shape),
        scratch_shapes=[pltpu.VMEM((FRONT_TILE, MIX_COLS), _F32), pltpu.VMEM((FRONT_TILE, MIX_COLS), _F32),
                        pltpu.VMEM((2, D_CONV), _F32), pltpu.VMEM((1, RWKV_COLS), _F32)],
        compiler_params=pltpu.CompilerParams(
            dimension_semantics=("arbitrary",), vmem_limit_bytes=VMEM_LIMIT_BYTES),
        name="front",
    )(*x_parts, vectors, w_in_bf16, conv0, shift0, cw, lora_w, *weights_to_cast)


N_POST_INPUTS = 10


def _post_kernel(n_x, n_out, n_cast, *refs):
    x_refs, refs = refs[:n_x], refs[n_x:]
    (tok_ref, gate_ref, mn_ref, s0_ref, vec_ref,
     wc_ref, wr_ref, wo_ref, wup_ref, wdown_ref) = refs[:N_POST_INPUTS]
    vec = _vector_views(vec_ref)
    lnw_ref, lnb_ref, gmix_ref = vec["ln_x_w"], vec["ln_x_b"], vec["norm_mix_post"]
    gpre_ref, gpost_ref = vec["norm_ffn_pre"], vec["norm_ffn_post"]
    cast_in_refs, refs = refs[N_POST_INPUTS:N_POST_INPUTS + n_cast], refs[N_POST_INPUTS + n_cast:]
    out_refs, (s_prompt_ref, s_sample_ref) = refs[:n_out], refs[n_out:n_out + 2]
    cast_out_refs, (state,) = refs[n_out + 2:n_out + 2 + n_cast], refs[n_out + 2 + n_cast:]
    _Steps(_cast_blocks(cast_in_refs, cast_out_refs)).finish()
    bgy_ref, rp_ref, y0_ref, g_ref, bg_ref = _token_fields(tok_ref)
    mt_ref, n0_ref = mn_ref.at[:, 0], mn_ref.at[:, 1]
    i = pl.program_id(0)
    is_stream = i >= N_PROMPT_TILES

    @pl.when(i == 0)
    def _():
        state[...] = jnp.zeros_like(state)

    eye = (_iota((PAIR, PAIR), 0) == _iota((PAIR, PAIR), 1)).astype(_BF16)
    chunk_rows = lambda c: slice(c * CHUNK, (c + 1) * CHUNK)
    pair_lanes = lambda p: slice(p * PAIR, (p + 1) * PAIR)

    def group_norm(ys):
        y = jnp.concatenate(ys, axis=1)
        inv_n = 1.0 / HEAD_SIZE
        d = y - _head_sums(y) * inv_n
        var = _head_sums(d * d) * inv_n
        return d * lax.rsqrt(var + LN_X_EPS) * lnw_ref[...] + lnb_ref[...]

    def heads_of(s):
        return s[0:HEAD_SIZE, 0:HEAD_SIZE], s[HEAD_SIZE:PAIR, HEAD_SIZE:PAIR]

    def prompt_steps():
        ws = [state[p] for p in range(N_PAIRS)]
        yns = []
        for c in range(TILE_CHUNKS):
            ys = []
            for p in range(N_PAIRS):
                w_pieces = _split2(ws[p])
                ys.append(_dot(rp_ref[chunk_rows(c), pair_lanes(p)].astype(_BF16), w_pieces[0])
                          + y0_ref[chunk_rows(c), pair_lanes(p)])
                ws[p] = _by_head_rows(_dot_pieces(_split2(mt_ref[c, p]), w_pieces) + n0_ref[c, p])
            yield
            yns.append(group_norm(ys))
            yield
        for p in range(N_PAIRS):
            state[p] = ws[p]
            s_prompt_ref[2 * p], s_prompt_ref[2 * p + 1] = heads_of(_dot_pieces((eye,), _split3(ws[p]), _NT))
        return jnp.concatenate(yns, axis=0)

    def stream_steps():
        zeros = jnp.zeros((HEAD_SIZE, HEAD_SIZE), _F32)
        yns = []
        for c in range(TILE_CHUNKS):
            ys = []
            for p in range(N_PAIRS):
                s0 = jnp.concatenate([jnp.concatenate([s0_ref[c, 2 * p], zeros], axis=1),
                                      jnp.concatenate([zeros, s0_ref[c, 2 * p + 1]], axis=1)], axis=0)
                s0_hi, s0_lo = _split2(s0)
                ys.append(_dot(rp_ref[chunk_rows(c), pair_lanes(p)].astype(_BF16), s0_hi, _NT)
                          + y0_ref[chunk_rows(c), pair_lanes(p)])
                mn = jnp.concatenate([_by_head_rows(mt_ref[c, p]), _by_head_rows(n0_ref[c, p])], axis=1)
                s_next = _dot_pieces((jnp.concatenate([s0_hi, eye], axis=1),
                                      jnp.concatenate([s0_lo, jnp.zeros_like(eye)], axis=1)),
                                     _split2(mn), _NT)
                s_sample_ref[c, 2 * p], s_sample_ref[c, 2 * p + 1] = heads_of(s_next)
            yield
            yns.append(group_norm(ys))
            yield
        return jnp.concatenate(yns, axis=0)

    def gate_steps():
        branch_a = _dot(bgy_ref[...].astype(_BF16), wc_ref[...])
        yield
        gates = []
        for c in range(TILE_CHUNKS):
            gates.append((jax.nn.sigmoid(gate_ref[chunk_rows(c), 0:D_MODEL]),
                          jax.nn.sigmoid(gate_ref[chunk_rows(c), D_MODEL:GATE_COLS])))
            yield
        gate_a = jnp.concatenate([ga for ga, _ in gates], axis=0)
        gate_b = jnp.concatenate([gb for _, gb in gates], axis=0)
        return branch_a, gate_a, gate_b

    def tile_output(recurrence_steps):
        gate_work = _Steps(gate_steps())
        yn = _zip_steps(_Steps(recurrence_steps), (gate_work, 1, 1))
        branch_a, gate_a, gate_b = gate_work.finish()
        z = yn * g_ref[...] + bg_ref[...]
        branch_b = _dot(z.astype(_BF16), wr_ref[...])
        merged = gate_a * branch_a + gate_b * branch_b
        m = _dot(merged.astype(_BF16), wo_ref[...])
        x = _load_tokens(x_refs) + _rms_scale(m) * gmix_ref[...]
        h = (_rms_scale(x) * gpre_ref[...]).astype(_BF16)
        up = _dot(h, wup_ref[...])
        act = jax.nn.silu(up[:, :D_FF]) * up[:, D_FF:]
        f = _dot(act.astype(_BF16), wdown_ref[...])
        return x + _rms_scale(f) * gpost_ref[...]

    @pl.when(jnp.logical_not(is_stream))
    def _():
        out = tile_output(prompt_steps())
        if len(out_refs) == 1:
            out_refs[0][...] = out
        else:
            out_refs[0][0] = out

    @pl.when(is_stream)
    def _():
        out = tile_output(stream_steps())
        if len(out_refs) == 1:
            out_refs[0][...] = out
        else:
            out_refs[1][...] = out.reshape(STREAMS_PER_TILE, DEC_SEQ, D_MODEL)


def _post(x_parts, layer, token_inputs, mn, state_wkv, params, weights_to_cast, split_out):
    n_steps = N_TOK // TOKEN_TILE
    cast = [_cast_specs(w, layer + 1, n_steps) for w in weights_to_cast]
    param_spec = lambda a: _layer_spec(a, layer if a.ndim == 3 else None)
    tok = lambda a: pl.BlockSpec((TOKEN_TILE, a.shape[1]), lambda i: (i, 0))
    blk = pl.BlockSpec((TILE_CHUNKS, 2, N_PAIRS, HEAD_SIZE, PAIR), lambda i: (i, 0, 0, 0, 0))
    heads = (N_HEADS, HEAD_SIZE, HEAD_SIZE)
    stream_block = lambda i: jnp.maximum(i - N_PROMPT_TILES, 0)
    st_in = pl.BlockSpec((None, STREAMS_PER_TILE) + heads, lambda i: (layer, stream_block(i), 0, 0, 0))
    st_out = pl.BlockSpec((STREAMS_PER_TILE,) + heads, lambda i: (stream_block(i), 0, 0, 0))
    if split_out:
        out_specs = _token_specs(2)
        out_shape = [jax.ShapeDtypeStruct((1, SEQ, D_MODEL), _F32),
                     jax.ShapeDtypeStruct((DEC_BATCH, DEC_SEQ, D_MODEL), _F32)]
    else:
        out_specs = _token_specs(1)
        out_shape = [jax.ShapeDtypeStruct((N_TOK, D_MODEL), _F32)]
    outs = pl.pallas_call(
        functools.partial(_post_kernel, len(x_parts), len(out_shape), len(cast)),
        grid=(n_steps,),
        in_specs=(_token_specs(len(x_parts)) + [tok(a) for a in token_inputs] + [blk, st_in]
                  + [param_spec(a) for a in params] + [c[0] for c in cast]),
        out_specs=out_specs + [pl.BlockSpec(heads, lambda i: (0, 0, 0)), st_out] + [c[1] for c in cast],
        out_shape=out_shape + [jax.ShapeDtypeStruct(heads, _F32),
                               jax.ShapeDtypeStruct((DEC_BATCH,) + heads, _F32)] + [c[2] for c in cast],
        scratch_shapes=[pltpu.VMEM((N_PAIRS, PAIR, PAIR), _F32)],
        compiler_params=pltpu.CompilerParams(
            dimension_semantics=("arbitrary",), vmem_limit_bytes=VMEM_LIMIT_BYTES),
        name="post",
    )(*x_parts, *token_inputs, mn, state_wkv, *params, *weights_to_cast)
    n_x_out = len(out_shape)
    return outs[:n_x_out], outs[n_x_out], outs[n_x_out + 1], outs[n_x_out + 2:]


def _lora_weight(w_decay2, a2, g2):
    w = jnp.zeros((DEPTH, D_LORA, 3 * D_RWKV), _BF16)
    w = w.at[:, 0:D_DECAY_LORA, 0:D_RWKV].set(w_decay2.astype(_BF16))
    w = w.at[:, D_DECAY_LORA:D_DECAY_LORA + D_AAA_LORA, D_RWKV:2 * D_RWKV].set(a2.astype(_BF16))
    return w.at[:, D_DECAY_LORA + D_AAA_LORA:, 2 * D_RWKV:].set(g2.astype(_BF16))


def kernel(x_prompt, x_sample, state_conv, state_shift, state_wkv, norm_mix_pre, norm_mix_post, w_in, mu_shift, conv_w, w_decay0, w_decay2, a0, a2, g2, k_k, k_a, r_k, ln_x_w, ln_x_b, w_conv_out, w_rwkv_out, w_o, norm_ffn_pre, norm_ffn_post, w_ffn_up, w_ffn_down):
    x_parts = (x_prompt, x_sample)
    lora_w = _lora_weight(w_decay2, a2, g2)
    by_name = dict(mu_shift=mu_shift, w_decay0=w_decay0, a0=a0, k_k=k_k, k_a=k_a, r_k=r_k.reshape(DEPTH, D_RWKV),
                   norm_mix_pre=norm_mix_pre, ln_x_w=ln_x_w, ln_x_b=ln_x_b, norm_mix_post=norm_mix_post,
                   norm_ffn_pre=norm_ffn_pre, norm_ffn_post=norm_ffn_post)
    vectors = jnp.concatenate([by_name[name] for name, _ in VECTOR_FIELDS], axis=1)[:, None, :]
    shift0 = state_shift[:, :, None]
    w_in_b = w_in[0].astype(_BF16)
    conv_p, shift_p, wkv_p, conv_s, shift_s, wkv_s = [], [], [], [], [], []
    for l in range(DEPTH):
        gates, fields, mn, conv_out, shift_out, wc, wr, wo, wup, wdown = _front(
            x_parts, l, vectors, w_in_b, state_conv, shift0, conv_w, lora_w,
            weights_to_cast=(w_conv_out, w_rwkv_out, w_o, w_ffn_up, w_ffn_down))
        last_layer = l == DEPTH - 1
        x_parts, s_prompt, s_sample, next_w_in = _post(
            x_parts, l, (fields, gates), mn, state_wkv, (vectors, wc, wr, wo, wup, wdown),
            weights_to_cast=() if last_layer else (w_in,), split_out=last_layer)
        if not last_layer:
            w_in_b, = next_w_in

        last = N_PROMPT_CHUNKS - 1
        conv_p.append(conv_out[last:last + 1])
        conv_s.append(conv_out[N_PROMPT_CHUNKS:])
        shift_p.append(shift_out[last:last + 1, 0])
        shift_s.append(shift_out[N_PROMPT_CHUNKS:, 0])
        wkv_p.append(s_prompt[None])
        wkv_s.append(s_sample)
    y_prompt, y_sample = x_parts
    return (y_prompt, y_sample,
            jnp.stack(conv_p, 0), jnp.stack(shift_p, 0), jnp.stack(wkv_p, 0),
            jnp.stack(conv_s, 0), jnp.stack(shift_s, 0), jnp.stack(wkv_s, 0))
```

```python
import functools

import jax
import jax.numpy as jnp
from jax import lax
from jax.experimental import pallas as pl
from jax.experimental.pallas import tpu as pltpu

D_MODEL = 1024
SEQ = 16384
DEPTH = 2
DEC_BATCH = 32
DEC_SEQ = 64
D_CONV = 512
D_RWKV = 512
HEAD_SIZE = 64
N_HEADS = 8
D_DECAY_LORA = 64
D_AAA_LORA = 64
D_GATE_LORA = 128
D_LORA = D_DECAY_LORA + D_AAA_LORA + D_GATE_LORA
LN_X_EPS = 64e-5
D_FF = 2816
RMS_EPS = 1e-6
RWKV_COLS = 3 * D_RWKV + D_LORA
MIX_COLS = 3 * D_CONV + RWKV_COLS
GATE_COLS = 2 * D_MODEL

CHUNK = 64
N_TOK = SEQ + DEC_BATCH * DEC_SEQ
N_CHUNKS = N_TOK // CHUNK
N_PROMPT_CHUNKS = SEQ // CHUNK
PAIR = 2 * HEAD_SIZE
N_PAIRS = N_HEADS // 2
TOKEN_TILE = 256
N_PROMPT_TILES = SEQ // TOKEN_TILE
STREAMS_PER_TILE = TOKEN_TILE // DEC_SEQ
TILE_CHUNKS = TOKEN_TILE // CHUNK
INTRA_CHUNKS = 4
INTRA_GROUP = 4
FRONT_TILE = INTRA_CHUNKS * CHUNK
PROJ_SLAB = 256
SUBLANES = 8
BF16_SUBLANES = 16
PIECES_A = 1
PIECES_INV = 1
PIECES_OUT = 1
VMEM_LIMIT_BYTES = 56 * 1024 * 1024

_BF16 = jnp.bfloat16
_F32 = jnp.float32
_NN = (((1,), (0,)), ((), ()))
_NT = (((1,), (1,)), ((), ()))
_TN = (((0,), (0,)), ((), ()))


def _dot(a, b, dims=_NN):
    return lax.dot_general(a, b, dims, preferred_element_type=_F32)


def _pieces(x, n):
    out = []
    for _ in range(n - 1):
        hi = x.astype(_BF16)
        out.append(hi)
        x = x - hi.astype(_F32)
    out.append(x.astype(_BF16))
    return tuple(out)


def _split2(x):
    return _pieces(x, 2)


def _split3(x):
    return _pieces(x, 3)


def _dot_pieces(a, b, dims=_NN):
    n = max(len(a), len(b))
    terms = [_dot(a[i], b[j], dims) for i in range(len(a)) for j in range(len(b)) if i + j < n]
    total = terms[0]
    if len(terms) > 1:
        rest = terms[1]
        for t in terms[2:]:
            rest = rest + t
        total = total + rest
    return total


class _Steps:
    def __init__(self, gen):
        self.gen, self.done, self.value = gen, False, None

    def step(self):
        if not self.done:
            try:
                next(self.gen)
            except StopIteration as stop:
                self.done, self.value = True, stop.value

    def finish(self):
        while not self.done:
            self.step()
        return self.value


def _zip_steps(main, *sides):
    count = 0
    while not main.done:
        main.step()
        count += 1
        for side, n, every in sides:
            if count % every == 0:
                for _ in range(n):
                    side.step()
    return main.value


def _rms_scale(x):
    return x * lax.rsqrt(jnp.mean(x * x, axis=-1, keepdims=True) + RMS_EPS)


def _iota(shape, dim):
    return lax.broadcasted_iota(jnp.int32, shape, dim)


def _first_head(width=PAIR):
    return _iota((1, width), 1) % PAIR < HEAD_SIZE


def _head_sums(x):
    first = _first_head()
    outs = []
    for p in range(x.shape[1] // PAIR):
        xp = x[:, p * PAIR:(p + 1) * PAIR]
        lo = jnp.sum(jnp.where(first, xp, 0.0), axis=1, keepdims=True)
        hi = jnp.sum(jnp.where(first, 0.0, xp), axis=1, keepdims=True)
        outs.append(jnp.where(first, lo, hi))
    return jnp.concatenate(outs, axis=1)


def _by_head_rows(x):
    first = _first_head()
    return jnp.concatenate([jnp.where(first, x, 0.0), jnp.where(first, 0.0, x)], axis=0)


def _diag_blocks(x):
    return jnp.where(_first_head(), x[0:HEAD_SIZE], x[HEAD_SIZE:PAIR])


def _layer_spec(a, layer):
    if layer is None:
        return pl.BlockSpec(a.shape, lambda i: (0,) * a.ndim, pipeline_mode=pl.Buffered(1))
    return pl.BlockSpec((None,) + a.shape[1:], lambda i: (layer,) + (0,) * (a.ndim - 1),
                        pipeline_mode=pl.Buffered(1))


def _cast_specs(w, layer, n_steps):
    _, k, n = w.shape
    rows = BF16_SUBLANES
    while k % rows or k // rows > n_steps:
        rows += BF16_SUBLANES
    block = lambda s: jnp.minimum(s, k // rows - 1)
    return (pl.BlockSpec((None, rows, n), lambda s: (layer, block(s), 0)),
            pl.BlockSpec((rows, n), lambda s: (block(s), 0)),
            jax.ShapeDtypeStruct((k, n), _BF16))


def _cast_blocks(in_refs, out_refs):
    for in_ref, out_ref in zip(in_refs, out_refs):
        out_ref[...] = in_ref[...].astype(_BF16)
        yield


def _token_specs(n_parts, tile=TOKEN_TILE, tile_of_step=lambda i: i):
    if n_parts == 1:
        return [pl.BlockSpec((tile, D_MODEL), lambda i: (tile_of_step(i), 0))]
    n_prompt = SEQ // tile
    return [pl.BlockSpec((1, tile, D_MODEL), lambda i: (0, jnp.minimum(tile_of_step(i), n_prompt - 1), 0)),
            pl.BlockSpec((tile // DEC_SEQ, DEC_SEQ, D_MODEL),
                         lambda i: (jnp.maximum(tile_of_step(i) - n_prompt, 0), 0, 0))]


def _load_tokens(x_refs, tile=TOKEN_TILE, tile_index=None):
    if len(x_refs) == 1:
        return x_refs[0][...]
    prompt_ref, stream_ref = x_refs
    tile_index = pl.program_id(0) if tile_index is None else tile_index
    return jnp.where(tile_index >= SEQ // tile, stream_ref[...].reshape(tile, D_MODEL), prompt_ref[0])


VECTOR_FIELDS = (("mu_shift", RWKV_COLS), ("w_decay0", D_RWKV), ("a0", D_RWKV), ("k_k", D_RWKV), ("k_a", D_RWKV),
                 ("r_k", D_RWKV), ("norm_mix_pre", D_MODEL), ("ln_x_w", D_RWKV), ("ln_x_b", D_RWKV),
                 ("norm_mix_post", D_MODEL), ("norm_ffn_pre", D_MODEL), ("norm_ffn_post", D_MODEL))


def _vector_views(vec_ref):
    views, start = {}, 0
    for name, width in VECTOR_FIELDS:
        views[name] = vec_ref.at[:, start:start + width]
        start += width
    return views


N_TOKEN_FIELDS = 5


def _token_fields(tok_ref):
    return [tok_ref.at[:, f * D_RWKV:(f + 1) * D_RWKV] for f in range(N_TOKEN_FIELDS)]


def _running_log_decay(lw):
    tri = jnp.where(_iota((CHUNK, CHUNK), 1) <= _iota((CHUNK, CHUNK), 0), 1.0, 0.0).astype(_BF16)
    return _dot_pieces((tri,), _split3(lw))


def _decayed_operands(r, kf, v, lw, cum, kk, a_sig):
    last = cum[CHUNK - 1:CHUNK]
    e_inv = jnp.exp(-cum)
    e_tail = jnp.exp(last - cum)
    b = kk * a_sig
    return dict(a_t=-kk * jnp.exp(cum - lw), r_t=r * jnp.exp(cum), b_t=b * e_inv, k_t=kf * e_inv,
                b_h=b * e_tail, k_h=kf * e_tail, v=v, dec=jnp.exp(last))


def _intra_pairs(chunks):
    first = _first_head()
    first2 = _first_head(2 * PAIR)
    g_row = _iota((2 * CHUNK, 2 * PAIR), 0)
    keep = (_iota((2 * CHUNK, 2 * PAIR), 1) % CHUNK) < (g_row % CHUNK) + g_row // CHUNK
    rc_xor = _iota((CHUNK, PAIR), 0) ^ (_iota((CHUNK, PAIR), 1) % HEAD_SIZE)
    eye = _iota((PAIR, PAIR), 0) == _iota((PAIR, PAIR), 1)
    zeros = jnp.zeros((CHUNK, PAIR), _F32)
    items = [(c, p) for c in range(len(chunks)) for p in range(N_PAIRS)]
    part = lambda it, name: chunks[it[0]][name][:, it[1] * PAIR:(it[1] + 1) * PAIR]
    out_pieces = lambda x: _pieces(x, PIECES_OUT)
    inv_pieces = lambda x: _pieces(x, PIECES_INV)

    gm, l_c, ak_c = {}, {}, {}
    for it in items:
        ar = jnp.concatenate([part(it, "a_t"), part(it, "r_t")], axis=0)
        bk = jnp.concatenate([part(it, "b_t"), part(it, "k_t")], axis=0)
        bk_heads = jnp.concatenate([jnp.where(first, bk, 0.0), jnp.where(first, 0.0, bk)], axis=0)
        gm[it] = jnp.where(keep, _dot_pieces(_pieces(ar, PIECES_A), _pieces(bk_heads, PIECES_A), _NT), 0.0)
        a_h0, a_h1 = gm[it][0:CHUNK, 0:PAIR], gm[it][0:CHUNK, PAIR:2 * PAIR]
        l_c[it] = jnp.where(first, a_h0, pltpu.roll(a_h1, HEAD_SIZE, axis=1))
        ak_c[it] = jnp.where(first, pltpu.roll(a_h0, HEAD_SIZE, axis=1), a_h1)
    yield

    inv = {it: jnp.where(rc_xor == 0, 1.0, jnp.where(rc_xor == 1, l_c[it], 0.0)) for it in items}
    akv = {it: _dot_pieces(out_pieces(ak_c[it]), out_pieces(_by_head_rows(part(it, "v")))) for it in items}
    yield
    m = 2
    while m < CHUNK:
        level = jnp.logical_and(rc_xor >= m, rc_xor < 2 * m)
        slabs = m >= SUBLANES
        n_blocks = CHUNK // (2 * m)
        if slabs:
            low = lambda a, m=m: jnp.concatenate([a[s:s + m] for s in range(m, CHUNK, 2 * m)], axis=0)
        else:
            low = lambda a: a
        inv_bd = {it: inv_pieces(_by_head_rows(inv[it])) for it in items}
        x = {it: _dot_pieces(inv_pieces(low(jnp.where(level, l_c[it], 0.0))), inv_bd[it]) for it in items}
        yield
        for it in items:
            if slabs:
                zeros_m = jnp.zeros((m, PAIR), _F32)
                x_full = jnp.concatenate([a for j in range(n_blocks)
                                          for a in (zeros_m, x[it][j * m:(j + 1) * m])], axis=0)
                upd = _dot_pieces(inv_pieces(low(inv[it])), inv_pieces(_by_head_rows(x_full)))
                inv[it] = jnp.concatenate(
                    [a for j in range(n_blocks)
                     for a in (inv[it][2 * j * m:(2 * j + 1) * m],
                               inv[it][(2 * j + 1) * m:(2 * j + 2) * m] + upd[j * m:(j + 1) * m])], axis=0)
            else:
                inv[it] = inv[it] + _dot_pieces(inv_pieces(inv[it]), inv_pieces(_by_head_rows(x[it])))
        yield
        m *= 2

    uva_pieces = {}
    for it in items:
        u0a = _dot_pieces(out_pieces(inv[it]), out_pieces(jnp.concatenate(
            [_by_head_rows(akv[it]), _by_head_rows(part(it, "a_t"))], axis=1)))
        uva = jnp.concatenate([u0a, jnp.concatenate([part(it, "v"), zeros], axis=1)], axis=0)
        uva_pieces[it] = out_pieces(uva)
    yield
    rps, y0s, mts, n0s = {}, {}, {}, {}
    for it in items:
        rr = jnp.concatenate([gm[it][CHUNK:2 * CHUNK, 0:PAIR], gm[it][CHUNK:2 * CHUNK, PAIR:2 * PAIR]],
                             axis=0)
        yr2 = _dot_pieces(out_pieces(rr), uva_pieces[it])
        yr = jnp.where(first2, yr2[0:CHUNK], yr2[CHUNK:2 * CHUNK])
        y0s[it] = yr[:, 0:PAIR]
        rps[it] = part(it, "r_t") + yr[:, PAIR:2 * PAIR]
    for it in items:
        bk_h = jnp.concatenate([part(it, "b_h"), part(it, "k_h")], axis=0)
        nm = _dot_pieces(out_pieces(bk_h), uva_pieces[it], _TN)
        n0s[it] = _diag_blocks(nm[:, 0:PAIR])
        mts[it] = _diag_blocks(nm[:, PAIR:2 * PAIR] + jnp.where(eye, part(it, "dec"), 0.0))
    cat = lambda d, c: jnp.concatenate([d[(c, p)] for p in range(N_PAIRS)], axis=1)
    return ([cat(rps, c) for c in range(len(chunks))], [cat(y0s, c) for c in range(len(chunks))], mts, n0s)


N_FRONT_PARAMS = 6
N_FRONT_OUTPUTS = 5


def _front_kernel(n_x, n_cast, *refs):
    x_refs, refs = refs[:n_x], refs[n_x:]
    vec_ref, w_ref, conv0_ref, shift0_ref, cw_ref, lora_w_ref = refs[:N_FRONT_PARAMS]
    vec = _vector_views(vec_ref)
    gpre_ref, mu_ref, wd0_ref, a0_ref = vec["norm_mix_pre"], vec["mu_shift"], vec["w_decay0"], vec["a0"]
    kk_w_ref, ka_w_ref, rk_w_ref = vec["k_k"], vec["k_a"], vec["r_k"]
    cast_in_refs, refs = refs[N_FRONT_PARAMS:N_FRONT_PARAMS + n_cast], refs[N_FRONT_PARAMS + n_cast:]
    gate_ref, tok_ref, mn_ref, conv_out_ref, shift_out_ref = refs[:N_FRONT_OUTPUTS]
    cast_out_refs = refs[N_FRONT_OUTPUTS:N_FRONT_OUTPUTS + n_cast]
    mix_next, mix_cur, hist_u_scr, hist_pr_scr = refs[N_FRONT_OUTPUTS + n_cast:]
    bgy_ref, rp_ref, y0_ref, g_ref, bg_ref = _token_fields(tok_ref)
    mt_ref, n0_ref = mn_ref.at[:, 0], mn_ref.at[:, 1]
    s = pl.program_id(0)
    is_sample = s - 1 >= N_PROMPT_CHUNKS // INTRA_CHUNKS
    row = _iota((CHUNK, 1), 0)
    cw = cw_ref[...]
    lane = _iota((CHUNK, D_LORA), 1)

    @pl.when(s == 0)
    def _():
        mix_cur[...] = jnp.zeros_like(mix_cur)
        hist_u_scr[...] = jnp.zeros_like(hist_u_scr)
        hist_pr_scr[...] = jnp.zeros_like(hist_pr_scr)

    def project():
        x = _load_tokens(x_refs, FRONT_TILE, jnp.minimum(s, N_TOK // FRONT_TILE - 1))
        h = (_rms_scale(x) * gpre_ref[...]).astype(_BF16)
        yield
        for j in range((MIX_COLS + GATE_COLS) // PROJ_SLAB):
            cols = slice(j * PROJ_SLAB, (j + 1) * PROJ_SLAB)
            p = _dot(h, w_ref[:, cols])
            if j < MIX_COLS // PROJ_SLAB:
                mix_next[:, cols] = p
            else:
                gate_ref[:, j * PROJ_SLAB - MIX_COLS:(j + 1) * PROJ_SLAB - MIX_COLS] = p
            yield

    hist = dict(u=hist_u_scr[...], pr=hist_pr_scr[...])

    def prepare(c):
        rows = slice(c * CHUNK, (c + 1) * CHUNK)
        u_hist = jnp.where(is_sample, conv0_ref[c], hist["u"])
        pr_hist = jnp.where(is_sample, shift0_ref[c], hist["pr"])

        xin = mix_cur[rows, 0:D_CONV]
        bg = mix_cur[rows, D_CONV:2 * D_CONV]
        cg = mix_cur[rows, 2 * D_CONV:3 * D_CONV]
        u = cg * xin
        u1 = jnp.where(row == 0, u_hist[1:2], pltpu.roll(u, 1, axis=0))
        u2 = jnp.where(row == 0, u_hist[0:1], jnp.where(row == 1, u_hist[1:2], pltpu.roll(u, 2, axis=0)))
        bgy_ref[rows, :] = bg * (cw[0:1] * u2 + cw[1:2] * u1 + cw[2:3] * u)
        hist["u"] = u[CHUNK - 2:CHUNK]
        conv_out_ref[c] = hist["u"]
        yield

        pr = mix_cur[rows, 3 * D_CONV:MIX_COLS]
        shifted = jnp.where(row == 0, pr_hist, pltpu.roll(pr, 1, axis=0))
        ps = pr + (shifted - pr) * mu_ref[...]
        hist["pr"] = pr[CHUNK - 1:CHUNK]
        shift_out_ref[c] = hist["pr"]
        r = ps[:, 0:D_RWKV]
        k = ps[:, D_RWKV:2 * D_RWKV]
        v = ps[:, 2 * D_RWKV:3 * D_RWKV]
        yield

        lora_in = ps[:, 3 * D_RWKV:RWKV_COLS]
        act = jnp.where(lane < D_DECAY_LORA, jnp.tanh(lora_in),
                        jnp.where(lane < D_DECAY_LORA + D_AAA_LORA, lora_in, jax.nn.sigmoid(lora_in)))
        lora = _dot(act.astype(_BF16), lora_w_ref[...])
        yield

        w_raw = wd0_ref[...] + lora[:, 0:D_RWKV]
        w_raw = -jax.nn.softplus(-w_raw) - 0.5
        lw = -jnp.exp(w_raw)
        cum = _running_log_decay(lw)
        yield

        a_sig = jax.nn.sigmoid(a0_ref[...] + lora[:, D_RWKV:2 * D_RWKV])
        g = lora[:, 2 * D_RWKV:3 * D_RWKV]
        g_ref[rows, :] = g
        kk = k * kk_w_ref[...]
        yield
        kk = kk / jnp.maximum(jnp.sqrt(_head_sums(kk * kk)), 1e-12)
        yield
        kf = k * (1.0 + (a_sig - 1.0) * ka_w_ref[...])
        bg_ref[rows, :] = _head_sums(r * kf * rk_w_ref[...]) * v * g
        yield
        return _decayed_operands(r, kf, v, lw, cum, kk, a_sig)

    def prepare_group(first_chunk):
        ops = []
        for c in range(first_chunk, first_chunk + INTRA_GROUP):
            ops.append((yield from prepare(c)))
        return ops

    def store(first_chunk, results):
        rps, y0s, mts, n0s = results
        for g in range(INTRA_GROUP):
            c = first_chunk + g
            rows = slice(c * CHUNK, (c + 1) * CHUNK)
            rp_ref[rows, :] = rps[g]
            y0_ref[rows, :] = y0s[g]
            for p in range(N_PAIRS):
                mt_ref[c, p] = mts[(g, p)]
                n0_ref[c, p] = n0s[(g, p)]

    def hand_over():
        for j in range(MIX_COLS // PROJ_SLAB):
            cols = slice(j * PROJ_SLAB, (j + 1) * PROJ_SLAB)
            mix_cur[:, cols] = mix_next[:, cols]
            yield

    assert INTRA_CHUNKS == INTRA_GROUP
    n_mix_steps = 1 + MIX_COLS // PROJ_SLAB
    n_prepare_steps = 7 * INTRA_GROUP
    assert n_prepare_steps % n_mix_steps == 0
    projection = _Steps(project())
    operands = _zip_steps(_Steps(prepare_group(0)), (projection, 1, n_prepare_steps // n_mix_steps))
    hist_u_scr[...] = hist["u"]
    hist_pr_scr[...] = hist["pr"]
    copy = _Steps(hand_over())
    casts = _Steps(_cast_blocks(cast_in_refs, cast_out_refs))
    results = _zip_steps(_Steps(_intra_pairs(operands)), (projection, 1, 1), (copy, 1, 1), (casts, 1, 2))
    projection.finish()
    copy.finish()
    casts.finish()
    store(0, results)


def _front(x_parts, layer, vectors, w_in_bf16, conv0, shift0, cw, lora_w, weights_to_cast):
    n_tiles = N_TOK // FRONT_TILE
    cast_in, cast_out, cast_shape = zip(*[_cast_specs(w, layer, n_tiles + 1) for w in weights_to_cast])
    n_prompt_tiles = SEQ // FRONT_TILE
    projected = lambda s: jnp.minimum(s, n_tiles - 1)
    prepared = lambda s: jnp.maximum(s - 1, 0)
    tok = lambda width: pl.BlockSpec((FRONT_TILE, width), lambda s: (prepared(s), 0))
    full = lambda a: _layer_spec(a, layer)
    seq = lambda a: pl.BlockSpec((None, INTRA_CHUNKS) + a.shape[2:],
                                 lambda s: (layer, jnp.maximum(prepared(s) - n_prompt_tiles, 0), 0, 0))
    per_chunk = lambda *dims: pl.BlockSpec((INTRA_CHUNKS,) + dims, lambda s: (prepared(s),) + (0,) * len(dims))
    return pl.pallas_call(
        functools.partial(_front_kernel, len(x_parts), len(weights_to_cast)),
        grid=(n_tiles + 1,),
        in_specs=(_token_specs(len(x_parts), FRONT_TILE, projected)
                  + [full(vectors), _layer_spec(w_in_bf16, None), seq(conv0), seq(shift0), full(cw), full(lora_w)]
                  + list(cast_in)),
        out_specs=[pl.BlockSpec((FRONT_TILE, GATE_COLS), lambda s: (projected(s), 0)),
                   tok(N_TOKEN_FIELDS * D_RWKV), per_chunk(2, N_PAIRS, HEAD_SIZE, PAIR),
                   per_chunk(2, D_CONV), per_chunk(1, RWKV_COLS)] + list(cast_out),
        out_shape=[jax.ShapeDtypeStruct((N_TOK, GATE_COLS), _F32),
                   jax.ShapeDtypeStruct((N_TOK, N_TOKEN_FIELDS * D_RWKV), _F32),
                   jax.ShapeDtypeStruct((N_CHUNKS, 2, N_PAIRS, HEAD_SIZE, PAIR), _F32),
                   jax.ShapeDtypeStruct((N_CHUNKS, 2, D_CONV), _F32),
                   jax.ShapeDtypeStruct((N_CHUNKS, 1, RWKV_COLS), _F32)] + list(cast_shape),
        scratch_shapes=[pltpu.VMEM((FRONT_TILE, MIX_COLS), _F32), pltpu.VMEM((FRONT_TILE, MIX_COLS), _F32),
                        pltpu.VMEM((2, D_CONV), _F32), pltpu.VMEM((1, RWKV_COLS), _F32)],
        compiler_params=pltpu.CompilerParams(
            dimension_semantics=("arbitrary",), vmem_limit_bytes=VMEM_LIMIT_BYTES),
        name="front",
    )(*x_parts, vectors, w_in_bf16, conv0, shift0, cw, lora_w, *weights_to_cast)


N_POST_INPUTS = 10


def _post_kernel(n_x, n_out, n_cast, *refs):
    x_refs, refs = refs[:n_x], refs[n_x:]
    (tok_ref, gate_ref, mn_ref, s0_ref, vec_ref,
     wc_ref, wr_ref, wo_ref, wup_ref, wdown_ref) = refs[:N_POST_INPUTS]
    vec = _vector_views(vec_ref)
    lnw_ref, lnb_ref, gmix_ref = vec["ln_x_w"], vec["ln_x_b"], vec["norm_mix_post"]
    gpre_ref, gpost_ref = vec["norm_ffn_pre"], vec["norm_ffn_post"]
    cast_in_refs, refs = refs[N_POST_INPUTS:N_POST_INPUTS + n_cast], refs[N_POST_INPUTS + n_cast:]
    out_refs, (s_prompt_ref, s_sample_ref) = refs[:n_out], refs[n_out:n_out + 2]
    cast_out_refs, (state,) = refs[n_out + 2:n_out + 2 + n_cast], refs[n_out + 2 + n_cast:]
    _Steps(_cast_blocks(cast_in_refs, cast_out_refs)).finish()
    bgy_ref, rp_ref, y0_ref, g_ref, bg_ref = _token_fields(tok_ref)
    mt_ref, n0_ref = mn_ref.at[:, 0], mn_ref.at[:, 1]
    i = pl.program_id(0)
    is_stream = i >= N_PROMPT_TILES

    @pl.when(i == 0)
    def _():
        state[...] = jnp.zeros_like(state)

    eye = (_iota((PAIR, PAIR), 0) == _iota((PAIR, PAIR), 1)).astype(_BF16)
    chunk_rows = lambda c: slice(c * CHUNK, (c + 1) * CHUNK)
    pair_lanes = lambda p: slice(p * PAIR, (p + 1) * PAIR)

    def group_norm(ys):
        y = jnp.concatenate(ys, axis=1)
        inv_n = 1.0 / HEAD_SIZE
        d = y - _head_sums(y) * inv_n
        var = _head_sums(d * d) * inv_n
        return d * lax.rsqrt(var + LN_X_EPS) * lnw_ref[...] + lnb_ref[...]

    def heads_of(s):
        return s[0:HEAD_SIZE, 0:HEAD_SIZE], s[HEAD_SIZE:PAIR, HEAD_SIZE:PAIR]

    def prompt_steps():
        ws = [state[p] for p in range(N_PAIRS)]
        yns = []
        for c in range(TILE_CHUNKS):
            ys = []
            for p in range(N_PAIRS):
                w_pieces = _split2(ws[p])
                ys.append(_dot(rp_ref[chunk_rows(c), pair_lanes(p)].astype(_BF16), w_pieces[0])
                          + y0_ref[chunk_rows(c), pair_lanes(p)])
                ws[p] = _by_head_rows(_dot_pieces(_split2(mt_ref[c, p]), w_pieces) + n0_ref[c, p])
            yield
            yns.append(group_norm(ys))
            yield
        for p in range(N_PAIRS):
            state[p] = ws[p]
            s_prompt_ref[2 * p], s_prompt_ref[2 * p + 1] = heads_of(_dot_pieces((eye,), _split3(ws[p]), _NT))
        return jnp.concatenate(yns, axis=0)

    def stream_steps():
        zeros = jnp.zeros((HEAD_SIZE, HEAD_SIZE), _F32)
        yns = []
        for c in range(TILE_CHUNKS):
            ys = []
            for p in range(N_PAIRS):
                s0 = jnp.concatenate([jnp.concatenate([s0_ref[c, 2 * p], zeros], axis=1),
                                      jnp.concatenate([zeros, s0_ref[c, 2 * p + 1]], axis=1)], axis=0)
                s0_hi, s0_lo = _split2(s0)
                ys.append(_dot(rp_ref[chunk_rows(c), pair_lanes(p)].astype(_BF16), s0_hi, _NT)
                          + y0_ref[chunk_rows(c), pair_lanes(p)])
                mn = jnp.concatenate([_by_head_rows(mt_ref[c, p]), _by_head_rows(n0_ref[c, p])], axis=1)
                s_next = _dot_pieces((jnp.concatenate([s0_hi, eye], axis=1),
                                      jnp.concatenate([s0_lo, jnp.zeros_like(eye)], axis=1)),
                                     _split2(mn), _NT)
                s_sample_ref[c, 2 * p], s_sample_ref[c, 2 * p + 1] = heads_of(s_next)
            yield
            yns.append(group_norm(ys))
            yield
        return jnp.concatenate(yns, axis=0)

    def gate_steps():
        branch_a = _dot(bgy_ref[...].astype(_BF16), wc_ref[...])
        yield
        gates = []
        for c in range(TILE_CHUNKS):
            gates.append((jax.nn.sigmoid(gate_ref[chunk_rows(c), 0:D_MODEL]),
                          jax.nn.sigmoid(gate_ref[chunk_rows(c), D_MODEL:GATE_COLS])))
            yield
        gate_a = jnp.concatenate([ga for ga, _ in gates], axis=0)
        gate_b = jnp.concatenate([gb for _, gb in gates], axis=0)
        return branch_a, gate_a, gate_b

    def tile_output(recurrence_steps):
        gate_work = _Steps(gate_steps())
        yn = _zip_steps(_Steps(recurrence_steps), (gate_work, 1, 1))
        branch_a, gate_a, gate_b = gate_work.finish()
        z = yn * g_ref[...] + bg_ref[...]
        branch_b = _dot(z.astype(_BF16), wr_ref[...])
        merged = gate_a * branch_a + gate_b * branch_b
        m = _dot(merged.astype(_BF16), wo_ref[...])
        x = _load_tokens(x_refs) + _rms_scale(m) * gmix_ref[...]
        h = (_rms_scale(x) * gpre_ref[...]).astype(_BF16)
        up = _dot(h, wup_ref[...])
        act = jax.nn.silu(up[:, :D_FF]) * up[:, D_FF:]
        f = _dot(act.astype(_BF16), wdown_ref[...])
        return x + _rms_scale(f) * gpost_ref[...]

    @pl.when(jnp.logical_not(is_stream))
    def _():
        out = tile_output(prompt_steps())
        if len(out_refs) == 1:
            out_refs[0][...] = out
        else:
            out_refs[0][0] = out

    @pl.when(is_stream)
    def _():
        out = tile_output(stream_steps())
        if len(out_refs) == 1:
            out_refs[0][...] = out
        else:
            out_refs[1][...] = out.reshape(STREAMS_PER_TILE, DEC_SEQ, D_MODEL)


def _post(x_parts, layer, token_inputs, mn, state_wkv, params, weights_to_cast, split_out):
    n_steps = N_TOK // TOKEN_TILE
    cast = [_cast_specs(w, layer + 1, n_steps) for w in weights_to_cast]
    param_spec = lambda a: _layer_spec(a, layer if a.ndim == 3 else None)
    tok = lambda a: pl.BlockSpec((TOKEN_TILE, a.shape[1]), lambda i: (i, 0))
    blk = pl.BlockSpec((TILE_CHUNKS, 2, N_PAIRS, HEAD_SIZE, PAIR), lambda i: (i, 0, 0, 0, 0))
    heads = (N_HEADS, HEAD_SIZE, HEAD_SIZE)
    stream_block = lambda i: jnp.maximum(i - N_PROMPT_TILES, 0)
    st_in = pl.BlockSpec((None, STREAMS_PER_TILE) + heads, lambda i: (layer, stream_block(i), 0, 0, 0))
    st_out = pl.BlockSpec((STREAMS_PER_TILE,) + heads, lambda i: (stream_block(i), 0, 0, 0))
    if split_out:
        out_specs = _token_specs(2)
        out_shape = [jax.ShapeDtypeStruct((1, SEQ, D_MODEL), _F32),
                     jax.ShapeDtypeStruct((DEC_BATCH, DEC_SEQ, D_MODEL), _F32)]
    else:
        out_specs = _token_specs(1)
        out_shape = [jax.ShapeDtypeStruct((N_TOK, D_MODEL), _F32)]
    outs = pl.pallas_call(
        functools.partial(_post_kernel, len(x_parts), len(out_shape), len(cast)),
        grid=(n_steps,),
        in_specs=(_token_specs(len(x_parts)) + [tok(a) for a in token_inputs] + [blk, st_in]
                  + [param_spec(a) for a in params] + [c[0] for c in cast]),
        out_specs=out_specs + [pl.BlockSpec(heads, lambda i: (0, 0, 0)), st_out] + [c[1] for c in cast],
        out_shape=out_shape + [jax.ShapeDtypeStruct(heads, _F32),
                               jax.ShapeDtypeStruct((DEC_BATCH,) + heads, _F32)] + [c[2] for c in cast],
        scratch_shapes=[pltpu.VMEM((N_PAIRS, PAIR, PAIR), _F32)],
        compiler_params=pltpu.CompilerParams(
            dimension_semantics=("arbitrary",), vmem_limit_bytes=VMEM_LIMIT_BYTES),
        name="post",
    )(*x_parts, *token_inputs, mn, state_wkv, *params, *weights_to_cast)
    n_x_out = len(out_shape)
    return outs[:n_x_out], outs[n_x_out], outs[n_x_out + 1], outs[n_x_out + 2:]


def _lora_weight(w_decay2, a2, g2):
    w = jnp.zeros((DEPTH, D_LORA, 3 * D_RWKV), _BF16)
    w = w.at[:, 0:D_DECAY_LORA, 0:D_RWKV].set(w_decay2.astype(_BF16))
    w = w.at[:, D_DECAY_LORA:D_DECAY_LORA + D_AAA_LORA, D_RWKV:2 * D_RWKV].set(a2.astype(_BF16))
    return w.at[:, D_DECAY_LORA + D_AAA_LORA:, 2 * D_RWKV:].set(g2.astype(_BF16))


def kernel(x_prompt, x_sample, state_conv, state_shift, state_wkv, norm_mix_pre, norm_mix_post, w_in, mu_shift, conv_w, w_decay0, w_decay2, a0, a2, g2, k_k, k_a, r_k, ln_x_w, ln_x_b, w_conv_out, w_rwkv_out, w_o, norm_ffn_pre, norm_ffn_post, w_ffn_up, w_ffn_down):
    x_parts = (x_prompt, x_sample)
    lora_w = _lora_weight(w_decay2, a2, g2)
    by_name = dict(mu_shift=mu_shift, w_decay0=w_decay0, a0=a0, k_k=k_k, k_a=k_a, r_k=r_k.reshape(DEPTH, D_RWKV),
                   norm_mix_pre=norm_mix_pre, ln_x_w=ln_x_w, ln_x_b=ln_x_b, norm_mix_post=norm_mix_post,
                   norm_ffn_pre=norm_ffn_pre, norm_ffn_post=norm_ffn_post)
    vectors = jnp.concatenate([by_name[name] for name, _ in VECTOR_FIELDS], axis=1)[:, None, :]
    shift0 = state_shift[:, :, None]
    w_in_b = w_in[0].astype(_BF16)
    conv_p, shift_p, wkv_p, conv_s, shift_s, wkv_s = [], [], [], [], [], []
    for l in range(DEPTH):
        gates, fields, mn, conv_out, shift_out, wc, wr, wo, wup, wdown = _front(
            x_parts, l, vectors, w_in_b, state_conv, shift0, conv_w, lora_w,
            weights_to_cast=(w_conv_out, w_rwkv_out, w_o, w_ffn_up, w_ffn_down))
        last_layer = l == DEPTH - 1
        x_parts, s_prompt, s_sample, next_w_in = _post(
            x_parts, l, (fields, gates), mn, state_wkv, (vectors, wc, wr, wo, wup, wdown),
            weights_to_cast=() if last_layer else (w_in,), split_out=last_layer)
        if not last_layer:
            w_in_b, = next_w_in

        last = N_PROMPT_CHUNKS - 1
        conv_p.append(conv_out[last:last + 1])
        conv_s.append(conv_out[N_PROMPT_CHUNKS:])
        shift_p.append(shift_out[last:last + 1, 0])
        shift_s.append(shift_out[N_PROMPT_CHUNKS:, 0])
        wkv_p.append(s_prompt[None])
        wkv_s.append(s_sample)
    y_prompt, y_sample = x_parts
    return (y_prompt, y_sample,
            jnp.stack(conv_p, 0), jnp.stack(shift_p, 0), jnp.stack(wkv_p, 0),
            jnp.stack(conv_s, 0), jnp.stack(shift_s, 0), jnp.stack(wkv_s, 0))
```

```python
import functools

import jax
import jax.numpy as jnp
from jax import lax
from jax.experimental import pallas as pl
from jax.experimental.pallas import tpu as pltpu

D_MODEL = 1024
SEQ = 16384
DEPTH = 2
DEC_BATCH = 32
DEC_SEQ = 64
D_CONV = 512
D_RWKV = 512
HEAD_SIZE = 64
N_HEADS = 8
D_DECAY_LORA = 64
D_AAA_LORA = 64
D_GATE_LORA = 128
D_LORA = D_DECAY_LORA + D_AAA_LORA + D_GATE_LORA
LN_X_EPS = 64e-5
D_FF = 2816
RMS_EPS = 1e-6
RWKV_COLS = 3 * D_RWKV + D_LORA
MIX_COLS = 3 * D_CONV + RWKV_COLS
GATE_COLS = 2 * D_MODEL

CHUNK = 64
N_TOK = SEQ + DEC_BATCH * DEC_SEQ
N_CHUNKS = N_TOK // CHUNK
N_PROMPT_CHUNKS = SEQ // CHUNK
PAIR = 2 * HEAD_SIZE
N_PAIRS = N_HEADS // 2
TOKEN_TILE = 256
N_PROMPT_TILES = SEQ // TOKEN_TILE
STREAMS_PER_TILE = TOKEN_TILE // DEC_SEQ
TILE_CHUNKS = TOKEN_TILE // CHUNK
INTRA_CHUNKS = 4
INTRA_GROUP = 4
FRONT_TILE = INTRA_CHUNKS * CHUNK
PROJ_SLAB = 256
SUBLANES = 8
BF16_SUBLANES = 16
PIECES_A = 1
PIECES_INV = 1
PIECES_OUT = 1
VMEM_LIMIT_BYTES = 56 * 1024 * 1024

_BF16 = jnp.bfloat16
_F32 = jnp.float32
_NN = (((1,), (0,)), ((), ()))
_NT = (((1,), (1,)), ((), ()))
_TN = (((0,), (0,)), ((), ()))


def _dot(a, b, dims=_NN):
    return lax.dot_general(a, b, dims, preferred_element_type=_F32)


def _pieces(x, n):
    out = []
    for _ in range(n - 1):
        hi = x.astype(_BF16)
        out.append(hi)
        x = x - hi.astype(_F32)
    out.append(x.astype(_BF16))
    return tuple(out)


def _split2(x):
    return _pieces(x, 2)


def _split3(x):
    return _pieces(x, 3)


def _dot_pieces(a, b, dims=_NN):
    n = max(len(a), len(b))
    terms = [_dot(a[i], b[j], dims) for i in range(len(a)) for j in range(len(b)) if i + j < n]
    total = terms[0]
    if len(terms) > 1:
        rest = terms[1]
        for t in terms[2:]:
            rest = rest + t
        total = total + rest
    return total


class _Steps:
    def __init__(self, gen):
        self.gen, self.done, self.value = gen, False, None

    def step(self):
        if not self.done:
            try:
                next(self.gen)
            except StopIteration as stop:
                self.done, self.value = True, stop.value

    def finish(self):
        while not self.done:
            self.step()
        return self.value


def _zip_steps(main, *sides):
    count = 0
    while not main.done:
        main.step()
        count += 1
        for side, n, every in sides:
            if count % every == 0:
                for _ in range(n):
                    side.step()
    return main.value


def _rms_scale(x):
    return x * lax.rsqrt(jnp.mean(x * x, axis=-1, keepdims=True) + RMS_EPS)


def _iota(shape, dim):
    return lax.broadcasted_iota(jnp.int32, shape, dim)


def _first_head(width=PAIR):
    return _iota((1, width), 1) % PAIR < HEAD_SIZE


def _head_sums(x):
    first = _first_head()
    outs = []
    for p in range(x.shape[1] // PAIR):
        xp = x[:, p * PAIR:(p + 1) * PAIR]
        lo = jnp.sum(jnp.where(first, xp, 0.0), axis=1, keepdims=True)
        hi = jnp.sum(jnp.where(first, 0.0, xp), axis=1, keepdims=True)
        outs.append(jnp.where(first, lo, hi))
    return jnp.concatenate(outs, axis=1)


def _by_head_rows(x):
    first = _first_head()
    return jnp.concatenate([jnp.where(first, x, 0.0), jnp.where(first, 0.0, x)], axis=0)


def _diag_blocks(x):
    return jnp.where(_first_head(), x[0:HEAD_SIZE], x[HEAD_SIZE:PAIR])


def _layer_spec(a, layer):
    if layer is None:
        return pl.BlockSpec(a.shape, lambda i: (0,) * a.ndim, pipeline_mode=pl.Buffered(1))
    return pl.BlockSpec((None,) + a.shape[1:], lambda i: (layer,) + (0,) * (a.ndim - 1),
                        pipeline_mode=pl.Buffered(1))


def _cast_specs(w, layer, n_steps):
    _, k, n = w.shape
    rows = BF16_SUBLANES
    while k % rows or k // rows > n_steps:
        rows += BF16_SUBLANES
    block = lambda s: jnp.minimum(s, k // rows - 1)
    return (pl.BlockSpec((None, rows, n), lambda s: (layer, block(s), 0)),
            pl.BlockSpec((rows, n), lambda s: (block(s), 0)),
            jax.ShapeDtypeStruct((k, n), _BF16))


def _cast_blocks(in_refs, out_refs):
    for in_ref, out_ref in zip(in_refs, out_refs):
        out_ref[...] = in_ref[...].astype(_BF16)
        yield


def _token_specs(n_parts, tile=TOKEN_TILE, tile_of_step=lambda i: i):
    if n_parts == 1:
        return [pl.BlockSpec((tile, D_MODEL), lambda i: (tile_of_step(i), 0))]
    n_prompt = SEQ // tile
    return [pl.BlockSpec((1, tile, D_MODEL), lambda i: (0, jnp.minimum(tile_of_step(i), n_prompt - 1), 0)),
            pl.BlockSpec((tile // DEC_SEQ, DEC_SEQ, D_MODEL),
                         lambda i: (jnp.maximum(tile_of_step(i) - n_prompt, 0), 0, 0))]


def _load_tokens(x_refs, tile=TOKEN_TILE, tile_index=None):
    if len(x_refs) == 1:
        return x_refs[0][...]
    prompt_ref, stream_ref = x_refs
    tile_index = pl.program_id(0) if tile_index is None else tile_index
    return jnp.where(tile_index >= SEQ // tile, stream_ref[...].reshape(tile, D_MODEL), prompt_ref[0])


VECTOR_FIELDS = (("mu_shift", RWKV_COLS), ("w_decay0", D_RWKV), ("a0", D_RWKV), ("k_k", D_RWKV), ("k_a", D_RWKV),
                 ("r_k", D_RWKV), ("norm_mix_pre", D_MODEL), ("ln_x_w", D_RWKV), ("ln_x_b", D_RWKV),
                 ("norm_mix_post", D_MODEL), ("norm_ffn_pre", D_MODEL), ("norm_ffn_post", D_MODEL))


def _vector_views(vec_ref):
    views, start = {}, 0
    for name, width in VECTOR_FIELDS:
        views[name] = vec_ref.at[:, start:start + width]
        start += width
    return views


N_TOKEN_FIELDS = 5


def _token_fields(tok_ref):
    return [tok_ref.at[:, f * D_RWKV:(f + 1) * D_RWKV] for f in range(N_TOKEN_FIELDS)]


def _running_log_decay(lw):
    tri = jnp.where(_iota((CHUNK, CHUNK), 1) <= _iota((CHUNK, CHUNK), 0), 1.0, 0.0).astype(_BF16)
    return _dot_pieces((tri,), _split3(lw))


def _decayed_operands(r, kf, v, lw, cum, kk, a_sig):
    last = cum[CHUNK - 1:CHUNK]
    e_inv = jnp.exp(-cum)
    e_tail = jnp.exp(last - cum)
    b = kk * a_sig
    return dict(a_t=-kk * jnp.exp(cum - lw), r_t=r * jnp.exp(cum), b_t=b * e_inv, k_t=kf * e_inv,
                b_h=b * e_tail, k_h=kf * e_tail, v=v, dec=jnp.exp(last))


def _intra_pairs(chunks):
    first = _first_head()
    first2 = _first_head(2 * PAIR)
    g_row = _iota((2 * CHUNK, 2 * PAIR), 0)
    keep = (_iota((2 * CHUNK, 2 * PAIR), 1) % CHUNK) < (g_row % CHUNK) + g_row // CHUNK
    rc_xor = _iota((CHUNK, PAIR), 0) ^ (_iota((CHUNK, PAIR), 1) % HEAD_SIZE)
    eye = _iota((PAIR, PAIR), 0) == _iota((PAIR, PAIR), 1)
    zeros = jnp.zeros((CHUNK, PAIR), _F32)
    items = [(c, p) for c in range(len(chunks)) for p in range(N_PAIRS)]
    part = lambda it, name: chunks[it[0]][name][:, it[1] * PAIR:(it[1] + 1) * PAIR]
    out_pieces = lambda x: _pieces(x, PIECES_OUT)
    inv_pieces = lambda x: _pieces(x, PIECES_INV)

    gm, l_c, ak_c = {}, {}, {}
    for it in items:
        ar = jnp.concatenate([part(it, "a_t"), part(it, "r_t")], axis=0)
        bk = jnp.concatenate([part(it, "b_t"), part(it, "k_t")], axis=0)
        bk_heads = jnp.concatenate([jnp.where(first, bk, 0.0), jnp.where(first, 0.0, bk)], axis=0)
        gm[it] = jnp.where(keep, _dot_pieces(_pieces(ar, PIECES_A), _pieces(bk_heads, PIECES_A), _NT), 0.0)
        a_h0, a_h1 = gm[it][0:CHUNK, 0:PAIR], gm[it][0:CHUNK, PAIR:2 * PAIR]
        l_c[it] = jnp.where(first, a_h0, pltpu.roll(a_h1, HEAD_SIZE, axis=1))
        ak_c[it] = jnp.where(first, pltpu.roll(a_h0, HEAD_SIZE, axis=1), a_h1)
    yield

    inv = {it: jnp.where(rc_xor == 0, 1.0, jnp.where(rc_xor == 1, l_c[it], 0.0)) for it in items}
    akv = {it: _dot_pieces(out_pieces(ak_c[it]), out_pieces(_by_head_rows(part(it, "v")))) for it in items}
    yield
    m = 2
    while m < CHUNK:
        level = jnp.logical_and(rc_xor >= m, rc_xor < 2 * m)
        slabs = m >= SUBLANES
        n_blocks = CHUNK // (2 * m)
        if slabs:
            low = lambda a, m=m: jnp.concatenate([a[s:s + m] for s in range(m, CHUNK, 2 * m)], axis=0)
        else:
            low = lambda a: a
        inv_bd = {it: inv_pieces(_by_head_rows(inv[it])) for it in items}
        x = {it: _dot_pieces(inv_pieces(low(jnp.where(level, l_c[it], 0.0))), inv_bd[it]) for it in items}
        yield
        for it in items:
            if slabs:
                zeros_m = jnp.zeros((m, PAIR), _F32)
                x_full = jnp.concatenate([a for j in range(n_blocks)
                                          for a in (zeros_m, x[it][j * m:(j + 1) * m])], axis=0)
                upd = _dot_pieces(inv_pieces(low(inv[it])), inv_pieces(_by_head_rows(x_full)))
                inv[it] = jnp.concatenate(
                    [a for j in range(n_blocks)
                     for a in (inv[it][2 * j * m:(2 * j + 1) * m],
                               inv[it][(2 * j + 1) * m:(2 * j + 2) * m] + upd[j * m:(j + 1) * m])], axis=0)
            else:
                inv[it] = inv[it] + _dot_pieces(inv_pieces(inv[it]), inv_pieces(_by_head_rows(x[it])))
        yield
        m *= 2

    uva_pieces = {}
    for it in items:
        u0a = _dot_pieces(out_pieces(inv[it]), out_pieces(jnp.concatenate(
            [_by_head_rows(akv[it]), _by_head_rows(part(it, "a_t"))], axis=1)))
        uva = jnp.concatenate([u0a, jnp.concatenate([part(it, "v"), zeros], axis=1)], axis=0)
        uva_pieces[it] = out_pieces(uva)
    yield
    rps, y0s, mts, n0s = {}, {}, {}, {}
    for it in items:
        rr = jnp.concatenate([gm[it][CHUNK:2 * CHUNK, 0:PAIR], gm[it][CHUNK:2 * CHUNK, PAIR:2 * PAIR]],
                             axis=0)
        yr2 = _dot_pieces(out_pieces(rr), uva_pieces[it])
        yr = jnp.where(first2, yr2[0:CHUNK], yr2[CHUNK:2 * CHUNK])
        y0s[it] = yr[:, 0:PAIR]
        rps[it] = part(it, "r_t") + yr[:, PAIR:2 * PAIR]
    for it in items:
        bk_h = jnp.concatenate([part(it, "b_h"), part(it, "k_h")], axis=0)
        nm = _dot_pieces(out_pieces(bk_h), uva_pieces[it], _TN)
        n0s[it] = _diag_blocks(nm[:, 0:PAIR])
        mts[it] = _diag_blocks(nm[:, PAIR:2 * PAIR] + jnp.where(eye, part(it, "dec"), 0.0))
    cat = lambda d, c: jnp.concatenate([d[(c, p)] for p in range(N_PAIRS)], axis=1)
    return ([cat(rps, c) for c in range(len(chunks))], [cat(y0s, c) for c in range(len(chunks))], mts, n0s)


N_FRONT_PARAMS = 6
N_FRONT_OUTPUTS = 5


def _front_kernel(n_x, n_cast, *refs):
    x_refs, refs = refs[:n_x], refs[n_x:]
    vec_ref, w_ref, conv0_ref, shift0_ref, cw_ref, lora_w_ref = refs[:N_FRONT_PARAMS]
    vec = _vector_views(vec_ref)
    gpre_ref, mu_ref, wd0_ref, a0_ref = vec["norm_mix_pre"], vec["mu_shift"], vec["w_decay0"], vec["a0"]
    kk_w_ref, ka_w_ref, rk_w_ref = vec["k_k"], vec["k_a"], vec["r_k"]
    cast_in_refs, refs = refs[N_FRONT_PARAMS:N_FRONT_PARAMS + n_cast], refs[N_FRONT_PARAMS + n_cast:]
    gate_ref, tok_ref, mn_ref, conv_out_ref, shift_out_ref = refs[:N_FRONT_OUTPUTS]
    cast_out_refs = refs[N_FRONT_OUTPUTS:N_FRONT_OUTPUTS + n_cast]
    mix_next, mix_cur, hist_u_scr, hist_pr_scr = refs[N_FRONT_OUTPUTS + n_cast:]
    bgy_ref, rp_ref, y0_ref, g_ref, bg_ref = _token_fields(tok_ref)
    mt_ref, n0_ref = mn_ref.at[:, 0], mn_ref.at[:, 1]
    s = pl.program_id(0)
    is_sample = s - 1 >= N_PROMPT_CHUNKS // INTRA_CHUNKS
    row = _iota((CHUNK, 1), 0)
    cw = cw_ref[...]
    lane = _iota((CHUNK, D_LORA), 1)

    @pl.when(s == 0)
    def _():
        mix_cur[...] = jnp.zeros_like(mix_cur)
        hist_u_scr[...] = jnp.zeros_like(hist_u_scr)
        hist_pr_scr[...] = jnp.zeros_like(hist_pr_scr)

    def project():
        x = _load_tokens(x_refs, FRONT_TILE, jnp.minimum(s, N_TOK // FRONT_TILE - 1))
        h = (_rms_scale(x) * gpre_ref[...]).astype(_BF16)
        yield
        for j in range((MIX_COLS + GATE_COLS) // PROJ_SLAB):
            cols = slice(j * PROJ_SLAB, (j + 1) * PROJ_SLAB)
            p = _dot(h, w_ref[:, cols])
            if j < MIX_COLS // PROJ_SLAB:
                mix_next[:, cols] = p
            else:
                gate_ref[:, j * PROJ_SLAB - MIX_COLS:(j + 1) * PROJ_SLAB - MIX_COLS] = p
            yield

    hist = dict(u=hist_u_scr[...], pr=hist_pr_scr[...])

    def prepare(c):
        rows = slice(c * CHUNK, (c + 1) * CHUNK)
        u_hist = jnp.where(is_sample, conv0_ref[c], hist["u"])
        pr_hist = jnp.where(is_sample, shift0_ref[c], hist["pr"])

        xin = mix_cur[rows, 0:D_CONV]
        bg = mix_cur[rows, D_CONV:2 * D_CONV]
        cg = mix_cur[rows, 2 * D_CONV:3 * D_CONV]
        u = cg * xin
        u1 = jnp.where(row == 0, u_hist[1:2], pltpu.roll(u, 1, axis=0))
        u2 = jnp.where(row == 0, u_hist[0:1], jnp.where(row == 1, u_hist[1:2], pltpu.roll(u, 2, axis=0)))
        bgy_ref[rows, :] = bg * (cw[0:1] * u2 + cw[1:2] * u1 + cw[2:3] * u)
        hist["u"] = u[CHUNK - 2:CHUNK]
        conv_out_ref[c] = hist["u"]
        yield

        pr = mix_cur[rows, 3 * D_CONV:MIX_COLS]
        shifted = jnp.where(row == 0, pr_hist, pltpu.roll(pr, 1, axis=0))
        ps = pr + (shifted - pr) * mu_ref[...]
        hist["pr"] = pr[CHUNK - 1:CHUNK]
        shift_out_ref[c] = hist["pr"]
        r = ps[:, 0:D_RWKV]
        k = ps[:, D_RWKV:2 * D_RWKV]
        v = ps[:, 2 * D_RWKV:3 * D_RWKV]
        yield

        lora_in = ps[:, 3 * D_RWKV:RWKV_COLS]
        act = jnp.where(lane < D_DECAY_LORA, jnp.tanh(lora_in),
                        jnp.where(lane < D_DECAY_LORA + D_AAA_LORA, lora_in, jax.nn.sigmoid(lora_in)))
        lora = _dot(act.astype(_BF16), lora_w_ref[...])
        yield

        w_raw = wd0_ref[...] + lora[:, 0:D_RWKV]
        w_raw = -jax.nn.softplus(-w_raw) - 0.5
        lw = -jnp.exp(w_raw)
        cum = _running_log_decay(lw)
        yield

        a_sig = jax.nn.sigmoid(a0_ref[...] + lora[:, D_RWKV:2 * D_RWKV])
        g = lora[:, 2 * D_RWKV:3 * D_RWKV]
        g_ref[rows, :] = g
        kk = k * kk_w_ref[...]
        yield
        kk = kk / jnp.maximum(jnp.sqrt(_head_sums(kk * kk)), 1e-12)
        yield
        kf = k * (1.0 + (a_sig - 1.0) * ka_w_ref[...])
        bg_ref[rows, :] = _head_sums(r * kf * rk_w_ref[...]) * v * g
        yield
        return _decayed_operands(r, kf, v, lw, cum, kk, a_sig)

    def prepare_group(first_chunk):
        ops = []
        for c in range(first_chunk, first_chunk + INTRA_GROUP):
            ops.append((yield from prepare(c)))
        return ops

    def store(first_chunk, results):
        rps, y0s, mts, n0s = results
        for g in range(INTRA_GROUP):
            c = first_chunk + g
            rows = slice(c * CHUNK, (c + 1) * CHUNK)
            rp_ref[rows, :] = rps[g]
            y0_ref[rows, :] = y0s[g]
            for p in range(N_PAIRS):
                mt_ref[c, p] = mts[(g, p)]
                n0_ref[c, p] = n0s[(g, p)]

    def hand_over():
        for j in range(MIX_COLS // PROJ_SLAB):
            cols = slice(j * PROJ_SLAB, (j + 1) * PROJ_SLAB)
            mix_cur[:, cols] = mix_next[:, cols]
            yield

    assert INTRA_CHUNKS == INTRA_GROUP
    n_mix_steps = 1 + MIX_COLS // PROJ_SLAB
    n_prepare_steps = 7 * INTRA_GROUP
    assert n_prepare_steps % n_mix_steps == 0
    projection = _Steps(project())
    operands = _zip_steps(_Steps(prepare_group(0)), (projection, 1, n_prepare_steps // n_mix_steps))
    hist_u_scr[...] = hist["u"]
    hist_pr_scr[...] = hist["pr"]
    copy = _Steps(hand_over())
    casts = _Steps(_cast_blocks(cast_in_refs, cast_out_refs))
    results = _zip_steps(_Steps(_intra_pairs(operands)), (projection, 1, 1), (copy, 1, 1), (casts, 1, 2))
    projection.finish()
    copy.finish()
    casts.finish()
    store(0, results)


def _front(x_parts, layer, vectors, w_in_bf16, conv0, shift0, cw, lora_w, weights_to_cast):
    n_tiles = N_TOK // FRONT_TILE
    cast_in, cast_out, cast_shape = zip(*[_cast_specs(w, layer, n_tiles + 1) for w in weights_to_cast])
    n_prompt_tiles = SEQ // FRONT_TILE
    projected = lambda s: jnp.minimum(s, n_tiles - 1)
    prepared = lambda s: jnp.maximum(s - 1, 0)
    tok = lambda width: pl.BlockSpec((FRONT_TILE, width), lambda s: (prepared(s), 0))
    full = lambda a: _layer_spec(a, layer)
    seq = lambda a: pl.BlockSpec((None, INTRA_CHUNKS) + a.shape[2:],
                                 lambda s: (layer, jnp.maximum(prepared(s) - n_prompt_tiles, 0), 0, 0))
    per_chunk = lambda *dims: pl.BlockSpec((INTRA_CHUNKS,) + dims, lambda s: (prepared(s),) + (0,) * len(dims))
    return pl.pallas_call(
        functools.partial(_front_kernel, len(x_parts), len(weights_to_cast)),
        grid=(n_tiles + 1,),
        in_specs=(_token_specs(len(x_parts), FRONT_TILE, projected)
                  + [full(vectors), _layer_spec(w_in_bf16, None), seq(conv0), seq(shift0), full(cw), full(lora_w)]
                  + list(cast_in)),
        out_specs=[pl.BlockSpec((FRONT_TILE, GATE_COLS), lambda s: (projected(s), 0)),
                   tok(N_TOKEN_FIELDS * D_RWKV), per_chunk(2, N_PAIRS, HEAD_SIZE, PAIR),
                   per_chunk(2, D_CONV), per_chunk(1, RWKV_COLS)] + list(cast_out),
        out_shape=[jax.ShapeDtypeStruct((N_TOK, GATE_COLS), _F32),
                   jax.ShapeDtypeStruct((N_TOK, N_TOKEN_FIELDS * D_RWKV), _F32),
                   jax.ShapeDtypeStruct((N_CHUNKS, 2, N_PAIRS, HEAD_SIZE, PAIR), _F32),
                   jax.ShapeDtypeStruct((N_CHUNKS, 2, D_CONV), _F32),
                   jax.ShapeDtypeStruct((N_CHUNKS, 1, RWKV_COLS), _F32)] + list(cast_shape),
        scratch_shapes=[pltpu.VMEM((FRONT_TILE, MIX_COLS), _F32), pltpu.VMEM((FRONT_TILE, MIX_COLS), _F32),
                        pltpu.VMEM((2, D_CONV), _F32), pltpu.VMEM((1, RWKV_COLS), _F32)],
        compiler_params=pltpu.CompilerParams(
            dimension_semantics=("arbitrary",), vmem_limit_bytes=VMEM_LIMIT_BYTES),
        name="front",
    )(*x_parts, vectors, w_in_bf16, conv0, shift0, cw, lora_w, *weights_to_cast)


N_POST_INPUTS = 10


def _post_kernel(n_x, n_out, n_cast, *refs):
    x_refs, refs = refs[:n_x], refs[n_x:]
    (tok_ref, gate_ref, mn_ref, s0_ref, vec_ref,
     wc_ref, wr_ref, wo_ref, wup_ref, wdown_ref) = refs[:N_POST_INPUTS]
    vec = _vector_views(vec_ref)
    lnw_ref, lnb_ref, gmix_ref = vec["ln_x_w"], vec["ln_x_b"], vec["norm_mix_post"]
    gpre_ref, gpost_ref = vec["norm_ffn_pre"], vec["norm_ffn_post"]
    cast_in_refs, refs = refs[N_POST_INPUTS:N_POST_INPUTS + n_cast], refs[N_POST_INPUTS + n_cast:]
    out_refs, (s_prompt_ref, s_sample_ref) = refs[:n_out], refs[n_out:n_out + 2]
    cast_out_refs, (state,) = refs[n_out + 2:n_out + 2 + n_cast], refs[n_out + 2 + n_cast:]
    _Steps(_cast_blocks(cast_in_refs, cast_out_refs)).finish()
    bgy_ref, rp_ref, y0_ref, g_ref, bg_ref = _token_fields(tok_ref)
    mt_ref, n0_ref = mn_ref.at[:, 0], mn_ref.at[:, 1]
    i = pl.program_id(0)
    is_stream = i >= N_PROMPT_TILES

    @pl.when(i == 0)
    def _():
        state[...] = jnp.zeros_like(state)

    eye = (_iota((PAIR, PAIR), 0) == _iota((PAIR, PAIR), 1)).astype(_BF16)
    chunk_rows = lambda c: slice(c * CHUNK, (c + 1) * CHUNK)
    pair_lanes = lambda p: slice(p * PAIR, (p + 1) * PAIR)

    def group_norm(ys):
        y = jnp.concatenate(ys, axis=1)
        inv_n = 1.0 / HEAD_SIZE
        d = y - _head_sums(y) * inv_n
        var = _head_sums(d * d) * inv_n
        return d * lax.rsqrt(var + LN_X_EPS) * lnw_ref[...] + lnb_ref[...]

    def heads_of(s):
        return s[0:HEAD_SIZE, 0:HEAD_SIZE], s[HEAD_SIZE:PAIR, HEAD_SIZE:PAIR]

    def prompt_steps():
        ws = [state[p] for p in range(N_PAIRS)]
        yns = []
        for c in range(TILE_CHUNKS):
            ys = []
            for p in range(N_PAIRS):
                w_pieces = _split2(ws[p])
                ys.append(_dot(rp_ref[chunk_rows(c), pair_lanes(p)].astype(_BF16), w_pieces[0])
                          + y0_ref[chunk_rows(c), pair_lanes(p)])
                ws[p] = _by_head_rows(_dot_pieces(_split2(mt_ref[c, p]), w_pieces) + n0_ref[c, p])
            yield
            yns.append(group_norm(ys))
            yield
        for p in range(N_PAIRS):
            state[p] = ws[p]
            s_prompt_ref[p] = _diag_blocks(ws[p])
        return jnp.concatenate(yns, axis=0)

    def stream_steps():
        zeros = jnp.zeros((HEAD_SIZE, HEAD_SIZE), _F32)
        yns = []
        for c in range(TILE_CHUNKS):
            ys = []
            for p in range(N_PAIRS):
                s0 = jnp.concatenate([jnp.concatenate([s0_ref[c, 2 * p], zeros], axis=1),
                                      jnp.concatenate([zeros, s0_ref[c, 2 * p + 1]], axis=1)], axis=0)
                s0_hi, s0_lo = _split2(s0)
                ys.append(_dot(rp_ref[chunk_rows(c), pair_lanes(p)].astype(_BF16), s0_hi, _NT)
                          + y0_ref[chunk_rows(c), pair_lanes(p)])
                mn = jnp.concatenate([_by_head_rows(mt_ref[c, p]), _by_head_rows(n0_ref[c, p])], axis=1)
                s_next = _dot_pieces((jnp.concatenate([s0_hi, eye], axis=1),
                                      jnp.concatenate([s0_lo, jnp.zeros_like(eye)], axis=1)),
                                     _split2(mn), _NT)
                s_sample_ref[c, 2 * p], s_sample_ref[c, 2 * p + 1] = heads_of(s_next)
            yield
            yns.append(group_norm(ys))
            yield
        return jnp.concatenate(yns, axis=0)

    def gate_steps():
        branch_a = _dot(bgy_ref[...].astype(_BF16), wc_ref[...])
        yield
        gates = []
        for c in range(TILE_CHUNKS):
            gates.append((jax.nn.sigmoid(gate_ref[chunk_rows(c), 0:D_MODEL]),
                          jax.nn.sigmoid(gate_ref[chunk_rows(c), D_MODEL:GATE_COLS])))
            yield
        gate_a = jnp.concatenate([ga for ga, _ in gates], axis=0)
        gate_b = jnp.concatenate([gb for _, gb in gates], axis=0)
        return branch_a, gate_a, gate_b

    def tile_output(recurrence_steps):
        gate_work = _Steps(gate_steps())
        yn = _zip_steps(_Steps(recurrence_steps), (gate_work, 1, 1))
        branch_a, gate_a, gate_b = gate_work.finish()
        z = yn * g_ref[...] + bg_ref[...]
        branch_b = _dot(z.astype(_BF16), wr_ref[...])
        merged = gate_a * branch_a + gate_b * branch_b
        m = _dot(merged.astype(_BF16), wo_ref[...])
        x = _load_tokens(x_refs) + _rms_scale(m) * gmix_ref[...]
        h = (_rms_scale(x) * gpre_ref[...]).astype(_BF16)
        up = _dot(h, wup_ref[...])
        act = jax.nn.silu(up[:, :D_FF]) * up[:, D_FF:]
        f = _dot(act.astype(_BF16), wdown_ref[...])
        return x + _rms_scale(f) * gpost_ref[...]

    @pl.when(jnp.logical_not(is_stream))
    def _():
        out = tile_output(prompt_steps())
        if len(out_refs) == 1:
            out_refs[0][...] = out
        else:
            out_refs[0][0] = out

    @pl.when(is_stream)
    def _():
        out = tile_output(stream_steps())
        if len(out_refs) == 1:
            out_refs[0][...] = out
        else:
            out_refs[1][...] = out.reshape(STREAMS_PER_TILE, DEC_SEQ, D_MODEL)


def _post(x_parts, layer, token_inputs, mn, state_wkv, params, weights_to_cast, split_out):
    n_steps = N_TOK // TOKEN_TILE
    cast = [_cast_specs(w, layer + 1, n_steps) for w in weights_to_cast]
    param_spec = lambda a: _layer_spec(a, layer if a.ndim == 3 else None)
    tok = lambda a: pl.BlockSpec((TOKEN_TILE, a.shape[1]), lambda i: (i, 0))
    blk = pl.BlockSpec((TILE_CHUNKS, 2, N_PAIRS, HEAD_SIZE, PAIR), lambda i: (i, 0, 0, 0, 0))
    heads = (N_HEADS, HEAD_SIZE, HEAD_SIZE)
    stream_block = lambda i: jnp.maximum(i - N_PROMPT_TILES, 0)
    st_in = pl.BlockSpec((None, STREAMS_PER_TILE) + heads, lambda i: (layer, stream_block(i), 0, 0, 0))
    st_out = pl.BlockSpec((STREAMS_PER_TILE,) + heads, lambda i: (stream_block(i), 0, 0, 0))
    if split_out:
        out_specs = _token_specs(2)
        out_shape = [jax.ShapeDtypeStruct((1, SEQ, D_MODEL), _F32),
                     jax.ShapeDtypeStruct((DEC_BATCH, DEC_SEQ, D_MODEL), _F32)]
    else:
        out_specs = _token_specs(1)
        out_shape = [jax.ShapeDtypeStruct((N_TOK, D_MODEL), _F32)]
    outs = pl.pallas_call(
        functools.partial(_post_kernel, len(x_parts), len(out_shape), len(cast)),
        grid=(n_steps,),
        in_specs=(_token_specs(len(x_parts)) + [tok(a) for a in token_inputs] + [blk, st_in]
                  + [param_spec(a) for a in params] + [c[0] for c in cast]),
        out_specs=(out_specs + [pl.BlockSpec((N_PAIRS, HEAD_SIZE, PAIR), lambda i: (0, 0, 0)), st_out]
                   + [c[1] for c in cast]),
        out_shape=out_shape + [jax.ShapeDtypeStruct((N_PAIRS, HEAD_SIZE, PAIR), _F32),
                               jax.ShapeDtypeStruct((DEC_BATCH,) + heads, _F32)] + [c[2] for c in cast],
        scratch_shapes=[pltpu.VMEM((N_PAIRS, PAIR, PAIR), _F32)],
        compiler_params=pltpu.CompilerParams(
            dimension_semantics=("arbitrary",), vmem_limit_bytes=VMEM_LIMIT_BYTES),
        name="post",
    )(*x_parts, *token_inputs, mn, state_wkv, *params, *weights_to_cast)
    n_x_out = len(out_shape)
    return outs[:n_x_out], outs[n_x_out], outs[n_x_out + 1], outs[n_x_out + 2:]


def _lora_weight(w_decay2, a2, g2):
    w = jnp.zeros((DEPTH, D_LORA, 3 * D_RWKV), _BF16)
    w = w.at[:, 0:D_DECAY_LORA, 0:D_RWKV].set(w_decay2.astype(_BF16))
    w = w.at[:, D_DECAY_LORA:D_DECAY_LORA + D_AAA_LORA, D_RWKV:2 * D_RWKV].set(a2.astype(_BF16))
    return w.at[:, D_DECAY_LORA + D_AAA_LORA:, 2 * D_RWKV:].set(g2.astype(_BF16))


def kernel(x_prompt, x_sample, state_conv, state_shift, state_wkv, norm_mix_pre, norm_mix_post, w_in, mu_shift, conv_w, w_decay0, w_decay2, a0, a2, g2, k_k, k_a, r_k, ln_x_w, ln_x_b, w_conv_out, w_rwkv_out, w_o, norm_ffn_pre, norm_ffn_post, w_ffn_up, w_ffn_down):
    x_parts = (x_prompt, x_sample)
    lora_w = _lora_weight(w_decay2, a2, g2)
    by_name = dict(mu_shift=mu_shift, w_decay0=w_decay0, a0=a0, k_k=k_k, k_a=k_a, r_k=r_k.reshape(DEPTH, D_RWKV),
                   norm_mix_pre=norm_mix_pre, ln_x_w=ln_x_w, ln_x_b=ln_x_b, norm_mix_post=norm_mix_post,
                   norm_ffn_pre=norm_ffn_pre, norm_ffn_post=norm_ffn_post)
    vectors = jnp.concatenate([by_name[name] for name, _ in VECTOR_FIELDS], axis=1)[:, None, :]
    shift0 = state_shift[:, :, None]
    w_in_b = w_in[0].astype(_BF16)
    conv_p, shift_p, wkv_p, conv_s, shift_s, wkv_s = [], [], [], [], [], []
    for l in range(DEPTH):
        gates, fields, mn, conv_out, shift_out, wc, wr, wo, wup, wdown = _front(
            x_parts, l, vectors, w_in_b, state_conv, shift0, conv_w, lora_w,
            weights_to_cast=(w_conv_out, w_rwkv_out, w_o, w_ffn_up, w_ffn_down))
        last_layer = l == DEPTH - 1
        x_parts, s_prompt, s_sample, next_w_in = _post(
            x_parts, l, (fields, gates), mn, state_wkv, (vectors, wc, wr, wo, wup, wdown),
            weights_to_cast=() if last_layer else (w_in,), split_out=last_layer)
        if not last_layer:
            w_in_b, = next_w_in

        last = N_PROMPT_CHUNKS - 1
        conv_p.append(conv_out[last:last + 1])
        conv_s.append(conv_out[N_PROMPT_CHUNKS:])
        shift_p.append(shift_out[last:last + 1, 0])
        shift_s.append(shift_out[N_PROMPT_CHUNKS:, 0])
        wkv_p.append(jnp.transpose(s_prompt.reshape(N_PAIRS, HEAD_SIZE, 2, HEAD_SIZE), (0, 2, 3, 1))
                     .reshape(1, N_HEADS, HEAD_SIZE, HEAD_SIZE))
        wkv_s.append(s_sample)
    y_prompt, y_sample = x_parts
    return (y_prompt, y_sample,
            jnp.stack(conv_p, 0), jnp.stack(shift_p, 0), jnp.stack(wkv_p, 0),
            jnp.stack(conv_s, 0), jnp.stack(shift_s, 0), jnp.stack(wkv_s, 0))
```
